```python
import math
import jax, jax.numpy as jnp
from jax import lax
import numpy as np

D_MODEL = 2048
BATCH = 16
SEQ = 256
DEPTH = 2
DEC_BATCH = 4
DEC_SEQ = 1024
PAST_LEN = 256

GRID_W = 64
HEAD_DIM = 128
BLOCK = 128
ROPE_THETA = 10000.0
EPS = 1e-6
A_HEADS = 8
A_KV = 2
A_WINDOW = 128
C_HEADS = 8
C_KV = 2
SSM_HEADS = 32
SSM_HEAD_DIM = 64
SSM_GROUPS = 2
SSM_STATE = 128
SSM_CHUNK = 128
CONV_K = 3
D_FF = 4 * D_MODEL

A_QW = A_HEADS * HEAD_DIM
A_KVW = A_KV * HEAD_DIM
C_QW = C_HEADS * HEAD_DIM
C_KVW = C_KV * HEAD_DIM
SSM_INNER = SSM_HEADS * SSM_HEAD_DIM
CONV_CH = SSM_INNER + 2 * SSM_GROUPS * SSM_STATE
N_IN = A_QW + 2 * A_KVW + SSM_INNER + CONV_CH + 2 * SSM_HEADS + C_QW + 2 * C_KVW + 3 * D_MODEL

kernel_name = 'hybrid_diffusion_prefix_trunk_step'


def rms_norm(x, g):
    xf = x.astype(jnp.float32)
    y = xf * lax.rsqrt(jnp.mean(xf * xf, axis=-1, keepdims=True) + EPS)
    return (y * g.astype(jnp.float32)).astype(x.dtype)


def axial_rope_tables(n_tokens):
    rows = n_tokens // GRID_W
    row = jnp.repeat(jnp.arange(rows, dtype=jnp.float32), GRID_W)
    col = jnp.tile(jnp.arange(GRID_W, dtype=jnp.float32), rows)
    n_freq = HEAD_DIM // 4
    inv_freq = ROPE_THETA ** (-jnp.arange(n_freq, dtype=jnp.float32) / n_freq)
    ang = jnp.concatenate([row[:, None] * inv_freq, col[:, None] * inv_freq], axis=-1)
    return jnp.cos(ang), jnp.sin(ang)


def apply_rope(x, cos, sin):
    half = HEAD_DIM // 2
    xf = x.astype(jnp.float32)
    x1, x2 = xf[..., :half], xf[..., half:]
    c = cos[None, :, None, :]
    s = sin[None, :, None, :]
    return jnp.concatenate([x1 * c - x2 * s, x2 * c + x1 * s], axis=-1).astype(x.dtype)


def dense_attention(q, k, v, sink):
    b, lq, nh, hd = q.shape
    nkv = k.shape[2]
    g = nh // nkv
    nb = lq // BLOCK
    scale = hd ** -0.5
    qb = q.reshape(b, nb, BLOCK, nkv, g, hd).transpose(1, 0, 2, 3, 4, 5)

    def one_block(qi):
        s = jnp.einsum('bqhgd,bkhd->bhgqk', qi, k).astype(jnp.float32) * scale
        if sink is not None:
            sk = jnp.broadcast_to(sink.astype(jnp.float32).reshape(1, nkv, g, 1, 1), s.shape[:-1] + (1,))
            p = jax.nn.softmax(jnp.concatenate([s, sk], axis=-1), axis=-1)[..., :-1]
        else:
            p = jax.nn.softmax(s, axis=-1)
        return jnp.einsum('bhgqk,bkhd->bqhgd', p.astype(v.dtype), v)

    out = lax.map(one_block, qb)
    return out.transpose(1, 0, 2, 3, 4, 5).reshape(b, lq, nh, hd)


def banded_attention_with_context(q, k, v, k_ctx, v_ctx, sink):
    b, l, nh, hd = q.shape
    nkv = k.shape[2]
    g = nh // nkv
    nb = l // BLOCK
    scale = hd ** -0.5
    qb = q.reshape(b, nb, BLOCK, nkv, g, hd)
    pad = jnp.zeros((b, BLOCK, nkv, hd), k.dtype)

    def windows(t):
        tb = jnp.concatenate([pad, t, pad], axis=1).reshape(b, nb + 2, BLOCK, nkv, hd)
        return jnp.concatenate([tb[:, :-2], tb[:, 1:-1], tb[:, 2:]], axis=2)

    kw, vw = windows(k), windows(v)
    s_loc = jnp.einsum('bnqhgd,bnkhd->bnhgqk', qb, kw).astype(jnp.float32) * scale
    blk = jnp.arange(nb)[:, None, None]
    qpos = blk * BLOCK + jnp.arange(BLOCK)[None, :, None]
    kpos = (blk - 1) * BLOCK + jnp.arange(3 * BLOCK)[None, None, :]
    valid = (jnp.abs(kpos - qpos) <= A_WINDOW) & (kpos >= 0) & (kpos < l)
    s_loc = jnp.where(valid[None, :, None, None], s_loc, -jnp.inf)
    s_ctx = jnp.einsum('bnqhgd,bkhd->bnhgqk', qb, k_ctx).astype(jnp.float32) * scale
    sk = jnp.broadcast_to(sink.astype(jnp.float32).reshape(1, 1, nkv, g, 1, 1), s_loc.shape[:-1] + (1,))
    p = jax.nn.softmax(jnp.concatenate([s_ctx, s_loc, sk], axis=-1), axis=-1)
    n_ctx = k_ctx.shape[1]
    p_ctx = p[..., :n_ctx].astype(v.dtype)
    p_loc = p[..., n_ctx:n_ctx + 3 * BLOCK].astype(v.dtype)
    out = jnp.einsum('bnhgqk,bkhd->bnqhgd', p_ctx, v_ctx) + jnp.einsum('bnhgqk,bnkhd->bnqhgd', p_loc, vw)
    return out.reshape(b, l, nh, hd)


def depthwise_conv_silu(u, w, bias):
    out = lax.conv_general_dilated(u, w[:, None, :], window_strides=(1,), padding=[(CONV_K // 2, CONV_K // 2)],
                                   dimension_numbers=('NWC', 'WIO', 'NWC'), feature_group_count=u.shape[-1])
    return jax.nn.silu(out + bias)


def ssd_scan(x, dt, a_log, bm, cm, h0):
    b, l, nh, p = x.shape
    ng, n = bm.shape[2], bm.shape[3]
    hg = nh // ng
    q = SSM_CHUNK
    nc = l // q
    f32 = jnp.float32
    a = dt * (-jnp.exp(a_log.astype(f32)))
    a = a.reshape(b, nc, q, ng, hg).transpose(0, 3, 4, 1, 2)
    xdt = (x.astype(f32) * dt[..., None]).reshape(b, nc, q, ng, hg, p)
    bm = bm.astype(f32).reshape(b, nc, q, ng, n)
    cm = cm.astype(f32).reshape(b, nc, q, ng, n)
    a_cum = jnp.cumsum(a, axis=-1)
    lower = jnp.tril(jnp.ones((q, q), dtype=bool))
    seg = jnp.exp(jnp.where(lower, a_cum[..., :, None] - a_cum[..., None, :], -jnp.inf))
    cb = jnp.einsum('bclgn,bcsgn->bgcls', cm, bm)
    y_diag = jnp.einsum('bgjcls,bcsgjp->bclgjp', cb[:, :, None] * seg, xdt)
    to_end = jnp.exp(a_cum[..., -1:] - a_cum).transpose(0, 3, 4, 1, 2)
    chunk_states = jnp.einsum('bcsgn,bcsgjp->cbgjpn', bm, xdt * to_end[..., None])
    chunk_decay = jnp.exp(a_cum[..., -1]).transpose(3, 0, 1, 2)

    def step(h, inp):
        st, dc = inp
        return dc[..., None, None] * h + st, h

    h_fin, h_enter = lax.scan(step, h0.astype(f32).reshape(b, ng, hg, p, n), (chunk_states, chunk_decay))
    from_start = jnp.exp(a_cum).transpose(0, 3, 4, 1, 2)
    y_off = jnp.einsum('bclgn,cbgjpn->bclgjp', cm, h_enter) * from_start[..., None]
    return (y_diag + y_off).reshape(b, l, nh, p), h_fin.reshape(b, nh, p, n)


def ssd_bidirectional(xbc, dt_raw, z, h0, lp):
    b, l, _ = xbc.shape
    gn = SSM_GROUPS * SSM_STATE
    xs = xbc[..., :SSM_INNER].reshape(b, l, SSM_HEADS, SSM_HEAD_DIM)
    bm = xbc[..., SSM_INNER:SSM_INNER + gn].reshape(b, l, SSM_GROUPS, SSM_STATE)
    cm = xbc[..., SSM_INNER + gn:].reshape(b, l, SSM_GROUPS, SSM_STATE)
    dt = jax.nn.softplus(dt_raw.reshape(b, l, 2, SSM_HEADS).astype(jnp.float32) + lp['ssm_dt_bias'].astype(jnp.float32))
    y_f, h_f = ssd_scan(xs, dt[:, :, 0], lp['ssm_a_log'][0], bm, cm, h0[:, 0])
    y_b, h_b = ssd_scan(jnp.flip(xs, 1), jnp.flip(dt[:, :, 1], 1), lp['ssm_a_log'][1],
                        jnp.flip(bm, 1), jnp.flip(cm, 1), h0[:, 1])
    y = y_f + jnp.flip(y_b, 1) + lp['ssm_d'].astype(jnp.float32)[:, None] * xs.astype(jnp.float32)
    y = y.reshape(b, l, SSM_INNER) * jax.nn.silu(z.astype(jnp.float32))
    y = rms_norm(y, lp['ssm_norm_g']).astype(xbc.dtype)
    return y, jnp.stack([h_f, h_b], axis=1)


def split_projection(pr):
    sizes = [A_QW, A_KVW, A_KVW, SSM_INNER, CONV_CH, 2 * SSM_HEADS, C_QW, C_KVW, C_KVW, D_MODEL, D_MODEL, D_MODEL]
    idx = np.cumsum(sizes)[:-1].tolist()
    return jnp.split(pr, idx, axis=-1)


def modulation(cond, lp):
    mod = jax.nn.silu(cond) @ lp['w_ada'] + lp['b_ada']
    return jnp.split(mod, 6, axis=-1)


def mixer_inputs(x, sh1, sc1, lp):
    h = rms_norm(x, lp['norm1_g']) * (1 + sc1) + sh1
    return split_projection(h @ lp['w_in'])


def merge_and_ffn(x, ya, yb, yc, ga, gb, gc, g1, sh2, sc2, g2, lp):
    b, l, _ = x.shape
    br_a = ya.reshape(b, l, A_QW) @ lp['w_oa']
    br_b = yb @ lp['w_ob']
    br_c = yc.reshape(b, l, C_QW) @ lp['w_oc']
    merged = jax.nn.sigmoid(ga) * br_a + jax.nn.sigmoid(gb) * br_b + jax.nn.sigmoid(gc) * br_c
    x = x + g1 * (merged @ lp['w_out'])
    h2 = rms_norm(x, lp['norm2_g']) * (1 + sc2) + sh2
    f = jnp.square(jax.nn.relu(h2 @ lp['w_mlp1'])) @ lp['w_mlp2']
    return x + g2 * f


def context_layer(x, c_ctx, lp):
    b, l, _ = x.shape
    sh1, sc1, g1, sh2, sc2, g2 = modulation(c_ctx, lp)
    qa, ka, va, z, xbc, dt_raw, qc, kc, vc, ga, gb, gc = mixer_inputs(x, sh1, sc1, lp)
    ka = ka.reshape(b, l, A_KV, HEAD_DIM)
    va = va.reshape(b, l, A_KV, HEAD_DIM)
    ya = dense_attention(qa.reshape(b, l, A_HEADS, HEAD_DIM), ka, va, lp['a_sink'])
    xbc = depthwise_conv_silu(xbc, lp['conv_w'], lp['conv_b'])
    h0 = jnp.zeros((b, 2, SSM_HEADS, SSM_HEAD_DIM, SSM_STATE), jnp.float32)
    yb, ssm_state = ssd_bidirectional(xbc, dt_raw, z, h0, lp)
    qc = rms_norm(qc.reshape(b, l, C_HEADS, HEAD_DIM), lp['c_q_norm'])
    kc = rms_norm(kc.reshape(b, l, C_KV, HEAD_DIM), lp['c_k_norm'])
    vc = vc.reshape(b, l, C_KV, HEAD_DIM)
    yc = dense_attention(qc, kc, vc, None)
    x = merge_and_ffn(x, ya, yb, yc, ga, gb, gc, g1, sh2, sc2, g2, lp)
    return x, (ka, va, kc, vc, ssm_state.astype(x.dtype))


def latent_layer(x, c, ak_ctx, av_ctx, ck_ctx, cv_ctx, h0, cos, sin, lp):
    b, l, _ = x.shape
    sh1, sc1, g1, sh2, sc2, g2 = [m[:, None, :] for m in modulation(c, lp)]
    qa, ka, va, z, xbc, dt_raw, qc, kc, vc, ga, gb, gc = mixer_inputs(x, sh1, sc1, lp)
    qa = apply_rope(qa.reshape(b, l, A_HEADS, HEAD_DIM), cos, sin)
    ka = apply_rope(ka.reshape(b, l, A_KV, HEAD_DIM), cos, sin)
    va = va.reshape(b, l, A_KV, HEAD_DIM)
    ya = banded_attention_with_context(qa, ka, va, ak_ctx, av_ctx, lp['a_sink'])
    xbc = depthwise_conv_silu(xbc, lp['conv_w'], lp['conv_b'])
    yb, _ = ssd_bidirectional(xbc, dt_raw, z, h0, lp)
    qc = apply_rope(rms_norm(qc.reshape(b, l, C_HEADS, HEAD_DIM), lp['c_q_norm']), cos, sin)
    kc = apply_rope(rms_norm(kc.reshape(b, l, C_KV, HEAD_DIM), lp['c_k_norm']), cos, sin)
    vc = vc.reshape(b, l, C_KV, HEAD_DIM)
    yc = dense_attention(qc, jnp.concatenate([ck_ctx, kc], axis=1), jnp.concatenate([cv_ctx, vc], axis=1), None)
    return merge_and_ffn(x, ya, yb, yc, ga, gb, gc, g1, sh2, sc2, g2, lp)


def setup_inputs(seed: int = 0) -> dict:
    key = jax.random.key(seed)
    ks = jax.random.split(key, 32)
    f32 = jnp.float32

    def nrm(k, shape, scale):
        return scale * jax.random.normal(k, shape, f32)

    dt0 = jnp.exp(jax.random.uniform(ks[16], (DEPTH, 2, SSM_HEADS), f32, math.log(1e-3), math.log(1e-1)))
    return {
        'x_prompt': nrm(ks[0], (BATCH, SEQ, D_MODEL), 1.0),
        'x_sample': nrm(ks[1], (DEC_BATCH, DEC_SEQ, D_MODEL), 1.0),
        'cache_a_k': nrm(ks[2], (DEC_BATCH, DEPTH, PAST_LEN, A_KV, HEAD_DIM), 1.0),
        'cache_a_v': nrm(ks[3], (DEC_BATCH, DEPTH, PAST_LEN, A_KV, HEAD_DIM), 1.0),
        'cache_c_k': nrm(ks[4], (DEC_BATCH, DEPTH, PAST_LEN, C_KV, HEAD_DIM), 1.0),
        'cache_c_v': nrm(ks[5], (DEC_BATCH, DEPTH, PAST_LEN, C_KV, HEAD_DIM), 1.0),
        'state_ssm': nrm(ks[6], (DEC_BATCH, DEPTH, 2, SSM_HEADS, SSM_HEAD_DIM, SSM_STATE), 0.1),
        'c': nrm(ks[7], (DEC_BATCH, D_MODEL), 1.0),
        'c_ctx': nrm(ks[8], (D_MODEL,), 1.0),
        'norm1_g': 1.0 + nrm(ks[9], (DEPTH, D_MODEL), 0.02),
        'w_ada': nrm(ks[10], (DEPTH, D_MODEL, 6 * D_MODEL), 0.5 * D_MODEL ** -0.5),
        'b_ada': nrm(ks[11], (DEPTH, 6 * D_MODEL), 0.02),
        'w_in': nrm(ks[12], (DEPTH, D_MODEL, N_IN), D_MODEL ** -0.5),
        'a_sink': nrm(ks[13], (DEPTH, A_HEADS), 0.5),
        'conv_w': nrm(ks[14], (DEPTH, CONV_K, CONV_CH), CONV_K ** -0.5),
        'conv_b': nrm(ks[15], (DEPTH, CONV_CH), 0.01),
        'ssm_a_log': jnp.log(jax.random.uniform(ks[17], (DEPTH, 2, SSM_HEADS), f32, 1.0, 16.0)),
        'ssm_dt_bias': dt0 + jnp.log(-jnp.expm1(-dt0)),
        'ssm_d': 1.0 + nrm(ks[18], (DEPTH, SSM_HEADS), 0.1),
        'ssm_norm_g': 1.0 + nrm(ks[19], (DEPTH, SSM_INNER), 0.02),
        'c_q_norm': 1.0 + nrm(ks[20], (DEPTH, HEAD_DIM), 0.02),
        'c_k_norm': 1.0 + nrm(ks[21], (DEPTH, HEAD_DIM), 0.02),
        'w_oa': nrm(ks[22], (DEPTH, A_QW, D_MODEL), A_QW ** -0.5),
        'w_ob': nrm(ks[23], (DEPTH, SSM_INNER, D_MODEL), SSM_INNER ** -0.5),
        'w_oc': nrm(ks[24], (DEPTH, C_QW, D_MODEL), C_QW ** -0.5),
        'w_out': nrm(ks[25], (DEPTH, D_MODEL, D_MODEL), D_MODEL ** -0.5),
        'norm2_g': 1.0 + nrm(ks[26], (DEPTH, D_MODEL), 0.02),
        'w_mlp1': nrm(ks[27], (DEPTH, D_MODEL, D_FF), D_MODEL ** -0.5),
        'w_mlp2': nrm(ks[28], (DEPTH, D_FF, D_MODEL), D_FF ** -0.5),
        'final_norm_g': 1.0 + nrm(ks[29], (D_MODEL,), 0.02),
    }


def reference(x_prompt, x_sample, cache_a_k, cache_a_v, cache_c_k, cache_c_v, state_ssm, c, c_ctx,
              norm1_g, w_ada, b_ada, w_in, a_sink, conv_w, conv_b, ssm_a_log, ssm_dt_bias, ssm_d, ssm_norm_g,
              c_q_norm, c_k_norm, w_oa, w_ob, w_oc, w_out, norm2_g, w_mlp1, w_mlp2, final_norm_g):
    cos, sin = axial_rope_tables(x_sample.shape[1])
    yp, ys = x_prompt, x_sample
    new_ak, new_av, new_ck, new_cv, new_st = [], [], [], [], []
    for layer in range(DEPTH):
        lp = {
            'norm1_g': norm1_g[layer], 'w_ada': w_ada[layer], 'b_ada': b_ada[layer], 'w_in': w_in[layer],
            'a_sink': a_sink[layer], 'conv_w': conv_w[layer], 'conv_b': conv_b[layer],
            'ssm_a_log': ssm_a_log[layer], 'ssm_dt_bias': ssm_dt_bias[layer], 'ssm_d': ssm_d[layer],
            'ssm_norm_g': ssm_norm_g[layer], 'c_q_norm': c_q_norm[layer], 'c_k_norm': c_k_norm[layer],
            'w_oa': w_oa[layer], 'w_ob': w_ob[layer], 'w_oc': w_oc[layer], 'w_out': w_out[layer],
            'norm2_g': norm2_g[layer], 'w_mlp1': w_mlp1[layer], 'w_mlp2': w_mlp2[layer],
        }
        yp, (ka, va, kc, vc, st) = context_layer(yp, c_ctx, lp)
        new_ak.append(ka)
        new_av.append(va)
        new_ck.append(kc)
        new_cv.append(vc)
        new_st.append(st)
        ys = latent_layer(ys, c, cache_a_k[:, layer], cache_a_v[:, layer], cache_c_k[:, layer], cache_c_v[:, layer],
                          state_ssm[:, layer], cos, sin, lp)
    y_prompt = rms_norm(yp, final_norm_g)
    y_sample = rms_norm(ys, final_norm_g)
    return (y_prompt, y_sample, jnp.stack(new_ak, axis=1), jnp.stack(new_av, axis=1), jnp.stack(new_ck, axis=1),
            jnp.stack(new_cv, axis=1), jnp.stack(new_st, axis=1))
```

```python
import functools
import math

import jax
import jax.numpy as jnp
from jax.experimental import pallas as pl
from jax.experimental.pallas import tpu as pltpu

F32 = jnp.float32
BF16 = jnp.bfloat16

D_MODEL = 2048
BATCH = 16
SEQ = 256
DEPTH = 2
DEC_BATCH = 4
DEC_SEQ = 1024
PAST_LEN = 256
GRID_W = 64
HEAD_DIM = 128
ROPE_THETA = 10000.0
EPS = 1e-6
Q_HEADS = 8
KV_HEADS = 2
Q_PER_KV = Q_HEADS // KV_HEADS
A_WINDOW = 128
SSM_HEADS = 32
SSM_HEAD_DIM = 64
SSM_GROUPS = 2
SSM_STATE = 128
CONV_K = 3
D_FF = 4 * D_MODEL
QW = Q_HEADS * HEAD_DIM
KVW = KV_HEADS * HEAD_DIM
SSM_INNER = SSM_HEADS * SSM_HEAD_DIM
CONV_CH = SSM_INNER + 2 * SSM_GROUPS * SSM_STATE
N_MOD = 6

T_CTX = BATCH * SEQ
T_LAT = DEC_BATCH * DEC_SEQ
T_ALL = T_CTX + T_LAT
MOD_ROWS = 8
CTX_MOD_ROW = DEC_BATCH

R1_COLS = QW + 2 * KVW + SSM_INNER + CONV_CH
DT_COLS = 2 * SSM_HEADS
R2_START = R1_COLS + DT_COLS
R2_COLS = QW + 2 * KVW + 3 * D_MODEL
OFF_Q, OFF_K, OFF_V = 0, QW, QW + KVW
OFF_Z = QW + 2 * KVW
OFF_XBC = OFF_Z + SSM_INNER
OFF_GATES = QW + 2 * KVW

LANE = 128
HALF_LANE = LANE // 2
SSD_CHUNK = 128
VMEM_LIMIT = 56 * 1024 * 1024


def _cparams(n_axes, vmem=VMEM_LIMIT):
    return pltpu.CompilerParams(dimension_semantics=("arbitrary",) * n_axes, vmem_limit_bytes=vmem)


def _mod_row(tile, tm):
    n_ctx_tiles = T_CTX // tm
    return jnp.where(tile < n_ctx_tiles, CTX_MOD_ROW, (tile - n_ctx_tiles) // (DEC_SEQ // tm))


def _silu(x):
    return x * (1.0 / (1.0 + jnp.exp(-x)))


def _sigmoid(x):
    return 1.0 / (1.0 + jnp.exp(-x))


def _ada_body(cond_ref, w_ref, b_ref, o_ref):
    a = _silu(cond_ref[...]).astype(BF16)
    o_ref[...] = jnp.dot(a, w_ref[...].astype(BF16), preferred_element_type=F32) + b_ref[...]


def ada_modulation(cond, w_ada, b_ada):
    tn = 1024
    n_out = N_MOD * D_MODEL
    return pl.pallas_call(
        _ada_body,
        grid=(DEPTH, n_out // tn),
        in_specs=[
            pl.BlockSpec((MOD_ROWS, D_MODEL), lambda l, n: (0, 0)),
            pl.BlockSpec((None, D_MODEL, tn), lambda l, n: (l, 0, n)),
            pl.BlockSpec((None, 1, tn), lambda l, n: (l, 0, n)),
        ],
        out_specs=pl.BlockSpec((None, MOD_ROWS, tn), lambda l, n: (l, 0, n)),
        out_shape=jax.ShapeDtypeStruct((DEPTH, MOD_ROWS, n_out), F32),
        compiler_params=_cparams(2),
        name="ada_modulation",
    )(cond, w_ada, b_ada.reshape(DEPTH, 1, n_out))


def _prenorm_body(x_ref, g_ref, sh_ref, sc_ref, *rest, with_dt):
    x = x_ref[...]
    y = x * jax.lax.rsqrt(jnp.mean(x * x, axis=-1, keepdims=True) + EPS) * g_ref[...]
    h = (y * (1.0 + sc_ref[...]) + sh_ref[...]).astype(BF16)
    if with_dt:
        wdt_ref, h_ref, dt_ref = rest
        dt_ref[...] = jnp.dot(h, wdt_ref[...].astype(BF16), preferred_element_type=F32)
    else:
        (h_ref,) = rest
    h_ref[...] = h


def prenorm(x, gain, mod, layer, shift_idx, w_in=None):
    tm = 512
    with_dt = w_in is not None
    mod_spec = lambda k: pl.BlockSpec((None, None, None, 1, D_MODEL),
                                      lambda i: (layer, _mod_row(i, tm), k, 0, 0))
    in_specs = [
        pl.BlockSpec((tm, D_MODEL), lambda i: (i, 0)),
        pl.BlockSpec((None, 1, D_MODEL), lambda i: (layer, 0, 0)),
        mod_spec(shift_idx),
        mod_spec(shift_idx + 1),
    ]
    args = [x, gain.reshape(DEPTH, 1, D_MODEL), mod, mod]
    out_specs = [pl.BlockSpec((tm, D_MODEL), lambda i: (i, 0))]
    out_shape = [jax.ShapeDtypeStruct((T_ALL, D_MODEL), BF16)]
    if with_dt:
        in_specs.append(pl.BlockSpec((None, D_MODEL, LANE), lambda i: (layer, 0, R1_COLS // LANE)))
        args.append(w_in)
        out_specs.append(pl.BlockSpec((tm, LANE), lambda i: (i, 0)))
        out_shape.append(jax.ShapeDtypeStruct((T_ALL, LANE), F32))
    res = pl.pallas_call(
        functools.partial(_prenorm_body, with_dt=with_dt),
        grid=(T_ALL // tm,),
        in_specs=in_specs,
        out_specs=out_specs,
        out_shape=out_shape,
        compiler_params=_cparams(1),
        name="prenorm_dt" if with_dt else "prenorm",
    )(*args)
    return res if with_dt else res[0]


PROJ_TN = 1536
PROJ_ROWS = 256


def _proj_body(h_ref, wa_ref, *rest, shifted):
    if shifted:
        wb_ref, o_ref, wbf_ref = rest
    else:
        o_ref, wbf_ref = rest

    @pl.when(pl.program_id(1) == 0)
    def _():
        lane = jax.lax.broadcasted_iota(jnp.int32, (PROJ_ROWS, LANE), 1)
        for r in range(0, D_MODEL, PROJ_ROWS):
            rows = slice(r, r + PROJ_ROWS)
            if not shifted:
                wbf_ref[rows, :] = wa_ref[rows, :].astype(BF16)
                continue
            n_tiles = PROJ_TN // LANE
            swapped = [pltpu.roll(wa_ref[rows, j * LANE:(j + 1) * LANE], HALF_LANE, 1) for j in range(n_tiles)]
            swapped.append(pltpu.roll(wb_ref[rows, :], HALF_LANE, 1))
            for j in range(n_tiles):
                tile = jnp.where(lane < HALF_LANE, swapped[j], swapped[j + 1])
                wbf_ref[rows, j * LANE:(j + 1) * LANE] = tile.astype(BF16)

    o_ref[...] = jnp.dot(h_ref[...], wbf_ref[...], preferred_element_type=F32).astype(o_ref.dtype)


def in_projection(h, w_in, layer, region):
    tm, tn = 1024, PROJ_TN
    shifted = region == 2
    n_cols = R2_COLS if shifted else R1_COLS
    first_block = R1_COLS // tn if shifted else 0
    in_specs = [
        pl.BlockSpec((tm, D_MODEL), lambda n, m: (m, 0)),
        pl.BlockSpec((None, D_MODEL, tn), lambda n, m: (layer, 0, first_block + n)),
    ]
    args = [h, w_in]
    if shifted:
        in_specs.append(pl.BlockSpec((None, D_MODEL, LANE),
                                     lambda n, m: (layer, 0, (R1_COLS + tn * (n + 1)) // LANE)))
        args.append(w_in)
    return pl.pallas_call(
        functools.partial(_proj_body, shifted=shifted),
        grid=(n_cols // tn, T_ALL // tm),
        in_specs=in_specs,
        out_specs=pl.BlockSpec((tm, tn), lambda n, m: (m, n)),
        out_shape=jax.ShapeDtypeStruct((T_ALL, n_cols), BF16),
        scratch_shapes=[pltpu.VMEM((D_MODEL, tn), BF16)],
        compiler_params=_cparams(2),
        name=f"in_projection_r{region}",
    )(*args)


MERGE_TN = 512
MERGE_K = QW + SSM_INNER + QW


def _merge_body(ya_ref, yb_ref, yc_ref, ga_ref, gb_ref, gc_ref, wa_ref, wb_ref, wc_ref, o_ref, wbf_ref):
    @pl.when(pl.program_id(1) == 0)
    def _():
        wbf_ref[0:QW, :] = wa_ref[...].astype(BF16)
        wbf_ref[QW:QW + SSM_INNER, :] = wb_ref[...].astype(BF16)
        wbf_ref[QW + SSM_INNER:MERGE_K, :] = wc_ref[...].astype(BF16)

    br_a = jnp.dot(ya_ref[...], wbf_ref[0:QW, :], preferred_element_type=F32)
    br_b = jnp.dot(yb_ref[...], wbf_ref[QW:QW + SSM_INNER, :], preferred_element_type=F32)
    br_c = jnp.dot(yc_ref[...], wbf_ref[QW + SSM_INNER:MERGE_K, :], preferred_element_type=F32)
    merged = (_sigmoid(ga_ref[...].astype(F32)) * br_a + _sigmoid(gb_ref[...].astype(F32)) * br_b
              + _sigmoid(gc_ref[...].astype(F32)) * br_c)
    o_ref[...] = merged.astype(BF16)


def merge_branches(ya, yb, yc, pr2, w_oa, w_ob, w_oc, layer):
    tm, tn = 512, MERGE_TN
    gate_spec = lambda k: pl.BlockSpec((tm, tn), lambda n, m: (m, (OFF_GATES + k * D_MODEL) // tn + n))
    w_spec = lambda rows: pl.BlockSpec((None, rows, tn), lambda n, m: (layer, 0, n))
    return pl.pallas_call(
        _merge_body,
        grid=(D_MODEL // tn, T_ALL // tm),
        in_specs=[
            pl.BlockSpec((tm, QW), lambda n, m: (m, 0)),
            pl.BlockSpec((tm, SSM_INNER), lambda n, m: (m, 0)),
            pl.BlockSpec((tm, QW), lambda n, m: (m, 0)),
            gate_spec(0), gate_spec(1), gate_spec(2),
            w_spec(QW), w_spec(SSM_INNER), w_spec(QW),
        ],
        out_specs=pl.BlockSpec((tm, tn), lambda n, m: (m, n)),
        out_shape=jax.ShapeDtypeStruct((T_ALL, D_MODEL), BF16),
        scratch_shapes=[pltpu.VMEM((MERGE_K, tn), BF16)],
        compiler_params=_cparams(2),
        name="merge_branches",
    )(ya, yb, yc, pr2, pr2, pr2, w_oa, w_ob, w_oc)


def _outproj_body(a_ref, w_ref, x_ref, g_ref, o_ref, wbf_ref):
    @pl.when(pl.program_id(1) == 0)
    def _():
        wbf_ref[...] = w_ref[...].astype(BF16)

    o_ref[...] = x_ref[...] + g_ref[...] * jnp.dot(a_ref[...], wbf_ref[...], preferred_element_type=F32)


def out_projection(merged, x, w_out, mod, layer):
    tm, tn = 512, 1024
    return pl.pallas_call(
        _outproj_body,
        grid=(D_MODEL // tn, T_ALL // tm),
        in_specs=[
            pl.BlockSpec((tm, D_MODEL), lambda n, m: (m, 0)),
            pl.BlockSpec((None, D_MODEL, tn), lambda n, m: (layer, 0, n)),
            pl.BlockSpec((tm, tn), lambda n, m: (m, n)),
            pl.BlockSpec((None, None, None, 1, tn), lambda n, m: (layer, _mod_row(m, tm), 2, 0, n)),
        ],
        out_specs=pl.BlockSpec((tm, tn), lambda n, m: (m, n)),
        out_shape=jax.ShapeDtypeStruct((T_ALL, D_MODEL), F32),
        scratch_shapes=[pltpu.VMEM((D_MODEL, tn), BF16)],
        compiler_params=_cparams(2),
        name="out_projection",
    )(merged, w_out, x, mod)


MLP_TF = 256
MLP_TN = 512
MLP_EPI_ROWS = 256


def _mlp_body(h_ref, w1_ref, w2_ref, x_ref, g_ref, fg_ref, o_ref, *, final_norm):
    f = pl.program_id(1)
    hid = jnp.dot(h_ref[...], w1_ref[...].astype(BF16), preferred_element_type=F32)
    hid = jnp.square(jnp.maximum(hid, 0.0)).astype(BF16)
    for n0 in range(0, D_MODEL, MLP_TN):
        cols = slice(n0, n0 + MLP_TN)
        part = jnp.dot(hid, w2_ref[:, cols].astype(BF16), preferred_element_type=F32)

        @pl.when(f == 0)
        def _():
            o_ref[:, cols] = part

        @pl.when(f > 0)
        def _():
            o_ref[:, cols] += part

    @pl.when(f == pl.num_programs(1) - 1)
    def _():
        for r0 in range(0, o_ref.shape[0], MLP_EPI_ROWS):
            rows = slice(r0, r0 + MLP_EPI_ROWS)
            y = x_ref[rows, :] + g_ref[...] * o_ref[rows, :]
            if final_norm:
                y = y * jax.lax.rsqrt(jnp.mean(y * y, axis=-1, keepdims=True) + EPS) * fg_ref[...]
            o_ref[rows, :] = y


def mlp_residual(h2, x1, w_mlp1, w_mlp2, mod, final_gain, layer, final_norm):
    tm, tf = 1024, MLP_TF
    return pl.pallas_call(
        functools.partial(_mlp_body, final_norm=final_norm),
        grid=(T_ALL // tm, D_FF // tf),
        in_specs=[
            pl.BlockSpec((tm, D_MODEL), lambda m, f: (m, 0)),
            pl.BlockSpec((None, D_MODEL, tf), lambda m, f: (layer, 0, f)),
            pl.BlockSpec((None, tf, D_MODEL), lambda m, f: (layer, f, 0)),
            pl.BlockSpec((tm, D_MODEL), lambda m, f: (m, 0), pipeline_mode=pl.Buffered(1)),
            pl.BlockSpec((None, None, None, 1, D_MODEL), lambda m, f: (layer, _mod_row(m, tm), 5, 0, 0)),
            pl.BlockSpec((1, D_MODEL), lambda m, f: (0, 0)),
        ],
        out_specs=pl.BlockSpec((tm, D_MODEL), lambda m, f: (m, 0)),
        out_shape=jax.ShapeDtypeStruct((T_ALL, D_MODEL), F32),
        compiler_params=_cparams(2),
        name="mlp_final" if final_norm else "mlp",
    )(h2, w_mlp1, w_mlp2, x1, mod, final_gain.reshape(1, D_MODEL))


ATT_TQ = 256


def _head_rms(x, g):
    return x * jax.lax.rsqrt(jnp.mean(x * x, axis=-1, keepdims=True) + EPS) * g


def _rope(x, cos2, sin2):
    return x * cos2 + pltpu.roll(x, HALF_LANE, 1) * sin2


def _attn_body(*refs, layer, n_ctx, seq_len, use_sink, band, qk_norm, rope, emit_kv, n_alias):
    it = iter(refs)
    q_ref, k_ref, v_ref = next(it), next(it), next(it)
    kctx_ref = vctx_ref = sink_ref = qg_ref = kg_ref = cosq_ref = sinq_ref = cosk_ref = sink_k_ref = None
    if n_ctx:
        kctx_ref, vctx_ref = next(it), next(it)
    if use_sink:
        sink_ref = next(it)
    if qk_norm:
        qg_ref, kg_ref = next(it), next(it)
    if rope:
        cosq_ref, sinq_ref, cosk_ref, sink_k_ref = next(it), next(it), next(it), next(it)
    for _ in range(n_alias):
        next(it)
    o_ref = next(it)
    kout_ref = vout_ref = None
    if emit_kv:
        kout_ref, vout_ref = next(it), next(it)
    kall_ref, vall_ref = next(it), next(it)

    j = pl.program_id(1)
    n = pl.program_id(2)

    @pl.when(n == 0)
    def _():
        k = k_ref[...].astype(F32)
        if qk_norm:
            k = _head_rms(k, kg_ref[...])
        if emit_kv:
            kout_ref[...] = k
            vout_ref[...] = v_ref[...].astype(F32)
        if rope:
            k = _rope(k, cosk_ref[...], sink_k_ref[...])
        if n_ctx:
            kall_ref[0:n_ctx, :] = kctx_ref[...].astype(BF16)
            vall_ref[0:n_ctx, :] = vctx_ref[...].astype(BF16)
        kall_ref[n_ctx:n_ctx + seq_len, :] = k.astype(BF16)
        vall_ref[n_ctx:n_ctx + seq_len, :] = v_ref[...]

    n_keys = n_ctx + seq_len
    if band:
        qpos = n * ATT_TQ + jax.lax.broadcasted_iota(jnp.int32, (ATT_TQ, n_keys), 0)
        col = jax.lax.broadcasted_iota(jnp.int32, (ATT_TQ, n_keys), 1)
        visible = (col < n_ctx) | (jnp.abs(col - n_ctx - qpos) <= A_WINDOW)
    scale = HEAD_DIM ** -0.5
    for g in range(Q_PER_KV):
        q = q_ref[:, g * HEAD_DIM:(g + 1) * HEAD_DIM].astype(F32)
        if qk_norm:
            q = _head_rms(q, qg_ref[...])
        if rope:
            q = _rope(q, cosq_ref[...], sinq_ref[...])
        q = (q * scale).astype(BF16)
        s = jax.lax.dot_general(q, kall_ref[...], (((1,), (1,)), ((), ())), preferred_element_type=F32)
        if band:
            s = jnp.where(visible, s, -jnp.inf)
        m = jnp.max(s, axis=-1, keepdims=True)
        if use_sink:
            sk = sink_ref[layer * Q_HEADS + j * Q_PER_KV + g]
            m = jnp.maximum(m, sk)
        p = jnp.exp(s - m)
        den = jnp.sum(p, axis=-1, keepdims=True)
        if use_sink:
            den = den + jnp.exp(sk - m)
        o = jnp.dot(p.astype(BF16), vall_ref[...], preferred_element_type=F32)
        o_ref[:, g * HEAD_DIM:(g + 1) * HEAD_DIM] = (o * (1.0 / den)).astype(BF16)


def attention(pr, layer, *, latent, mixer, y_prev=None, ctx_k=None, ctx_v=None, sink=None,
              q_gain=None, k_gain=None, rope_tabs=None, kv_prev=None):
    n_seq, seq_len, row0 = (DEC_BATCH, DEC_SEQ, T_CTX) if latent else (BATCH, SEQ, 0)
    n_ctx = PAST_LEN if latent else 0
    use_sink = mixer == "a"
    qk_norm = mixer == "c"
    band = latent and mixer == "a"
    rope = latent
    emit_kv = not latent
    tq = ATT_TQ
    qblocks = seq_len // tq
    grid = (n_seq, KV_HEADS, qblocks)
    qw_kv = Q_PER_KV * HEAD_DIM

    in_specs = [
        pl.BlockSpec((tq, qw_kv), lambda b, j, n: (row0 // tq + b * qblocks + n, OFF_Q // qw_kv + j)),
        pl.BlockSpec((seq_len, HEAD_DIM), lambda b, j, n: (row0 // seq_len + b, OFF_K // HEAD_DIM + j)),
        pl.BlockSpec((seq_len, HEAD_DIM), lambda b, j, n: (row0 // seq_len + b, OFF_V // HEAD_DIM + j)),
    ]
    args = [pr, pr, pr]
    if n_ctx:
        cache_spec = pl.BlockSpec((None, None, PAST_LEN, HEAD_DIM), lambda b, j, n: (b, layer, 0, j))
        in_specs += [cache_spec, cache_spec]
        args += [ctx_k.reshape(DEC_BATCH, DEPTH, PAST_LEN, KVW), ctx_v.reshape(DEC_BATCH, DEPTH, PAST_LEN, KVW)]
    if use_sink:
        in_specs.append(pl.BlockSpec(memory_space=pltpu.SMEM))
        args.append(sink.reshape(DEPTH * Q_HEADS))
    if qk_norm:
        gain_spec = pl.BlockSpec((None, 1, HEAD_DIM), lambda b, j, n: (layer, 0, 0))
        in_specs += [gain_spec, gain_spec]
        args += [q_gain.reshape(DEPTH, 1, HEAD_DIM), k_gain.reshape(DEPTH, 1, HEAD_DIM)]
    if rope:
        cos2, sin2 = rope_tabs
        in_specs += [pl.BlockSpec((tq, HEAD_DIM), lambda b, j, n: (n, 0))] * 2
        in_specs += [pl.BlockSpec((seq_len, HEAD_DIM), lambda b, j, n: (0, 0))] * 2
        args += [cos2, sin2, cos2, sin2]

    y_shape = jax.ShapeDtypeStruct((T_ALL, QW), BF16)
    y_spec = pl.BlockSpec((tq, qw_kv), lambda b, j, n: (row0 // tq + b * qblocks + n, j))
    out_specs, out_shape, aliases = [y_spec], [y_shape], {}
    if emit_kv:
        kv_shape = jax.ShapeDtypeStruct((BATCH, DEPTH, SEQ, KVW), F32)
        kv_spec = pl.BlockSpec((None, None, SEQ, HEAD_DIM), lambda b, j, n: (b, layer, 0, j))
        out_specs += [kv_spec, kv_spec]
        out_shape += [kv_shape, kv_shape]
        if kv_prev is not None:
            aliases[len(args)] = 1
            aliases[len(args) + 1] = 2
            in_specs += [pl.BlockSpec(memory_space=pl.ANY)] * 2
            args += list(kv_prev)
    if y_prev is not None:
        aliases[len(args)] = 0
        in_specs.append(pl.BlockSpec(memory_space=pl.ANY))
        args.append(y_prev)

    body = functools.partial(_attn_body, layer=layer, n_ctx=n_ctx, seq_len=seq_len, use_sink=use_sink, band=band,
                             qk_norm=qk_norm, rope=rope, emit_kv=emit_kv, n_alias=len(aliases))
    return pl.pallas_call(
        body,
        grid=grid,
        in_specs=in_specs,
        out_specs=out_specs,
        out_shape=out_shape,
        input_output_aliases=aliases,
        scratch_shapes=[pltpu.VMEM((n_ctx + seq_len, HEAD_DIM), BF16)] * 2,
        compiler_params=_cparams(3),
        name=f"attn_{mixer}_{'lat' if latent else 'ctx'}",
    )(*args)


SSD_COLBLK = 512
N_Z_BLK = SSM_INNER // SSD_COLBLK
N_XBC_BLK = CONV_CH // SSD_COLBLK
HALO = 16
GROUP_W = SSM_INNER // SSM_GROUPS
PAIRS = SSM_HEADS // 2
PAIRS_PER_GROUP = PAIRS // SSM_GROUPS


def _softplus(x):
    return jnp.maximum(x, 0.0) + jnp.log1p(jnp.exp(-jnp.abs(x)))


def _split3(x):
    hi = x.astype(BF16)
    r = x - hi.astype(F32)
    mid = r.astype(BF16)
    lo = (r - mid.astype(F32)).astype(BF16)
    return hi, mid, lo


def _ssd_body(*refs, seq_len, has_h0, emit_state, n_alias):
    it = iter(refs)
    z_refs = [next(it) for _ in range(N_Z_BLK)]
    xbc_refs = [next(it) for _ in range(N_XBC_BLK)]
    dt_ref, convw_ref, convb_ref, alog_ref, dtb_ref, dexp_ref, ng_ref = (next(it) for _ in range(7))
    h0_ref = next(it) if has_h0 else None
    for _ in range(n_alias):
        next(it)
    y_ref = next(it)
    st_ref = next(it) if emit_state else None
    conv_scr, y_scr, h_scr = next(it), next(it), next(it)

    n_chunks = seq_len // SSD_CHUNK
    q = SSD_CHUNK
    lane = jax.lax.broadcasted_iota(jnp.int32, (q, LANE), 1)
    row = jax.lax.broadcasted_iota(jnp.int32, (q, LANE), 0)
    low_half = lane < HALF_LANE

    for d in range(2):
        for blk in range(SSM_INNER // LANE):
            cols = slice(blk * LANE, (blk + 1) * LANE)
            if has_h0:
                h_scr[d, :, cols] = h0_ref[d, cols, :].T
            else:
                h_scr[d, :, cols] = jnp.zeros((SSM_STATE, LANE), F32)

    def conv_chunk(c, carry):
        r0 = pl.multiple_of(c * q, q)
        prev0 = pl.multiple_of(jnp.maximum(r0 - HALO, 0), HALO)
        next0 = pl.multiple_of(jnp.minimum(r0 + q, seq_len - HALO), HALO)
        has_prev = (r0 > 0).astype(F32)
        has_next = (r0 + q < seq_len).astype(F32)
        for j in range(CONV_CH // LANE):
            src = xbc_refs[j // (SSD_COLBLK // LANE)]
            sc = slice((j % (SSD_COLBLK // LANE)) * LANE, (j % (SSD_COLBLK // LANE) + 1) * LANE)
            cols = slice(j * LANE, (j + 1) * LANE)
            u = src[pl.ds(r0, q), sc].astype(F32)
            prev_row = src[pl.ds(prev0, HALO), sc].astype(F32)[HALO - 1:HALO, :] * has_prev
            next_row = src[pl.ds(next0, HALO), sc].astype(F32)[0:1, :] * has_next
            up = jnp.where(row == 0, prev_row, pltpu.roll(u, 1, 0))
            dn = jnp.where(row == q - 1, next_row, pltpu.roll(u, q - 1, 0))
            v = (convw_ref[0:1, cols] * up + convw_ref[1:2, cols] * u + convw_ref[2:3, cols] * dn
                 + convb_ref[:, cols])
            act = _silu(v)
            conv_scr[pl.ds(r0, q), cols] = act.astype(BF16)
            if j < SSM_INNER // LANE:
                y_scr[pl.ds(r0, q), cols] = dexp_ref[:, cols] * act
        return carry

    jax.lax.fori_loop(0, n_chunks, conv_chunk, 0)

    def scan_chunk(c, d):
        r0 = pl.multiple_of(c * q, q)
        edge = q - 1 if d == 0 else 0
        vis = (row >= lane) if d == 0 else (row <= lane)
        tri = jnp.where(vis, 1.0, 0.0).astype(BF16)
        dt = _softplus(dt_ref[pl.ds(r0, q), :] + dtb_ref[...])
        a = dt * (-jnp.exp(alog_ref[...]))
        a_hi, a_mid, a_lo = _split3(a)
        acum = (jnp.dot(tri, a_hi, preferred_element_type=F32) + jnp.dot(tri, a_mid, preferred_element_type=F32)
                + jnp.dot(tri, a_lo, preferred_element_type=F32))
        fs = jnp.exp(acum)
        acum_t = acum.T
        dt_t = dt.T
        row_t = acum_t - jnp.log(dt_t)
        w_t = dt_t * jnp.exp(acum_t[:, edge:edge + 1] - acum_t)
        for g in range(SSM_GROUPS):
            b_g = conv_scr[pl.ds(r0, q), SSM_INNER + g * SSM_STATE:SSM_INNER + (g + 1) * SSM_STATE]
            c_lo = SSM_INNER + SSM_GROUPS * SSM_STATE + g * SSM_STATE
            c_g = conv_scr[pl.ds(r0, q), c_lo:c_lo + SSM_STATE]
            cb = jax.lax.dot_general(c_g, b_g, (((1,), (1,)), ((), ())), preferred_element_type=F32)
            b_t = b_g.astype(F32).T
            h_g = h_scr[d, :, g * GROUP_W:(g + 1) * GROUP_W].astype(BF16)
            y_off = jnp.dot(c_g, h_g, preferred_element_type=F32)
            for pp in range(PAIRS_PER_GROUP):
                p = g * PAIRS_PER_GROUP + pp
                cols = slice(p * LANE, (p + 1) * LANE)
                lhs_top, lhs_bot = [], []
                for h in (2 * p, 2 * p + 1):
                    ell = d * SSM_HEADS + h
                    diff = acum[:, ell:ell + 1] - row_t[ell:ell + 1, :]
                    lhs_top.append(cb * jnp.exp(jnp.where(vis, diff, -jnp.inf)))
                    lhs_bot.append(b_t * w_t[ell:ell + 1, :])
                lhs = jnp.concatenate([jnp.concatenate(lhs_top, axis=1), jnp.concatenate(lhs_bot, axis=1)],
                                      axis=0).astype(BF16)
                xp = conv_scr[pl.ds(r0, q), cols]
                zero = jnp.zeros_like(xp)
                rhs = jnp.concatenate([jnp.where(low_half, xp, zero), jnp.where(low_half, zero, xp)], axis=0)
                res = jnp.dot(lhs, rhs, preferred_element_type=F32)
                ell_a = d * SSM_HEADS + 2 * p
                factor = jnp.where(low_half, fs[:, ell_a:ell_a + 1], fs[:, ell_a + 1:ell_a + 2])
                y_scr[pl.ds(r0, q), cols] += res[0:q, :] + y_off[:, pp * LANE:(pp + 1) * LANE] * factor
                h_scr[d, :, cols] = h_scr[d, :, cols] * factor[edge:edge + 1, :] + res[q:2 * q, :]

    def scan_step(i, carry):
        scan_chunk(i, 0)
        scan_chunk(n_chunks - 1 - i, 1)
        return carry

    jax.lax.fori_loop(0, n_chunks, scan_step, 0)

    if emit_state:
        for d in range(2):
            for blk in range(SSM_INNER // LANE):
                cols = slice(blk * LANE, (blk + 1) * LANE)
                st_ref[d, cols, :] = h_scr[d, :, cols].T

    def finish_chunk(c, carry):
        r0 = pl.multiple_of(c * q, q)
        gated = []
        for k in range(N_Z_BLK):
            cols = slice(k * SSD_COLBLK, (k + 1) * SSD_COLBLK)
            gated.append(y_scr[pl.ds(r0, q), cols] * _silu(z_refs[k][pl.ds(r0, q), :].astype(F32)))
        ssq = sum(jnp.sum(gk * gk, axis=-1, keepdims=True) for gk in gated)
        inv = jax.lax.rsqrt(ssq * (1.0 / SSM_INNER) + EPS)
        for k in range(N_Z_BLK):
            cols = slice(k * SSD_COLBLK, (k + 1) * SSD_COLBLK)
            y_ref[pl.ds(r0, q), cols] = (gated[k] * inv * ng_ref[:, cols]).astype(BF16)
        return carry

    jax.lax.fori_loop(0, n_chunks, finish_chunk, 0)


def ssd_mixer(pr1, dt_raw, layer, *, latent, conv_w, conv_b, a_log_row, dt_bias_row, d_row, norm_gain,
              h0=None, y_prev=None, state_prev=None):
    n_seq, seq_len, row0 = (DEC_BATCH, DEC_SEQ, T_CTX) if latent else (BATCH, SEQ, 0)
    has_h0 = latent
    emit_state = not latent
    seq_blk = row0 // seq_len

    def col_spec(first, k):
        return pl.BlockSpec((seq_len, SSD_COLBLK), lambda b: (seq_blk + b, first // SSD_COLBLK + k))

    in_specs = [col_spec(OFF_Z, k) for k in range(N_Z_BLK)] + [col_spec(OFF_XBC, k) for k in range(N_XBC_BLK)]
    args = [pr1] * (N_Z_BLK + N_XBC_BLK)
    in_specs += [
        pl.BlockSpec((seq_len, LANE), lambda b: (seq_blk + b, 0)),
        pl.BlockSpec((None, CONV_K, CONV_CH), lambda b: (layer, 0, 0)),
        pl.BlockSpec((None, 1, CONV_CH), lambda b: (layer, 0, 0)),
        pl.BlockSpec((None, 1, LANE), lambda b: (layer, 0, 0)),
        pl.BlockSpec((None, 1, LANE), lambda b: (layer, 0, 0)),
        pl.BlockSpec((None, 1, SSM_INNER), lambda b: (layer, 0, 0)),
        pl.BlockSpec((None, 1, SSM_INNER), lambda b: (layer, 0, 0)),
    ]
    args += [dt_raw, conv_w, conv_b.reshape(DEPTH, 1, CONV_CH), a_log_row, dt_bias_row, d_row,
             norm_gain.reshape(DEPTH, 1, SSM_INNER)]
    state_block = (None, None, 2, SSM_INNER, SSM_STATE)
    if has_h0:
        in_specs.append(pl.BlockSpec(state_block, lambda b: (b, layer, 0, 0, 0)))
        args.append(h0.reshape(DEC_BATCH, DEPTH, 2, SSM_INNER, SSM_STATE))

    out_specs = [pl.BlockSpec((seq_len, SSM_INNER), lambda b: (seq_blk + b, 0))]
    out_shape = [jax.ShapeDtypeStruct((T_ALL, SSM_INNER), BF16)]
    aliases = {}
    if emit_state:
        out_specs.append(pl.BlockSpec(state_block, lambda b: (b, layer, 0, 0, 0)))
        out_shape.append(jax.ShapeDtypeStruct((BATCH, DEPTH, 2, SSM_INNER, SSM_STATE), F32))
        if state_prev is not None:
            aliases[len(args)] = 1
            in_specs.append(pl.BlockSpec(memory_space=pl.ANY))
            args.append(state_prev)
    if y_prev is not None:
        aliases[len(args)] = 0
        in_specs.append(pl.BlockSpec(memory_space=pl.ANY))
        args.append(y_prev)

    return pl.pallas_call(
        functools.partial(_ssd_body, seq_len=seq_len, has_h0=has_h0, emit_state=emit_state, n_alias=len(aliases)),
        grid=(n_seq,),
        in_specs=in_specs,
        out_specs=out_specs,
        out_shape=out_shape,
        input_output_aliases=aliases,
        scratch_shapes=[
            pltpu.VMEM((seq_len, CONV_CH), BF16),
            pltpu.VMEM((seq_len, SSM_INNER), F32),
            pltpu.VMEM((2, SSM_STATE, SSM_INNER), F32),
        ],
        compiler_params=_cparams(1),
        name=f"ssd_{'lat' if latent else 'ctx'}",
    )(*args)


def _rope_tables():
    rows = DEC_SEQ // GRID_W
    r = jnp.repeat(jnp.arange(rows, dtype=F32), GRID_W)
    c = jnp.tile(jnp.arange(GRID_W, dtype=F32), rows)
    n_freq = HEAD_DIM // 4
    inv_freq = ROPE_THETA ** (-jnp.arange(n_freq, dtype=F32) / n_freq)
    ang = jnp.concatenate([r[:, None] * inv_freq, c[:, None] * inv_freq], axis=-1)
    cos, sin = jnp.cos(ang), jnp.sin(ang)
    return jnp.concatenate([cos, cos], axis=-1), jnp.concatenate([-sin, sin], axis=-1)


def kernel(x_prompt, x_sample, cache_a_k, cache_a_v, cache_c_k, cache_c_v, state_ssm, c, c_ctx, norm1_g, w_ada, b_ada, w_in, a_sink, conv_w, conv_b, ssm_a_log, ssm_dt_bias, ssm_d, ssm_norm_g, c_q_norm, c_k_norm, w_oa, w_ob, w_oc, w_out, norm2_g, w_mlp1, w_mlp2, final_norm_g):
    x = jnp.concatenate([x_prompt.reshape(T_CTX, D_MODEL), x_sample.reshape(T_LAT, D_MODEL)], axis=0)
    cond = jnp.concatenate([c, c_ctx[None, :], jnp.zeros((MOD_ROWS - DEC_BATCH - 1, D_MODEL), F32)], axis=0)
    mod = ada_modulation(cond, w_ada, b_ada).reshape(DEPTH, MOD_ROWS, N_MOD, 1, D_MODEL)
    rope_tabs = _rope_tables()
    lane_pad = jnp.zeros((DEPTH, LANE - 2 * SSM_HEADS), F32)
    a_log_row = jnp.concatenate([ssm_a_log.reshape(DEPTH, 2 * SSM_HEADS), lane_pad], axis=1).reshape(DEPTH, 1, LANE)
    dt_bias_row = jnp.concatenate([ssm_dt_bias.reshape(DEPTH, 2 * SSM_HEADS), lane_pad], axis=1).reshape(DEPTH, 1, LANE)
    d_row = jnp.repeat(ssm_d, SSM_HEAD_DIM, axis=1).reshape(DEPTH, 1, SSM_INNER)

    kv_a = kv_c = state = None
    for layer in range(DEPTH):
        h, dt_raw = prenorm(x, norm1_g, mod, layer, 0, w_in=w_in)
        pr1 = in_projection(h, w_in, layer, 1)
        pr2 = in_projection(h, w_in, layer, 2)

        ya, ka, va = attention(pr1, layer, latent=False, mixer="a", sink=a_sink, kv_prev=kv_a)
        kv_a = (ka, va)
        (ya,) = attention(pr1, layer, latent=True, mixer="a", sink=a_sink, ctx_k=cache_a_k, ctx_v=cache_a_v,
                          rope_tabs=rope_tabs, y_prev=ya)
        yc, kc, vc = attention(pr2, layer, latent=False, mixer="c", q_gain=c_q_norm, k_gain=c_k_norm, kv_prev=kv_c)
        kv_c = (kc, vc)
        (yc,) = attention(pr2, layer, latent=True, mixer="c", q_gain=c_q_norm, k_gain=c_k_norm, ctx_k=cache_c_k,
                          ctx_v=cache_c_v, rope_tabs=rope_tabs, y_prev=yc)
        ssd_args = dict(conv_w=conv_w, conv_b=conv_b, a_log_row=a_log_row, dt_bias_row=dt_bias_row, d_row=d_row,
                        norm_gain=ssm_norm_g)
        yb, state = ssd_mixer(pr1, dt_raw, layer, latent=False, state_prev=state, **ssd_args)
        (yb,) = ssd_mixer(pr1, dt_raw, layer, latent=True, h0=state_ssm, y_prev=yb, **ssd_args)

        merged = merge_branches(ya, yb, yc, pr2, w_oa, w_ob, w_oc, layer)
        x1 = out_projection(merged, x, w_out, mod, layer)
        h2 = prenorm(x1, norm2_g, mod, layer, 3)
        x = mlp_residual(h2, x1, w_mlp1, w_mlp2, mod, final_norm_g, layer, final_norm=layer == DEPTH - 1)

    y_prompt = x[:T_CTX].reshape(BATCH, SEQ, D_MODEL)
    y_sample = x[T_CTX:].reshape(DEC_BATCH, DEC_SEQ, D_MODEL)
    kv5 = (BATCH, DEPTH, SEQ, KV_HEADS, HEAD_DIM)
    new_state = state.reshape(BATCH, DEPTH, 2, SSM_HEADS, SSM_HEAD_DIM, SSM_STATE)
    return (y_prompt, y_sample, kv_a[0].reshape(kv5), kv_a[1].reshape(kv5), kv_c[0].reshape(kv5),
            kv_c[1].reshape(kv5), new_state)
```

```python
import functools
import math

import jax
import jax.numpy as jnp
from jax.experimental import pallas as pl
from jax.experimental.pallas import tpu as pltpu

F32 = jnp.float32
BF16 = jnp.bfloat16

D_MODEL = 2048
BATCH = 16
SEQ = 256
DEPTH = 2
DEC_BATCH = 4
DEC_SEQ = 1024
PAST_LEN = 256
GRID_W = 64
HEAD_DIM = 128
ROPE_THETA = 10000.0
EPS = 1e-6
Q_HEADS = 8
KV_HEADS = 2
Q_PER_KV = Q_HEADS // KV_HEADS
A_WINDOW = 128
SSM_HEADS = 32
SSM_HEAD_DIM = 64
SSM_GROUPS = 2
SSM_STATE = 128
CONV_K = 3
D_FF = 4 * D_MODEL
QW = Q_HEADS * HEAD_DIM
KVW = KV_HEADS * HEAD_DIM
SSM_INNER = SSM_HEADS * SSM_HEAD_DIM
CONV_CH = SSM_INNER + 2 * SSM_GROUPS * SSM_STATE
N_MOD = 6

T_CTX = BATCH * SEQ
T_LAT = DEC_BATCH * DEC_SEQ
T_ALL = T_CTX + T_LAT
MOD_ROWS = 8
CTX_MOD_ROW = DEC_BATCH

R1_COLS = QW + 2 * KVW + SSM_INNER + CONV_CH
DT_COLS = 2 * SSM_HEADS
R2_COLS = QW + 2 * KVW + 3 * D_MODEL
PR_COLS = R1_COLS + R2_COLS
OFF_Q, OFF_K, OFF_V = 0, QW, QW + KVW
OFF_Z = QW + 2 * KVW
OFF_XBC = OFF_Z + SSM_INNER
OFF_GATES = R1_COLS + QW + 2 * KVW

LANE = 128
HALF_LANE = LANE // 2
SSD_CHUNK = 128
VMEM_LIMIT = 56 * 1024 * 1024


def _cparams(n_axes, vmem=VMEM_LIMIT):
    return pltpu.CompilerParams(dimension_semantics=("arbitrary",) * n_axes, vmem_limit_bytes=vmem)


def _mod_row(tile, tm):
    n_ctx_tiles = T_CTX // tm
    return jnp.where(tile < n_ctx_tiles, CTX_MOD_ROW, (tile - n_ctx_tiles) // (DEC_SEQ // tm))


def _silu(x):
    return x * (1.0 / (1.0 + jnp.exp(-x)))


def _sigmoid(x):
    return 1.0 / (1.0 + jnp.exp(-x))


def _ada_body(cond_ref, w_ref, b_ref, o_ref):
    a = _silu(cond_ref[...]).astype(BF16)
    o_ref[...] = jnp.dot(a, w_ref[...].astype(BF16), preferred_element_type=F32) + b_ref[...]


def ada_modulation(cond, w_ada, b_ada):
    tn = 1024
    n_out = N_MOD * D_MODEL
    return pl.pallas_call(
        _ada_body,
        grid=(DEPTH, n_out // tn),
        in_specs=[
            pl.BlockSpec((MOD_ROWS, D_MODEL), lambda l, n: (0, 0)),
            pl.BlockSpec((None, D_MODEL, tn), lambda l, n: (l, 0, n)),
            pl.BlockSpec((None, 1, tn), lambda l, n: (l, 0, n)),
        ],
        out_specs=pl.BlockSpec((None, MOD_ROWS, tn), lambda l, n: (l, 0, n)),
        out_shape=jax.ShapeDtypeStruct((DEPTH, MOD_ROWS, n_out), F32),
        compiler_params=_cparams(2),
        name="ada_modulation",
    )(cond, w_ada, b_ada.reshape(DEPTH, 1, n_out))


def _prenorm_body(x_ref, g_ref, sh_ref, sc_ref, *rest, with_dt):
    x = x_ref[...]
    y = x * jax.lax.rsqrt(jnp.mean(x * x, axis=-1, keepdims=True) + EPS) * g_ref[...]
    h = (y * (1.0 + sc_ref[...]) + sh_ref[...]).astype(BF16)
    if with_dt:
        wdt_ref, h_ref, dt_ref = rest
        dt_ref[...] = jax.lax.dot_general(h, wdt_ref[...].astype(BF16), (((1,), (1,)), ((), ())),
                                          preferred_element_type=F32)
    else:
        (h_ref,) = rest
    h_ref[...] = h


def prenorm(x, gain, mod, layer, shift_idx, w_in_t=None):
    tm = 512
    with_dt = w_in_t is not None
    mod_spec = lambda k: pl.BlockSpec((None, None, None, 1, D_MODEL),
                                      lambda i: (layer, _mod_row(i, tm), k, 0, 0))
    in_specs = [
        pl.BlockSpec((tm, D_MODEL), lambda i: (i, 0)),
        pl.BlockSpec((None, 1, D_MODEL), lambda i: (layer, 0, 0)),
        mod_spec(shift_idx),
        mod_spec(shift_idx + 1),
    ]
    args = [x, gain.reshape(DEPTH, 1, D_MODEL), mod, mod]
    out_specs = [pl.BlockSpec((tm, D_MODEL), lambda i: (i, 0))]
    out_shape = [jax.ShapeDtypeStruct((T_ALL, D_MODEL), BF16)]
    if with_dt:
        in_specs.append(pl.BlockSpec((None, LANE, D_MODEL), lambda i: (layer, R1_COLS // LANE, 0)))
        args.append(w_in_t)
        out_specs.append(pl.BlockSpec((tm, LANE), lambda i: (i, 0)))
        out_shape.append(jax.ShapeDtypeStruct((T_ALL, LANE), F32))
    res = pl.pallas_call(
        functools.partial(_prenorm_body, with_dt=with_dt),
        grid=(T_ALL // tm,),
        in_specs=in_specs,
        out_specs=out_specs,
        out_shape=out_shape,
        compiler_params=_cparams(1),
        name="prenorm_dt" if with_dt else "prenorm",
    )(*args)
    return res if with_dt else res[0]


PROJ_TN = 1536


def _proj_body(h_ref, w_ref, o_ref, wbf_ref):
    @pl.when(pl.program_id(1) == 0)
    def _():
        wbf_ref[...] = w_ref[...].astype(BF16)

    o_ref[...] = jax.lax.dot_general(h_ref[...], wbf_ref[...], (((1,), (1,)), ((), ())),
                                     preferred_element_type=F32).astype(o_ref.dtype)


def in_projection(h, w_in_t, layer):
    tm, tn = 1024, PROJ_TN
    first_row = lambda n: pl.multiple_of(n * tn + jnp.where(n >= R1_COLS // tn, DT_COLS, 0), DT_COLS)
    return pl.pallas_call(
        _proj_body,
        grid=(PR_COLS // tn, T_ALL // tm),
        in_specs=[
            pl.BlockSpec((tm, D_MODEL), lambda n, m: (m, 0)),
            pl.BlockSpec((None, pl.Element(tn), pl.Element(D_MODEL)), lambda n, m: (layer, first_row(n), 0)),
        ],
        out_specs=pl.BlockSpec((tm, tn), lambda n, m: (m, n)),
        out_shape=jax.ShapeDtypeStruct((T_ALL, PR_COLS), BF16),
        scratch_shapes=[pltpu.VMEM((tn, D_MODEL), BF16)],
        compiler_params=_cparams(2),
        name="in_projection",
    )(h, w_in_t)


MERGE_TN = 512
MERGE_K = QW + SSM_INNER + QW


def _merge_body(ya_ref, yb_ref, yc_ref, ga_ref, gb_ref, gc_ref, wa_ref, wb_ref, wc_ref, o_ref, wbf_ref):
    @pl.when(pl.program_id(1) == 0)
    def _():
        wbf_ref[0:QW, :] = wa_ref[...].astype(BF16)
        wbf_ref[QW:QW + SSM_INNER, :] = wb_ref[...].astype(BF16)
        wbf_ref[QW + SSM_INNER:MERGE_K, :] = wc_ref[...].astype(BF16)

    br_a = jnp.dot(ya_ref[...], wbf_ref[0:QW, :], preferred_element_type=F32)
    br_b = jnp.dot(yb_ref[...], wbf_ref[QW:QW + SSM_INNER, :], preferred_element_type=F32)
    br_c = jnp.dot(yc_ref[...], wbf_ref[QW + SSM_INNER:MERGE_K, :], preferred_element_type=F32)
    merged = (_sigmoid(ga_ref[...].astype(F32)) * br_a + _sigmoid(gb_ref[...].astype(F32)) * br_b
              + _sigmoid(gc_ref[...].astype(F32)) * br_c)
    o_ref[...] = merged.astype(BF16)


def merge_branches(ya, yb, yc, pr, w_oa, w_ob, w_oc, layer):
    tm, tn = 512, MERGE_TN
    gate_spec = lambda k: pl.BlockSpec((tm, tn), lambda n, m: (m, (OFF_GATES + k * D_MODEL) // tn + n))
    w_spec = lambda rows: pl.BlockSpec((None, rows, tn), lambda n, m: (layer, 0, n))
    return pl.pallas_call(
        _merge_body,
        grid=(D_MODEL // tn, T_ALL // tm),
        in_specs=[
            pl.BlockSpec((tm, QW), lambda n, m: (m, 0)),
            pl.BlockSpec((tm, SSM_INNER), lambda n, m: (m, 0)),
            pl.BlockSpec((tm, QW), lambda n, m: (m, 0)),
            gate_spec(0), gate_spec(1), gate_spec(2),
            w_spec(QW), w_spec(SSM_INNER), w_spec(QW),
        ],
        out_specs=pl.BlockSpec((tm, tn), lambda n, m: (m, n)),
        out_shape=jax.ShapeDtypeStruct((T_ALL, D_MODEL), BF16),
        scratch_shapes=[pltpu.VMEM((MERGE_K, tn), BF16)],
        compiler_params=_cparams(2),
        name="merge_branches",
    )(ya, yb, yc, pr, pr, pr, w_oa, w_ob, w_oc)


def _outproj_body(a_ref, w_ref, x_ref, g_ref, o_ref, wbf_ref):
    @pl.when(pl.program_id(1) == 0)
    def _():
        wbf_ref[...] = w_ref[...].astype(BF16)

    o_ref[...] = x_ref[...] + g_ref[...] * jnp.dot(a_ref[...], wbf_ref[...], preferred_element_type=F32)


def out_projection(merged, x, w_out, mod, layer):
    tm, tn = 512, 1024
    return pl.pallas_call(
        _outproj_body,
        grid=(D_MODEL // tn, T_ALL // tm),
        in_specs=[
            pl.BlockSpec((tm, D_MODEL), lambda n, m: (m, 0)),
            pl.BlockSpec((None, D_MODEL, tn), lambda n, m: (layer, 0, n)),
            pl.BlockSpec((tm, tn), lambda n, m: (m, n)),
            pl.BlockSpec((None, None, None, 1, tn), lambda n, m: (layer, _mod_row(m, tm), 2, 0, n)),
        ],
        out_specs=pl.BlockSpec((tm, tn), lambda n, m: (m, n)),
        out_shape=jax.ShapeDtypeStruct((T_ALL, D_MODEL), F32),
        scratch_shapes=[pltpu.VMEM((D_MODEL, tn), BF16)],
        compiler_params=_cparams(2),
        name="out_projection",
    )(merged, w_out, x, mod)


MLP_TF = 512
MLP_TN = 512
MLP_EPI_ROWS = 256


def _mlp_body(h_ref, w1_ref, w2_ref, x_ref, g_ref, fg_ref, o_ref, *, final_norm):
    f = pl.program_id(1)
    hid = jnp.dot(h_ref[...], w1_ref[...], preferred_element_type=F32)
    hid = jnp.square(jnp.maximum(hid, 0.0)).astype(BF16)
    for n0 in range(0, D_MODEL, MLP_TN):
        cols = slice(n0, n0 + MLP_TN)
        part = jnp.dot(hid, w2_ref[:, cols], preferred_element_type=F32)

        @pl.when(f == 0)
        def _():
            o_ref[:, cols] = part

        @pl.when(f > 0)
        def _():
            o_ref[:, cols] += part

    @pl.when(f == pl.num_programs(1) - 1)
    def _():
        for r0 in range(0, o_ref.shape[0], MLP_EPI_ROWS):
            rows = slice(r0, r0 + MLP_EPI_ROWS)
            y = x_ref[rows, :] + g_ref[...] * o_ref[rows, :]
            if final_norm:
                y = y * jax.lax.rsqrt(jnp.mean(y * y, axis=-1, keepdims=True) + EPS) * fg_ref[...]
            o_ref[rows, :] = y


def mlp_residual(h2, x1, w_mlp1, w_mlp2, mod, final_gain, layer, final_norm):
    tm, tf = 1024, MLP_TF
    return pl.pallas_call(
        functools.partial(_mlp_body, final_norm=final_norm),
        grid=(T_ALL // tm, D_FF // tf),
        in_specs=[
            pl.BlockSpec((tm, D_MODEL), lambda m, f: (m, 0)),
            pl.BlockSpec((None, D_MODEL, tf), lambda m, f: (layer, 0, f)),
            pl.BlockSpec((None, tf, D_MODEL), lambda m, f: (layer, f, 0)),
            pl.BlockSpec((tm, D_MODEL), lambda m, f: (m, 0), pipeline_mode=pl.Buffered(1)),
            pl.BlockSpec((None, None, None, 1, D_MODEL), lambda m, f: (layer, _mod_row(m, tm), 5, 0, 0)),
            pl.BlockSpec((1, D_MODEL), lambda m, f: (0, 0)),
        ],
        out_specs=pl.BlockSpec((tm, D_MODEL), lambda m, f: (m, 0)),
        out_shape=jax.ShapeDtypeStruct((T_ALL, D_MODEL), F32),
        compiler_params=_cparams(2),
        name="mlp_final" if final_norm else "mlp",
    )(h2, w_mlp1, w_mlp2, x1, mod, final_gain.reshape(1, D_MODEL))


ATT_TQ = 256


def _head_rms(x, g):
    return x * jax.lax.rsqrt(jnp.mean(x * x, axis=-1, keepdims=True) + EPS) * g


def _rope(x, cos2, sin2):
    return x * cos2 + pltpu.roll(x, HALF_LANE, 1) * sin2


def _attn_body(*refs, layer, n_ctx, seq_len, use_sink, band, qk_norm, rope, emit_kv, n_alias):
    it = iter(refs)
    q_ref, k_ref, v_ref = next(it), next(it), next(it)
    kctx_ref = vctx_ref = sink_ref = qg_ref = kg_ref = cosq_ref = sinq_ref = cosk_ref = sink_k_ref = None
    if n_ctx:
        kctx_ref, vctx_ref = next(it), next(it)
    if use_sink:
        sink_ref = next(it)
    if qk_norm:
        qg_ref, kg_ref = next(it), next(it)
    if rope:
        cosq_ref, sinq_ref, cosk_ref, sink_k_ref = next(it), next(it), next(it), next(it)
    for _ in range(n_alias):
        next(it)
    o_ref = next(it)
    kout_ref = vout_ref = None
    if emit_kv:
        kout_ref, vout_ref = next(it), next(it)
    kall_ref, vall_ref = next(it), next(it)

    j = pl.program_id(1)
    n = pl.program_id(2)

    @pl.when(n == 0)
    def _():
        k = k_ref[...].astype(F32)
        if qk_norm:
            k = _head_rms(k, kg_ref[...])
        if emit_kv:
            kout_ref[...] = k
            vout_ref[...] = v_ref[...].astype(F32)
        if rope:
            k = _rope(k, cosk_ref[...], sink_k_ref[...])
        if n_ctx:
            kall_ref[0:n_ctx, :] = kctx_ref[...].astype(BF16)
            vall_ref[0:n_ctx, :] = vctx_ref[...].astype(BF16)
        kall_ref[n_ctx:n_ctx + seq_len, :] = k.astype(BF16)
        vall_ref[n_ctx:n_ctx + seq_len, :] = v_ref[...]

    n_keys = n_ctx + seq_len
    if band:
        qpos = n * ATT_TQ + jax.lax.broadcasted_iota(jnp.int32, (ATT_TQ, n_keys), 0)
        col = jax.lax.broadcasted_iota(jnp.int32, (ATT_TQ, n_keys), 1)
        visible = (col < n_ctx) | (jnp.abs(col - n_ctx - qpos) <= A_WINDOW)
    scale = HEAD_DIM ** -0.5
    for g in range(Q_PER_KV):
        q = q_ref[:, g * HEAD_DIM:(g + 1) * HEAD_DIM].astype(F32)
        if qk_norm:
            q = _head_rms(q, qg_ref[...])
        if rope:
            q = _rope(q, cosq_ref[...], sinq_ref[...])
        q = (q * scale).astype(BF16)
        s = jax.lax.dot_general(q, kall_ref[...], (((1,), (1,)), ((), ())), preferred_element_type=F32)
        if band:
            s = jnp.where(visible, s, -jnp.inf)
        m = jnp.max(s, axis=-1, keepdims=True)
        if use_sink:
            sk = sink_ref[layer * Q_HEADS + j * Q_PER_KV + g]
            m = jnp.maximum(m, sk)
        p = jnp.exp(s - m)
        den = jnp.sum(p, axis=-1, keepdims=True)
        if use_sink:
            den = den + jnp.exp(sk - m)
        o = jnp.dot(p.astype(BF16), vall_ref[...], preferred_element_type=F32)
        o_ref[:, g * HEAD_DIM:(g + 1) * HEAD_DIM] = (o * (1.0 / den)).astype(BF16)


def attention(pr, layer, *, latent, mixer, y_prev=None, ctx_k=None, ctx_v=None, sink=None,
              q_gain=None, k_gain=None, rope_tabs=None, kv_prev=None):
    n_seq, seq_len, row0 = (DEC_BATCH, DEC_SEQ, T_CTX) if latent else (BATCH, SEQ, 0)
    n_ctx = PAST_LEN if latent else 0
    base = 0 if mixer == "a" else R1_COLS
    use_sink = mixer == "a"
    qk_norm = mixer == "c"
    band = latent and mixer == "a"
    rope = latent
    emit_kv = not latent
    tq = ATT_TQ
    qblocks = seq_len // tq
    grid = (n_seq, KV_HEADS, qblocks)
    qw_kv = Q_PER_KV * HEAD_DIM

    in_specs = [
        pl.BlockSpec((tq, qw_kv), lambda b, j, n: (row0 // tq + b * qblocks + n, (base + OFF_Q) // qw_kv + j)),
        pl.BlockSpec((seq_len, HEAD_DIM), lambda b, j, n: (row0 // seq_len + b, (base + OFF_K) // HEAD_DIM + j)),
        pl.BlockSpec((seq_len, HEAD_DIM), lambda b, j, n: (row0 // seq_len + b, (base + OFF_V) // HEAD_DIM + j)),
    ]
    args = [pr, pr, pr]
    if n_ctx:
        cache_spec = pl.BlockSpec((None, None, PAST_LEN, HEAD_DIM), lambda b, j, n: (b, layer, 0, j))
        in_specs += [cache_spec, cache_spec]
        args += [ctx_k.reshape(DEC_BATCH, DEPTH, PAST_LEN, KVW), ctx_v.reshape(DEC_BATCH, DEPTH, PAST_LEN, KVW)]
    if use_sink:
        in_specs.append(pl.BlockSpec(memory_space=pltpu.SMEM))
        args.append(sink.reshape(DEPTH * Q_HEADS))
    if qk_norm:
        gain_spec = pl.BlockSpec((None, 1, HEAD_DIM), lambda b, j, n: (layer, 0, 0))
        in_specs += [gain_spec, gain_spec]
        args += [q_gain.reshape(DEPTH, 1, HEAD_DIM), k_gain.reshape(DEPTH, 1, HEAD_DIM)]
    if rope:
        cos2, sin2 = rope_tabs
        in_specs += [pl.BlockSpec((tq, HEAD_DIM), lambda b, j, n: (n, 0))] * 2
        in_specs += [pl.BlockSpec((seq_len, HEAD_DIM), lambda b, j, n: (0, 0))] * 2
        args += [cos2, sin2, cos2, sin2]

    y_shape = jax.ShapeDtypeStruct((T_ALL, QW), BF16)
    y_spec = pl.BlockSpec((tq, qw_kv), lambda b, j, n: (row0 // tq + b * qblocks + n, j))
    out_specs, out_shape, aliases = [y_spec], [y_shape], {}
    if emit_kv:
        kv_shape = jax.ShapeDtypeStruct((BATCH, DEPTH, SEQ, KVW), F32)
        kv_spec = pl.BlockSpec((None, None, SEQ, HEAD_DIM), lambda b, j, n: (b, layer, 0, j))
        out_specs += [kv_spec, kv_spec]
        out_shape += [kv_shape, kv_shape]
        if kv_prev is not None:
            aliases[len(args)] = 1
            aliases[len(args) + 1] = 2
            in_specs += [pl.BlockSpec(memory_space=pl.ANY)] * 2
            args += list(kv_prev)
    if y_prev is not None:
        aliases[len(args)] = 0
        in_specs.append(pl.BlockSpec(memory_space=pl.ANY))
        args.append(y_prev)

    body = functools.partial(_attn_body, layer=layer, n_ctx=n_ctx, seq_len=seq_len, use_sink=use_sink, band=band,
                             qk_norm=qk_norm, rope=rope, emit_kv=emit_kv, n_alias=len(aliases))
    return pl.pallas_call(
        body,
        grid=grid,
        in_specs=in_specs,
        out_specs=out_specs,
        out_shape=out_shape,
        input_output_aliases=aliases,
        scratch_shapes=[pltpu.VMEM((n_ctx + seq_len, HEAD_DIM), BF16)] * 2,
        compiler_params=_cparams(3),
        name=f"attn_{mixer}_{'lat' if latent else 'ctx'}",
    )(*args)


SSD_COLBLK = 512
N_Z_BLK = SSM_INNER // SSD_COLBLK
N_XBC_BLK = CONV_CH // SSD_COLBLK
HALO = 16
GROUP_W = SSM_INNER // SSM_GROUPS
PAIRS = SSM_HEADS // 2
PAIRS_PER_GROUP = PAIRS // SSM_GROUPS


def _softplus(x):
    return jnp.maximum(x, 0.0) + jnp.log1p(jnp.exp(-jnp.abs(x)))


def _split3(x):
    hi = x.astype(BF16)
    r = x - hi.astype(F32)
    mid = r.astype(BF16)
    lo = (r - mid.astype(F32)).astype(BF16)
    return hi, mid, lo


def _ssd_body(*refs, seq_len, has_h0, emit_state, n_alias):
    it = iter(refs)
    z_refs = [next(it) for _ in range(N_Z_BLK)]
    xbc_refs = [next(it) for _ in range(N_XBC_BLK)]
    dt_ref, convw_ref, convb_ref, alog_ref, dtb_ref, dexp_ref, ng_ref = (next(it) for _ in range(7))
    h0_ref = next(it) if has_h0 else None
    for _ in range(n_alias):
        next(it)
    y_ref = next(it)
    st_ref = next(it) if emit_state else None
    conv_scr, y_scr, h_scr = next(it), next(it), next(it)

    n_chunks = seq_len // SSD_CHUNK
    q = SSD_CHUNK
    lane = jax.lax.broadcasted_iota(jnp.int32, (q, LANE), 1)
    row = jax.lax.broadcasted_iota(jnp.int32, (q, LANE), 0)
    low_half = lane < HALF_LANE

    for d in range(2):
        for blk in range(SSM_INNER // LANE):
            cols = slice(blk * LANE, (blk + 1) * LANE)
            if has_h0:
                h_scr[d, :, cols] = h0_ref[d, cols, :].T
            else:
                h_scr[d, :, cols] = jnp.zeros((SSM_STATE, LANE), F32)

    def conv_chunk(c, carry):
        r0 = pl.multiple_of(c * q, q)
        prev0 = pl.multiple_of(jnp.maximum(r0 - HALO, 0), HALO)
        next0 = pl.multiple_of(jnp.minimum(r0 + q, seq_len - HALO), HALO)
        has_prev = (r0 > 0).astype(F32)
        has_next = (r0 + q < seq_len).astype(F32)
        for j in range(CONV_CH // LANE):
            src = xbc_refs[j // (SSD_COLBLK // LANE)]
            sc = slice((j % (SSD_COLBLK // LANE)) * LANE, (j % (SSD_COLBLK // LANE) + 1) * LANE)
            cols = slice(j * LANE, (j + 1) * LANE)
            u = src[pl.ds(r0, q), sc].astype(F32)
            prev_row = src[pl.ds(prev0, HALO), sc].astype(F32)[HALO - 1:HALO, :] * has_prev
            next_row = src[pl.ds(next0, HALO), sc].astype(F32)[0:1, :] * has_next
            up = jnp.where(row == 0, prev_row, pltpu.roll(u, 1, 0))
            dn = jnp.where(row == q - 1, next_row, pltpu.roll(u, q - 1, 0))
            v = (convw_ref[0:1, cols] * up + convw_ref[1:2, cols] * u + convw_ref[2:3, cols] * dn
                 + convb_ref[:, cols])
            act = _silu(v)
            conv_scr[pl.ds(r0, q), cols] = act.astype(BF16)
            if j < SSM_INNER // LANE:
                y_scr[pl.ds(r0, q), cols] = dexp_ref[:, cols] * act
        return carry

    jax.lax.fori_loop(0, n_chunks, conv_chunk, 0)

    def scan_chunk(c, d):
        r0 = pl.multiple_of(c * q, q)
        edge = q - 1 if d == 0 else 0
        vis = (row >= lane) if d == 0 else (row <= lane)
        tri = jnp.where(vis, 1.0, 0.0).astype(BF16)
        dt = _softplus(dt_ref[pl.ds(r0, q), :] + dtb_ref[...])
        a = dt * (-jnp.exp(alog_ref[...]))
        a_hi, a_mid, a_lo = _split3(a)
        acum = (jnp.dot(tri, a_hi, preferred_element_type=F32) + jnp.dot(tri, a_mid, preferred_element_type=F32)
                + jnp.dot(tri, a_lo, preferred_element_type=F32))
        fs = jnp.exp(acum)
        acum_t = acum.T
        dt_t = dt.T
        row_t = acum_t - jnp.log(dt_t)
        w_t = dt_t * jnp.exp(acum_t[:, edge:edge + 1] - acum_t)
        for g in range(SSM_GROUPS):
            b_g = conv_scr[pl.ds(r0, q), SSM_INNER + g * SSM_STATE:SSM_INNER + (g + 1) * SSM_STATE]
            c_lo = SSM_INNER + SSM_GROUPS * SSM_STATE + g * SSM_STATE
            c_g = conv_scr[pl.ds(r0, q), c_lo:c_lo + SSM_STATE]
            cb = jax.lax.dot_general(c_g, b_g, (((1,), (1,)), ((), ())), preferred_element_type=F32)
            b_t = b_g.astype(F32).T
            h_g = h_scr[d, :, g * GROUP_W:(g + 1) * GROUP_W].astype(BF16)
            y_off = jnp.dot(c_g, h_g, preferred_element_type=F32)
            for pp in range(PAIRS_PER_GROUP):
                p = g * PAIRS_PER_GROUP + pp
                cols = slice(p * LANE, (p + 1) * LANE)
                lhs_top, lhs_bot = [], []
                for h in (2 * p, 2 * p + 1):
                    ell = d * SSM_HEADS + h
                    diff = acum[:, ell:ell + 1] - row_t[ell:ell + 1, :]
                    lhs_top.append(cb * jnp.exp(jnp.where(vis, diff, -jnp.inf)))
                    lhs_bot.append(b_t * w_t[ell:ell + 1, :])
                lhs = jnp.concatenate([jnp.concatenate(lhs_top, axis=1), jnp.concatenate(lhs_bot, axis=1)],
                                      axis=0).astype(BF16)
                xp = conv_scr[pl.ds(r0, q), cols]
                zero = jnp.zeros_like(xp)
                rhs = jnp.concatenate([jnp.where(low_half, xp, zero), jnp.where(low_half, zero, xp)], axis=0)
                res = jnp.dot(lhs, rhs, preferred_element_type=F32)
                ell_a = d * SSM_HEADS + 2 * p
                factor = jnp.where(low_half, fs[:, ell_a:ell_a + 1], fs[:, ell_a + 1:ell_a + 2])
                y_scr[pl.ds(r0, q), cols] += res[0:q, :] + y_off[:, pp * LANE:(pp + 1) * LANE] * factor
                h_scr[d, :, cols] = h_scr[d, :, cols] * factor[edge:edge + 1, :] + res[q:2 * q, :]

    def scan_step(i, carry):
        scan_chunk(i, 0)
        scan_chunk(n_chunks - 1 - i, 1)
        return carry

    jax.lax.fori_loop(0, n_chunks, scan_step, 0)

    if emit_state:
        for d in range(2):
            for blk in range(SSM_INNER // LANE):
                cols = slice(blk * LANE, (blk + 1) * LANE)
                st_ref[d, cols, :] = h_scr[d, :, cols].T

    def finish_chunk(c, carry):
        r0 = pl.multiple_of(c * q, q)
        gated = []
        for k in range(N_Z_BLK):
            cols = slice(k * SSD_COLBLK, (k + 1) * SSD_COLBLK)
            gated.append(y_scr[pl.ds(r0, q), cols] * _silu(z_refs[k][pl.ds(r0, q), :].astype(F32)))
        ssq = sum(jnp.sum(gk * gk, axis=-1, keepdims=True) for gk in gated)
        inv = jax.lax.rsqrt(ssq * (1.0 / SSM_INNER) + EPS)
        for k in range(N_Z_BLK):
            cols = slice(k * SSD_COLBLK, (k + 1) * SSD_COLBLK)
            y_ref[pl.ds(r0, q), cols] = (gated[k] * inv * ng_ref[:, cols]).astype(BF16)
        return carry

    jax.lax.fori_loop(0, n_chunks, finish_chunk, 0)


def ssd_mixer(pr1, dt_raw, layer, *, latent, conv_w, conv_b, a_log_row, dt_bias_row, d_row, norm_gain,
              h0=None, y_prev=None, state_prev=None):
    n_seq, seq_len, row0 = (DEC_BATCH, DEC_SEQ, T_CTX) if latent else (BATCH, SEQ, 0)
    has_h0 = latent
    emit_state = not latent
    seq_blk = row0 // seq_len

    def col_spec(first, k):
        return pl.BlockSpec((seq_len, SSD_COLBLK), lambda b: (seq_blk + b, first // SSD_COLBLK + k))

    in_specs = [col_spec(OFF_Z, k) for k in range(N_Z_BLK)] + [col_spec(OFF_XBC, k) for k in range(N_XBC_BLK)]
    args = [pr1] * (N_Z_BLK + N_XBC_BLK)
    in_specs += [
        pl.BlockSpec((seq_len, LANE), lambda b: (seq_blk + b, 0)),
        pl.BlockSpec((None, CONV_K, CONV_CH), lambda b: (layer, 0, 0)),
        pl.BlockSpec((None, 1, CONV_CH), lambda b: (layer, 0, 0)),
        pl.BlockSpec((None, 1, LANE), lambda b: (layer, 0, 0)),
        pl.BlockSpec((None, 1, LANE), lambda b: (layer, 0, 0)),
        pl.BlockSpec((None, 1, SSM_INNER), lambda b: (layer, 0, 0)),
        pl.BlockSpec((None, 1, SSM_INNER), lambda b: (layer, 0, 0)),
    ]
    args += [dt_raw, conv_w, conv_b.reshape(DEPTH, 1, CONV_CH), a_log_row, dt_bias_row, d_row,
             norm_gain.reshape(DEPTH, 1, SSM_INNER)]
    state_block = (None, None, 2, SSM_INNER, SSM_STATE)
    if has_h0:
        in_specs.append(pl.BlockSpec(state_block, lambda b: (b, layer, 0, 0, 0)))
        args.append(h0.reshape(DEC_BATCH, DEPTH, 2, SSM_INNER, SSM_STATE))

    out_specs = [pl.BlockSpec((seq_len, SSM_INNER), lambda b: (seq_blk + b, 0))]
    out_shape = [jax.ShapeDtypeStruct((T_ALL, SSM_INNER), BF16)]
    aliases = {}
    if emit_state:
        out_specs.append(pl.BlockSpec(state_block, lambda b: (b, layer, 0, 0, 0)))
        out_shape.append(jax.ShapeDtypeStruct((BATCH, DEPTH, 2, SSM_INNER, SSM_STATE), F32))
        if state_prev is not None:
            aliases[len(args)] = 1
            in_specs.append(pl.BlockSpec(memory_space=pl.ANY))
            args.append(state_prev)
    if y_prev is not None:
        aliases[len(args)] = 0
        in_specs.append(pl.BlockSpec(memory_space=pl.ANY))
        args.append(y_prev)

    return pl.pallas_call(
        functools.partial(_ssd_body, seq_len=seq_len, has_h0=has_h0, emit_state=emit_state, n_alias=len(aliases)),
        grid=(n_seq,),
        in_specs=in_specs,
        out_specs=out_specs,
        out_shape=out_shape,
        input_output_aliases=aliases,
        scratch_shapes=[
            pltpu.VMEM((seq_len, CONV_CH), BF16),
            pltpu.VMEM((seq_len, SSM_INNER), F32),
            pltpu.VMEM((2, SSM_STATE, SSM_INNER), F32),
        ],
        compiler_params=_cparams(1),
        name=f"ssd_{'lat' if latent else 'ctx'}",
    )(*args)


def _rope_tables():
    rows = DEC_SEQ // GRID_W
    r = jnp.repeat(jnp.arange(rows, dtype=F32), GRID_W)
    c = jnp.tile(jnp.arange(GRID_W, dtype=F32), rows)
    n_freq = HEAD_DIM // 4
    inv_freq = ROPE_THETA ** (-jnp.arange(n_freq, dtype=F32) / n_freq)
    ang = jnp.concatenate([r[:, None] * inv_freq, c[:, None] * inv_freq], axis=-1)
    cos, sin = jnp.cos(ang), jnp.sin(ang)
    return jnp.concatenate([cos, cos], axis=-1), jnp.concatenate([-sin, sin], axis=-1)


def kernel(x_prompt, x_sample, cache_a_k, cache_a_v, cache_c_k, cache_c_v, state_ssm, c, c_ctx, norm1_g, w_ada, b_ada, w_in, a_sink, conv_w, conv_b, ssm_a_log, ssm_dt_bias, ssm_d, ssm_norm_g, c_q_norm, c_k_norm, w_oa, w_ob, w_oc, w_out, norm2_g, w_mlp1, w_mlp2, final_norm_g):
    x = jnp.concatenate([x_prompt.reshape(T_CTX, D_MODEL), x_sample.reshape(T_LAT, D_MODEL)], axis=0)
    cond = jnp.concatenate([c, c_ctx[None, :], jnp.zeros((MOD_ROWS - DEC_BATCH - 1, D_MODEL), F32)], axis=0)
    mod = ada_modulation(cond, w_ada, b_ada).reshape(DEPTH, MOD_ROWS, N_MOD, 1, D_MODEL)
    rope_tabs = _rope_tables()
    lane_pad = jnp.zeros((DEPTH, LANE - 2 * SSM_HEADS), F32)
    a_log_row = jnp.concatenate([ssm_a_log.reshape(DEPTH, 2 * SSM_HEADS), lane_pad], axis=1).reshape(DEPTH, 1, LANE)
    dt_bias_row = jnp.concatenate([ssm_dt_bias.reshape(DEPTH, 2 * SSM_HEADS), lane_pad], axis=1).reshape(DEPTH, 1, LANE)
    d_row = jnp.repeat(ssm_d, SSM_HEAD_DIM, axis=1).reshape(DEPTH, 1, SSM_INNER)

    w_in_t = jnp.swapaxes(w_in, 1, 2)
    w_mlp1_bf, w_mlp2_bf = w_mlp1.astype(BF16), w_mlp2.astype(BF16)

    kv_a = kv_c = state = None
    for layer in range(DEPTH):
        h, dt_raw = prenorm(x, norm1_g, mod, layer, 0, w_in_t=w_in_t)
        pr = in_projection(h, w_in_t, layer)

        ya, ka, va = attention(pr, layer, latent=False, mixer="a", sink=a_sink, kv_prev=kv_a)
        kv_a = (ka, va)
        (ya,) = attention(pr, layer, latent=True, mixer="a", sink=a_sink, ctx_k=cache_a_k, ctx_v=cache_a_v,
                          rope_tabs=rope_tabs, y_prev=ya)
        yc, kc, vc = attention(pr, layer, latent=False, mixer="c", q_gain=c_q_norm, k_gain=c_k_norm, kv_prev=kv_c)
        kv_c = (kc, vc)
        (yc,) = attention(pr, layer, latent=True, mixer="c", q_gain=c_q_norm, k_gain=c_k_norm, ctx_k=cache_c_k,
                          ctx_v=cache_c_v, rope_tabs=rope_tabs, y_prev=yc)
        ssd_args = dict(conv_w=conv_w, conv_b=conv_b, a_log_row=a_log_row, dt_bias_row=dt_bias_row, d_row=d_row,
                        norm_gain=ssm_norm_g)
        yb, state = ssd_mixer(pr, dt_raw, layer, latent=False, state_prev=state, **ssd_args)
        (yb,) = ssd_mixer(pr, dt_raw, layer, latent=True, h0=state_ssm, y_prev=yb, **ssd_args)

        merged = merge_branches(ya, yb, yc, pr, w_oa, w_ob, w_oc, layer)
        x1 = out_projection(merged, x, w_out, mod, layer)
        h2 = prenorm(x1, norm2_g, mod, layer, 3)
        x = mlp_residual(h2, x1, w_mlp1_bf, w_mlp2_bf, mod, final_norm_g, layer, final_norm=layer == DEPTH - 1)

    y_prompt = x[:T_CTX].reshape(BATCH, SEQ, D_MODEL)
    y_sample = x[T_CTX:].reshape(DEC_BATCH, DEC_SEQ, D_MODEL)
    kv5 = (BATCH, DEPTH, SEQ, KV_HEADS, HEAD_DIM)
    new_state = state.reshape(BATCH, DEPTH, 2, SSM_HEADS, SSM_HEAD_DIM, SSM_STATE)
    return (y_prompt, y_sample, kv_a[0].reshape(kv5), kv_a[1].reshape(kv5), kv_c[0].reshape(kv5),
            kv_c[1].reshape(kv5), new_state)
```

```python
import functools
import math

import jax
import jax.numpy as jnp
from jax.experimental import pallas as pl
from jax.experimental.pallas import tpu as pltpu

F32 = jnp.float32
BF16 = jnp.bfloat16

D_MODEL = 2048
BATCH = 16
SEQ = 256
DEPTH = 2
DEC_BATCH = 4
DEC_SEQ = 1024
PAST_LEN = 256
GRID_W = 64
HEAD_DIM = 128
ROPE_THETA = 10000.0
EPS = 1e-6
Q_HEADS = 8
KV_HEADS = 2
Q_PER_KV = Q_HEADS // KV_HEADS
A_WINDOW = 128
SSM_HEADS = 32
SSM_HEAD_DIM = 64
SSM_GROUPS = 2
SSM_STATE = 128
CONV_K = 3
D_FF = 4 * D_MODEL
QW = Q_HEADS * HEAD_DIM
KVW = KV_HEADS * HEAD_DIM
SSM_INNER = SSM_HEADS * SSM_HEAD_DIM
CONV_CH = SSM_INNER + 2 * SSM_GROUPS * SSM_STATE
N_MOD = 6

T_CTX = BATCH * SEQ
T_LAT = DEC_BATCH * DEC_SEQ
T_ALL = T_CTX + T_LAT
MOD_ROWS = 8
CTX_MOD_ROW = DEC_BATCH

R1_COLS = QW + 2 * KVW + SSM_INNER + CONV_CH
DT_COLS = 2 * SSM_HEADS
R2_COLS = QW + 2 * KVW + 3 * D_MODEL
PR_COLS = R1_COLS + R2_COLS
OFF_Q, OFF_K, OFF_V = 0, QW, QW + KVW
OFF_Z = QW + 2 * KVW
OFF_XBC = OFF_Z + SSM_INNER
OFF_GATES = R1_COLS + QW + 2 * KVW

LANE = 128
HALF_LANE = LANE // 2
SSD_CHUNK = 128
VMEM_LIMIT = 56 * 1024 * 1024


def _cparams(n_axes, vmem=VMEM_LIMIT):
    return pltpu.CompilerParams(dimension_semantics=("arbitrary",) * n_axes, vmem_limit_bytes=vmem)


def _mod_row(tile, tm):
    n_ctx_tiles = T_CTX // tm
    return jnp.where(tile < n_ctx_tiles, CTX_MOD_ROW, (tile - n_ctx_tiles) // (DEC_SEQ // tm))


def _silu(x):
    return x * (1.0 / (1.0 + jnp.exp(-x)))


def _sigmoid(x):
    return 1.0 / (1.0 + jnp.exp(-x))


def _ada_body(cond_ref, w_ref, b_ref, o_ref):
    a = _silu(cond_ref[...]).astype(BF16)
    o_ref[...] = jnp.dot(a, w_ref[...].astype(BF16), preferred_element_type=F32) + b_ref[...]


def ada_modulation(cond, w_ada, b_ada):
    tn = 1024
    n_out = N_MOD * D_MODEL
    return pl.pallas_call(
        _ada_body,
        grid=(DEPTH, n_out // tn),
        in_specs=[
            pl.BlockSpec((MOD_ROWS, D_MODEL), lambda l, n: (0, 0)),
            pl.BlockSpec((None, D_MODEL, tn), lambda l, n: (l, 0, n)),
            pl.BlockSpec((None, 1, tn), lambda l, n: (l, 0, n)),
        ],
        out_specs=pl.BlockSpec((None, MOD_ROWS, tn), lambda l, n: (l, 0, n)),
        out_shape=jax.ShapeDtypeStruct((DEPTH, MOD_ROWS, n_out), F32),
        compiler_params=_cparams(2),
        name="ada_modulation",
    )(cond, w_ada, b_ada.reshape(DEPTH, 1, n_out))


def _prenorm_body(x_ref, g_ref, sh_ref, sc_ref, *rest, with_dt):
    x = x_ref[...]
    y = x * jax.lax.rsqrt(jnp.mean(x * x, axis=-1, keepdims=True) + EPS) * g_ref[...]
    h = (y * (1.0 + sc_ref[...]) + sh_ref[...]).astype(BF16)
    if with_dt:
        wdt_ref, h_ref, dt_ref = rest
        dt_ref[...] = jax.lax.dot_general(h, wdt_ref[...].astype(BF16), (((1,), (1,)), ((), ())),
                                          preferred_element_type=F32)
    else:
        (h_ref,) = rest
    h_ref[...] = h


def prenorm(x, gain, mod, layer, shift_idx, w_in_t=None):
    tm = 512
    with_dt = w_in_t is not None
    mod_spec = lambda k: pl.BlockSpec((None, None, None, 1, D_MODEL),
                                      lambda i: (layer, _mod_row(i, tm), k, 0, 0))
    in_specs = [
        pl.BlockSpec((tm, D_MODEL), lambda i: (i, 0)),
        pl.BlockSpec((None, 1, D_MODEL), lambda i: (layer, 0, 0)),
        mod_spec(shift_idx),
        mod_spec(shift_idx + 1),
    ]
    args = [x, gain.reshape(DEPTH, 1, D_MODEL), mod, mod]
    out_specs = [pl.BlockSpec((tm, D_MODEL), lambda i: (i, 0))]
    out_shape = [jax.ShapeDtypeStruct((T_ALL, D_MODEL), BF16)]
    if with_dt:
        in_specs.append(pl.BlockSpec((None, LANE, D_MODEL), lambda i: (layer, R1_COLS // LANE, 0)))
        args.append(w_in_t)
        out_specs.append(pl.BlockSpec((tm, LANE), lambda i: (i, 0)))
        out_shape.append(jax.ShapeDtypeStruct((T_ALL, LANE), F32))
    res = pl.pallas_call(
        functools.partial(_prenorm_body, with_dt=with_dt),
        grid=(T_ALL // tm,),
        in_specs=in_specs,
        out_specs=out_specs,
        out_shape=out_shape,
        compiler_params=_cparams(1),
        name="prenorm_dt" if with_dt else "prenorm",
    )(*args)
    return res if with_dt else res[0]


PROJ_TN = 1536


def _proj_body(h_ref, w_ref, o_ref, wbf_ref):
    @pl.when(pl.program_id(1) == 0)
    def _():
        wbf_ref[...] = w_ref[...].astype(BF16)

    o_ref[...] = jax.lax.dot_general(h_ref[...], wbf_ref[...], (((1,), (1,)), ((), ())),
                                     preferred_element_type=F32).astype(o_ref.dtype)


def in_projection(h, w_in_t, layer):
    tm, tn = 1024, PROJ_TN
    first_row = lambda n: pl.multiple_of(n * tn + jnp.where(n >= R1_COLS // tn, DT_COLS, 0), DT_COLS)
    return pl.pallas_call(
        _proj_body,
        grid=(PR_COLS // tn, T_ALL // tm),
        in_specs=[
            pl.BlockSpec((tm, D_MODEL), lambda n, m: (m, 0)),
            pl.BlockSpec((None, pl.Element(tn), pl.Element(D_MODEL)), lambda n, m: (layer, first_row(n), 0)),
        ],
        out_specs=pl.BlockSpec((tm, tn), lambda n, m: (m, n)),
        out_shape=jax.ShapeDtypeStruct((T_ALL, PR_COLS), BF16),
        scratch_shapes=[pltpu.VMEM((tn, D_MODEL), BF16)],
        compiler_params=_cparams(2),
        name="in_projection",
    )(h, w_in_t)


MERGE_TN = 512
MERGE_K = QW + SSM_INNER + QW


def _merge_body(ya_ref, yb_ref, yc_ref, ga_ref, gb_ref, gc_ref, wa_ref, wb_ref, wc_ref, o_ref, wbf_ref):
    @pl.when(pl.program_id(1) == 0)
    def _():
        wbf_ref[0:QW, :] = wa_ref[...].astype(BF16)
        wbf_ref[QW:QW + SSM_INNER, :] = wb_ref[...].astype(BF16)
        wbf_ref[QW + SSM_INNER:MERGE_K, :] = wc_ref[...].astype(BF16)

    br_a = jnp.dot(ya_ref[...], wbf_ref[0:QW, :], preferred_element_type=F32)
    br_b = jnp.dot(yb_ref[...], wbf_ref[QW:QW + SSM_INNER, :], preferred_element_type=F32)
    br_c = jnp.dot(yc_ref[...], wbf_ref[QW + SSM_INNER:MERGE_K, :], preferred_element_type=F32)
    merged = (_sigmoid(ga_ref[...].astype(F32)) * br_a + _sigmoid(gb_ref[...].astype(F32)) * br_b
              + _sigmoid(gc_ref[...].astype(F32)) * br_c)
    o_ref[...] = merged.astype(BF16)


def merge_branches(ya, yb, yc, pr, w_oa, w_ob, w_oc, layer):
    tm, tn = 1024, MERGE_TN
    gate_spec = lambda k: pl.BlockSpec((tm, tn), lambda n, m: (m, (OFF_GATES + k * D_MODEL) // tn + n))
    w_spec = lambda rows: pl.BlockSpec((None, rows, tn), lambda n, m: (layer, 0, n), pipeline_mode=pl.Buffered(1))
    return pl.pallas_call(
        _merge_body,
        grid=(D_MODEL // tn, T_ALL // tm),
        in_specs=[
            pl.BlockSpec((tm, QW), lambda n, m: (m, 0)),
            pl.BlockSpec((tm, SSM_INNER), lambda n, m: (m, 0)),
            pl.BlockSpec((tm, QW), lambda n, m: (m, 0)),
            gate_spec(0), gate_spec(1), gate_spec(2),
            w_spec(QW), w_spec(SSM_INNER), w_spec(QW),
        ],
        out_specs=pl.BlockSpec((tm, tn), lambda n, m: (m, n)),
        out_shape=jax.ShapeDtypeStruct((T_ALL, D_MODEL), BF16),
        scratch_shapes=[pltpu.VMEM((MERGE_K, tn), BF16)],
        compiler_params=_cparams(2),
        name="merge_branches",
    )(ya, yb, yc, pr, pr, pr, w_oa, w_ob, w_oc)


CAST_ROWS = 2048


def _cast_weight(w_ref, wbf_ref):
    for r in range(0, w_ref.shape[0], CAST_ROWS):
        wbf_ref[r:r + CAST_ROWS, :] = w_ref[r:r + CAST_ROWS, :].astype(BF16)


def _gated_residual_body(a_ref, w_ref, x_ref, g_ref, o_ref, wbf_ref):
    @pl.when(pl.program_id(1) == 0)
    def _():
        _cast_weight(w_ref, wbf_ref)

    o_ref[...] = x_ref[...] + g_ref[...] * jnp.dot(a_ref[...], wbf_ref[...], preferred_element_type=F32)


def gated_residual_matmul(a, w, x, mod, layer, gate_idx, *, tm, tn, single_buffer_w, name):
    k = a.shape[1]
    w_mode = dict(pipeline_mode=pl.Buffered(1)) if single_buffer_w else {}
    return pl.pallas_call(
        _gated_residual_body,
        grid=(D_MODEL // tn, T_ALL // tm),
        in_specs=[
            pl.BlockSpec((tm, k), lambda n, m: (m, 0)),
            pl.BlockSpec((None, k, tn), lambda n, m: (layer, 0, n), **w_mode),
            pl.BlockSpec((tm, tn), lambda n, m: (m, n)),
            pl.BlockSpec((None, None, None, 1, tn), lambda n, m: (layer, _mod_row(m, tm), gate_idx, 0, n)),
        ],
        out_specs=pl.BlockSpec((tm, tn), lambda n, m: (m, n)),
        out_shape=jax.ShapeDtypeStruct((T_ALL, D_MODEL), F32),
        scratch_shapes=[pltpu.VMEM((k, tn), BF16)],
        compiler_params=_cparams(2),
        name=name,
    )(a, w, x, mod)


def out_projection(merged, x, w_out, mod, layer):
    return gated_residual_matmul(merged, w_out, x, mod, layer, 2, tm=1024, tn=1024, single_buffer_w=False,
                                 name="out_projection")


def _mlp_up_body(h_ref, w_ref, o_ref, wbf_ref):
    @pl.when(pl.program_id(1) == 0)
    def _():
        _cast_weight(w_ref, wbf_ref)

    hid = jnp.dot(h_ref[...], wbf_ref[...], preferred_element_type=F32)
    o_ref[...] = jnp.square(jnp.maximum(hid, 0.0)).astype(BF16)


def mlp_up(h2, w_mlp1, layer):
    tm, tn = 1024, 1024
    return pl.pallas_call(
        _mlp_up_body,
        grid=(D_FF // tn, T_ALL // tm),
        in_specs=[
            pl.BlockSpec((tm, D_MODEL), lambda n, m: (m, 0)),
            pl.BlockSpec((None, D_MODEL, tn), lambda n, m: (layer, 0, n)),
        ],
        out_specs=pl.BlockSpec((tm, tn), lambda n, m: (m, n)),
        out_shape=jax.ShapeDtypeStruct((T_ALL, D_FF), BF16),
        scratch_shapes=[pltpu.VMEM((D_MODEL, tn), BF16)],
        compiler_params=_cparams(2),
        name="mlp_up",
    )(h2, w_mlp1)


def mlp_down(hid, x1, w_mlp2, mod, layer):
    return gated_residual_matmul(hid, w_mlp2, x1, mod, layer, 5, tm=512, tn=512, single_buffer_w=True,
                                 name="mlp_down")


FINAL_TM = 512
N_CTX_TILES = T_CTX // FINAL_TM


def _final_norm_body(x_ref, g_ref, yp_ref, ys_ref):
    x = x_ref[...]
    y = x * jax.lax.rsqrt(jnp.mean(x * x, axis=-1, keepdims=True) + EPS) * g_ref[...]
    i = pl.program_id(0)

    @pl.when(i < N_CTX_TILES)
    def _():
        yp_ref[...] = y

    @pl.when(i >= N_CTX_TILES)
    def _():
        ys_ref[...] = y


def final_norm(x, gain):
    tm = FINAL_TM
    return pl.pallas_call(
        _final_norm_body,
        grid=(T_ALL // tm,),
        in_specs=[
            pl.BlockSpec((tm, D_MODEL), lambda i: (i, 0)),
            pl.BlockSpec((1, D_MODEL), lambda i: (0, 0)),
        ],
        out_specs=[
            pl.BlockSpec((tm, D_MODEL), lambda i: (jnp.minimum(i, N_CTX_TILES - 1), 0)),
            pl.BlockSpec((tm, D_MODEL), lambda i: (jnp.maximum(i - N_CTX_TILES, 0), 0)),
        ],
        out_shape=[jax.ShapeDtypeStruct((T_CTX, D_MODEL), F32), jax.ShapeDtypeStruct((T_LAT, D_MODEL), F32)],
        compiler_params=_cparams(1),
        name="final_norm",
    )(x, gain.reshape(1, D_MODEL))


ATT_TQ = 256


def _head_rms(x, g):
    return x * jax.lax.rsqrt(jnp.mean(x * x, axis=-1, keepdims=True) + EPS) * g


def _rope(x, cos2, sin2):
    return x * cos2 + pltpu.roll(x, HALF_LANE, 1) * sin2


def _attn_body(*refs, layer, n_ctx, seq_len, use_sink, band, qk_norm, rope, emit_kv, n_alias):
    it = iter(refs)
    q_ref, k_ref, v_ref = next(it), next(it), next(it)
    kctx_ref = vctx_ref = sink_ref = qg_ref = kg_ref = cosq_ref = sinq_ref = cosk_ref = sink_k_ref = None
    if n_ctx:
        kctx_ref, vctx_ref = next(it), next(it)
    if use_sink:
        sink_ref = next(it)
    if qk_norm:
        qg_ref, kg_ref = next(it), next(it)
    if rope:
        cosq_ref, sinq_ref, cosk_ref, sink_k_ref = next(it), next(it), next(it), next(it)
    for _ in range(n_alias):
        next(it)
    o_ref = next(it)
    kout_ref = vout_ref = None
    if emit_kv:
        kout_ref, vout_ref = next(it), next(it)
    kall_ref, vall_ref = next(it), next(it)

    j = pl.program_id(1)
    n = pl.program_id(2)

    @pl.when(n == 0)
    def _():
        k = k_ref[...].astype(F32)
        if qk_norm:
            k = _head_rms(k, kg_ref[...])
        if emit_kv:
            kout_ref[...] = k
            vout_ref[...] = v_ref[...].astype(F32)
        if rope:
            k = _rope(k, cosk_ref[...], sink_k_ref[...])
        if n_ctx:
            kall_ref[0:n_ctx, :] = kctx_ref[...].astype(BF16)
            vall_ref[0:n_ctx, :] = vctx_ref[...].astype(BF16)
        kall_ref[n_ctx:n_ctx + seq_len, :] = k.astype(BF16)
        vall_ref[n_ctx:n_ctx + seq_len, :] = v_ref[...]

    n_keys = n_ctx + seq_len
    if band:
        qpos = n * ATT_TQ + jax.lax.broadcasted_iota(jnp.int32, (ATT_TQ, n_keys), 0)
        col = jax.lax.broadcasted_iota(jnp.int32, (ATT_TQ, n_keys), 1)
        visible = (col < n_ctx) | (jnp.abs(col - n_ctx - qpos) <= A_WINDOW)
    scale = HEAD_DIM ** -0.5
    for g in range(Q_PER_KV):
        q = q_ref[:, g * HEAD_DIM:(g + 1) * HEAD_DIM].astype(F32)
        if qk_norm:
            q = _head_rms(q, qg_ref[...])
        if rope:
            q = _rope(q, cosq_ref[...], sinq_ref[...])
        q = (q * scale).astype(BF16)
        s = jax.lax.dot_general(q, kall_ref[...], (((1,), (1,)), ((), ())), preferred_element_type=F32)
        if band:
            s = jnp.where(visible, s, -jnp.inf)
        m = jnp.max(s, axis=-1, keepdims=True)
        if use_sink:
            sk = sink_ref[layer * Q_HEADS + j * Q_PER_KV + g]
            m = jnp.maximum(m, sk)
        p = jnp.exp(s - m)
        den = jnp.sum(p, axis=-1, keepdims=True)
        if use_sink:
            den = den + jnp.exp(sk - m)
        o = jnp.dot(p.astype(BF16), vall_ref[...], preferred_element_type=F32)
        o_ref[:, g * HEAD_DIM:(g + 1) * HEAD_DIM] = (o * (1.0 / den)).astype(BF16)


def attention(pr, layer, *, latent, mixer, y_prev=None, ctx_k=None, ctx_v=None, sink=None,
              q_gain=None, k_gain=None, rope_tabs=None, kv_prev=None):
    n_seq, seq_len, row0 = (DEC_BATCH, DEC_SEQ, T_CTX) if latent else (BATCH, SEQ, 0)
    n_ctx = PAST_LEN if latent else 0
    base = 0 if mixer == "a" else R1_COLS
    use_sink = mixer == "a"
    qk_norm = mixer == "c"
    band = latent and mixer == "a"
    rope = latent
    emit_kv = not latent
    tq = ATT_TQ
    qblocks = seq_len // tq
    grid = (n_seq, KV_HEADS, qblocks)
    qw_kv = Q_PER_KV * HEAD_DIM

    in_specs = [
        pl.BlockSpec((tq, qw_kv), lambda b, j, n: (row0 // tq + b * qblocks + n, (base + OFF_Q) // qw_kv + j)),
        pl.BlockSpec((seq_len, HEAD_DIM), lambda b, j, n: (row0 // seq_len + b, (base + OFF_K) // HEAD_DIM + j)),
        pl.BlockSpec((seq_len, HEAD_DIM), lambda b, j, n: (row0 // seq_len + b, (base + OFF_V) // HEAD_DIM + j)),
    ]
    args = [pr, pr, pr]
    if n_ctx:
        cache_spec = pl.BlockSpec((None, None, PAST_LEN, HEAD_DIM), lambda b, j, n: (b, layer, 0, j))
        in_specs += [cache_spec, cache_spec]
        args += [ctx_k.reshape(DEC_BATCH, DEPTH, PAST_LEN, KVW), ctx_v.reshape(DEC_BATCH, DEPTH, PAST_LEN, KVW)]
    if use_sink:
        in_specs.append(pl.BlockSpec(memory_space=pltpu.SMEM))
        args.append(sink.reshape(DEPTH * Q_HEADS))
    if qk_norm:
        gain_spec = pl.BlockSpec((None, 1, HEAD_DIM), lambda b, j, n: (layer, 0, 0))
        in_specs += [gain_spec, gain_spec]
        args += [q_gain.reshape(DEPTH, 1, HEAD_DIM), k_gain.reshape(DEPTH, 1, HEAD_DIM)]
    if rope:
        cos2, sin2 = rope_tabs
        in_specs += [pl.BlockSpec((tq, HEAD_DIM), lambda b, j, n: (n, 0))] * 2
        in_specs += [pl.BlockSpec((seq_len, HEAD_DIM), lambda b, j, n: (0, 0))] * 2
        args += [cos2, sin2, cos2, sin2]

    y_shape = jax.ShapeDtypeStruct((T_ALL, QW), BF16)
    y_spec = pl.BlockSpec((tq, qw_kv), lambda b, j, n: (row0 // tq + b * qblocks + n, j))
    out_specs, out_shape, aliases = [y_spec], [y_shape], {}
    if emit_kv:
        kv_shape = jax.ShapeDtypeStruct((BATCH, DEPTH, SEQ, KVW), F32)
        kv_spec = pl.BlockSpec((None, None, SEQ, HEAD_DIM), lambda b, j, n: (b, layer, 0, j))
        out_specs += [kv_spec, kv_spec]
        out_shape += [kv_shape, kv_shape]
        if kv_prev is not None:
            aliases[len(args)] = 1
            aliases[len(args) + 1] = 2
            in_specs += [pl.BlockSpec(memory_space=pl.ANY)] * 2
            args += list(kv_prev)
    if y_prev is not None:
        aliases[len(args)] = 0
        in_specs.append(pl.BlockSpec(memory_space=pl.ANY))
        args.append(y_prev)

    body = functools.partial(_attn_body, layer=layer, n_ctx=n_ctx, seq_len=seq_len, use_sink=use_sink, band=band,
                             qk_norm=qk_norm, rope=rope, emit_kv=emit_kv, n_alias=len(aliases))
    return pl.pallas_call(
        body,
        grid=grid,
        in_specs=in_specs,
        out_specs=out_specs,
        out_shape=out_shape,
        input_output_aliases=aliases,
        scratch_shapes=[pltpu.VMEM((n_ctx + seq_len, HEAD_DIM), BF16)] * 2,
        compiler_params=_cparams(3),
        name=f"attn_{mixer}_{'lat' if latent else 'ctx'}",
    )(*args)


SSD_COLBLK = 512
N_Z_BLK = SSM_INNER // SSD_COLBLK
N_XBC_BLK = CONV_CH // SSD_COLBLK
HALO = 16
GROUP_W = SSM_INNER // SSM_GROUPS
PAIRS = SSM_HEADS // 2
PAIRS_PER_GROUP = PAIRS // SSM_GROUPS
LOG2E = math.log2(math.e)


def _softplus(x):
    return jnp.maximum(x, 0.0) + jnp.log1p(jnp.exp(-jnp.abs(x)))


def _split3(x):
    hi = x.astype(BF16)
    r = x - hi.astype(F32)
    mid = r.astype(BF16)
    lo = (r - mid.astype(F32)).astype(BF16)
    return hi, mid, lo


def _ssd_body(*refs, seq_len, has_h0, emit_state, n_alias):
    it = iter(refs)
    z_refs = [next(it) for _ in range(N_Z_BLK)]
    xbc_refs = [next(it) for _ in range(N_XBC_BLK)]
    dt_ref, convw_ref, convb_ref, alog_ref, dtb_ref, dexp_ref, ng_ref = (next(it) for _ in range(7))
    h0_ref = next(it) if has_h0 else None
    for _ in range(n_alias):
        next(it)
    y_ref = next(it)
    st_ref = next(it) if emit_state else None
    conv_scr, y_scr, h_scr = next(it), next(it), next(it)

    n_chunks = seq_len // SSD_CHUNK
    q = SSD_CHUNK
    lane = jax.lax.broadcasted_iota(jnp.int32, (q, LANE), 1)
    row = jax.lax.broadcasted_iota(jnp.int32, (q, LANE), 0)
    low_half = lane < HALF_LANE

    for d in range(2):
        for blk in range(SSM_INNER // LANE):
            cols = slice(blk * LANE, (blk + 1) * LANE)
            if has_h0:
                h_scr[d, :, cols] = h0_ref[d, cols, :].T
            else:
                h_scr[d, :, cols] = jnp.zeros((SSM_STATE, LANE), F32)

    def conv_chunk(c, carry):
        r0 = pl.multiple_of(c * q, q)
        prev0 = pl.multiple_of(jnp.maximum(r0 - HALO, 0), HALO)
        next0 = pl.multiple_of(jnp.minimum(r0 + q, seq_len - HALO), HALO)
        has_prev = (r0 > 0).astype(F32)
        has_next = (r0 + q < seq_len).astype(F32)
        for j in range(CONV_CH // LANE):
            src = xbc_refs[j // (SSD_COLBLK // LANE)]
            sc = slice((j % (SSD_COLBLK // LANE)) * LANE, (j % (SSD_COLBLK // LANE) + 1) * LANE)
            cols = slice(j * LANE, (j + 1) * LANE)
            u = src[pl.ds(r0, q), sc].astype(F32)
            prev_row = src[pl.ds(prev0, HALO), sc].astype(F32)[HALO - 1:HALO, :] * has_prev
            next_row = src[pl.ds(next0, HALO), sc].astype(F32)[0:1, :] * has_next
            up = jnp.where(row == 0, prev_row, pltpu.roll(u, 1, 0))
            dn = jnp.where(row == q - 1, next_row, pltpu.roll(u, q - 1, 0))
            v = (convw_ref[0:1, cols] * up + convw_ref[1:2, cols] * u + convw_ref[2:3, cols] * dn
                 + convb_ref[:, cols])
            act = _silu(v)
            conv_scr[pl.ds(r0, q), cols] = act.astype(BF16)
            if j < SSM_INNER // LANE:
                y_scr[pl.ds(r0, q), cols] = dexp_ref[:, cols] * act
        return carry

    jax.lax.fori_loop(0, n_chunks, conv_chunk, 0)

    def scan_chunk(c, d):
        r0 = pl.multiple_of(c * q, q)
        edge = q - 1 if d == 0 else 0
        vis = (row >= lane) if d == 0 else (row <= lane)
        tri = jnp.where(vis, 1.0, 0.0).astype(BF16)
        dt = _softplus(dt_ref[pl.ds(r0, q), :] + dtb_ref[...])
        a = dt * (-jnp.exp(alog_ref[...]))
        a_hi, a_mid, a_lo = _split3(a)
        acum = (jnp.dot(tri, a_hi, preferred_element_type=F32) + jnp.dot(tri, a_mid, preferred_element_type=F32)
                + jnp.dot(tri, a_lo, preferred_element_type=F32))
        acum2 = acum * LOG2E
        acum_t = acum.T
        dt_t = dt.T
        row_t2 = (acum_t - jnp.log(dt_t)) * LOG2E
        w_t = dt_t * jnp.exp(acum_t[:, edge:edge + 1] - acum_t)
        for g in range(SSM_GROUPS):
            b_g = conv_scr[pl.ds(r0, q), SSM_INNER + g * SSM_STATE:SSM_INNER + (g + 1) * SSM_STATE]
            c_lo = SSM_INNER + SSM_GROUPS * SSM_STATE + g * SSM_STATE
            c_g = conv_scr[pl.ds(r0, q), c_lo:c_lo + SSM_STATE]
            cb = jax.lax.dot_general(c_g, b_g, (((1,), (1,)), ((), ())),
                                     preferred_element_type=F32).astype(BF16)
            b_t = b_g.astype(F32).T.astype(BF16)
            h_g = h_scr[d, :, g * GROUP_W:(g + 1) * GROUP_W].astype(BF16)
            y_off = jnp.dot(c_g, h_g, preferred_element_type=F32)
            for pp in range(PAIRS_PER_GROUP):
                p = g * PAIRS_PER_GROUP + pp
                cols = slice(p * LANE, (p + 1) * LANE)
                lhs_top, lhs_bot, decay = [], [], []
                for h in (2 * p, 2 * p + 1):
                    ell = d * SSM_HEADS + h
                    col = jnp.broadcast_to(acum2[:, ell:ell + 1], (q, LANE))
                    seg = jnp.exp2(jnp.where(vis, col - row_t2[ell:ell + 1, :], -jnp.inf))
                    lhs_top.append(cb * seg.astype(BF16))
                    lhs_bot.append(b_t * w_t[ell:ell + 1, :].astype(BF16))
                    decay.append(jnp.exp2(col))
                lhs = jnp.concatenate([jnp.concatenate(lhs_top, axis=1), jnp.concatenate(lhs_bot, axis=1)],
                                      axis=0)
                xp = conv_scr[pl.ds(r0, q), cols]
                zero = jnp.zeros_like(xp)
                rhs = jnp.concatenate([jnp.where(low_half, xp, zero), jnp.where(low_half, zero, xp)], axis=0)
                res = jnp.dot(lhs, rhs, preferred_element_type=F32)
                factor = jnp.where(low_half, decay[0], decay[1])
                y_scr[pl.ds(r0, q), cols] += res[0:q, :] + y_off[:, pp * LANE:(pp + 1) * LANE] * factor
                h_scr[d, :, cols] = h_scr[d, :, cols] * factor[edge:edge + 1, :] + res[q:2 * q, :]

    def scan_step(i, carry):
        scan_chunk(i, 0)
        scan_chunk(n_chunks - 1 - i, 1)
        return carry

    jax.lax.fori_loop(0, n_chunks, scan_step, 0)

    if emit_state:
        for d in range(2):
            for blk in range(SSM_INNER // LANE):
                cols = slice(blk * LANE, (blk + 1) * LANE)
                st_ref[d, cols, :] = h_scr[d, :, cols].T

    def finish_chunk(c, carry):
        r0 = pl.multiple_of(c * q, q)
        gated = []
        for k in range(N_Z_BLK):
            cols = slice(k * SSD_COLBLK, (k + 1) * SSD_COLBLK)
            gated.append(y_scr[pl.ds(r0, q), cols] * _silu(z_refs[k][pl.ds(r0, q), :].astype(F32)))
        ssq = sum(jnp.sum(gk * gk, axis=-1, keepdims=True) for gk in gated)
        inv = jax.lax.rsqrt(ssq * (1.0 / SSM_INNER) + EPS)
        for k in range(N_Z_BLK):
            cols = slice(k * SSD_COLBLK, (k + 1) * SSD_COLBLK)
            y_ref[pl.ds(r0, q), cols] = (gated[k] * inv * ng_ref[:, cols]).astype(BF16)
        return carry

    jax.lax.fori_loop(0, n_chunks, finish_chunk, 0)


def ssd_mixer(pr1, dt_raw, layer, *, latent, conv_w, conv_b, a_log_row, dt_bias_row, d_row, norm_gain,
              h0=None, y_prev=None, state_prev=None):
    n_seq, seq_len, row0 = (DEC_BATCH, DEC_SEQ, T_CTX) if latent else (BATCH, SEQ, 0)
    has_h0 = latent
    emit_state = not latent
    seq_blk = row0 // seq_len

    def col_spec(first, k):
        return pl.BlockSpec((seq_len, SSD_COLBLK), lambda b: (seq_blk + b, first // SSD_COLBLK + k))

    in_specs = [col_spec(OFF_Z, k) for k in range(N_Z_BLK)] + [col_spec(OFF_XBC, k) for k in range(N_XBC_BLK)]
    args = [pr1] * (N_Z_BLK + N_XBC_BLK)
    in_specs += [
        pl.BlockSpec((seq_len, LANE), lambda b: (seq_blk + b, 0)),
        pl.BlockSpec((None, CONV_K, CONV_CH), lambda b: (layer, 0, 0)),
        pl.BlockSpec((None, 1, CONV_CH), lambda b: (layer, 0, 0)),
        pl.BlockSpec((None, 1, LANE), lambda b: (layer, 0, 0)),
        pl.BlockSpec((None, 1, LANE), lambda b: (layer, 0, 0)),
        pl.BlockSpec((None, 1, SSM_INNER), lambda b: (layer, 0, 0)),
        pl.BlockSpec((None, 1, SSM_INNER), lambda b: (layer, 0, 0)),
    ]
    args += [dt_raw, conv_w, conv_b.reshape(DEPTH, 1, CONV_CH), a_log_row, dt_bias_row, d_row,
             norm_gain.reshape(DEPTH, 1, SSM_INNER)]
    state_block = (None, None, 2, SSM_INNER, SSM_STATE)
    if has_h0:
        in_specs.append(pl.BlockSpec(state_block, lambda b: (b, layer, 0, 0, 0)))
        args.append(h0.reshape(DEC_BATCH, DEPTH, 2, SSM_INNER, SSM_STATE))

    out_specs = [pl.BlockSpec((seq_len, SSM_INNER), lambda b: (seq_blk + b, 0))]
    out_shape = [jax.ShapeDtypeStruct((T_ALL, SSM_INNER), BF16)]
    aliases = {}
    if emit_state:
        out_specs.append(pl.BlockSpec(state_block, lambda b: (b, layer, 0, 0, 0)))
        out_shape.append(jax.ShapeDtypeStruct((BATCH, DEPTH, 2, SSM_INNER, SSM_STATE), F32))
        if state_prev is not None:
            aliases[len(args)] = 1
            in_specs.append(pl.BlockSpec(memory_space=pl.ANY))
            args.append(state_prev)
    if y_prev is not None:
        aliases[len(args)] = 0
        in_specs.append(pl.BlockSpec(memory_space=pl.ANY))
        args.append(y_prev)

    return pl.pallas_call(
        functools.partial(_ssd_body, seq_len=seq_len, has_h0=has_h0, emit_state=emit_state, n_alias=len(aliases)),
        grid=(n_seq,),
        in_specs=in_specs,
        out_specs=out_specs,
        out_shape=out_shape,
        input_output_aliases=aliases,
        scratch_shapes=[
            pltpu.VMEM((seq_len, CONV_CH), BF16),
            pltpu.VMEM((seq_len, SSM_INNER), F32),
            pltpu.VMEM((2, SSM_STATE, SSM_INNER), F32),
        ],
        compiler_params=_cparams(1),
        name=f"ssd_{'lat' if latent else 'ctx'}",
    )(*args)


def _rope_tables():
    rows = DEC_SEQ // GRID_W
    r = jnp.repeat(jnp.arange(rows, dtype=F32), GRID_W)
    c = jnp.tile(jnp.arange(GRID_W, dtype=F32), rows)
    n_freq = HEAD_DIM // 4
    inv_freq = ROPE_THETA ** (-jnp.arange(n_freq, dtype=F32) / n_freq)
    ang = jnp.concatenate([r[:, None] * inv_freq, c[:, None] * inv_freq], axis=-1)
    cos, sin = jnp.cos(ang), jnp.sin(ang)
    return jnp.concatenate([cos, cos], axis=-1), jnp.concatenate([-sin, sin], axis=-1)


def kernel(x_prompt, x_sample, cache_a_k, cache_a_v, cache_c_k, cache_c_v, state_ssm, c, c_ctx, norm1_g, w_ada, b_ada, w_in, a_sink, conv_w, conv_b, ssm_a_log, ssm_dt_bias, ssm_d, ssm_norm_g, c_q_norm, c_k_norm, w_oa, w_ob, w_oc, w_out, norm2_g, w_mlp1, w_mlp2, final_norm_g):
    x = jnp.concatenate([x_prompt.reshape(T_CTX, D_MODEL), x_sample.reshape(T_LAT, D_MODEL)], axis=0)
    cond = jnp.concatenate([c, c_ctx[None, :], jnp.zeros((MOD_ROWS - DEC_BATCH - 1, D_MODEL), F32)], axis=0)
    mod = ada_modulation(cond, w_ada, b_ada).reshape(DEPTH, MOD_ROWS, N_MOD, 1, D_MODEL)
    rope_tabs = _rope_tables()
    lane_pad = jnp.zeros((DEPTH, LANE - 2 * SSM_HEADS), F32)
    a_log_row = jnp.concatenate([ssm_a_log.reshape(DEPTH, 2 * SSM_HEADS), lane_pad], axis=1).reshape(DEPTH, 1, LANE)
    dt_bias_row = jnp.concatenate([ssm_dt_bias.reshape(DEPTH, 2 * SSM_HEADS), lane_pad], axis=1).reshape(DEPTH, 1, LANE)
    d_row = jnp.repeat(ssm_d, SSM_HEAD_DIM, axis=1).reshape(DEPTH, 1, SSM_INNER)

    w_in_t = jnp.swapaxes(w_in, 1, 2)

    kv_a = kv_c = state = None
    for layer in range(DEPTH):
        h, dt_raw = prenorm(x, norm1_g, mod, layer, 0, w_in_t=w_in_t)
        pr = in_projection(h, w_in_t, layer)

        ya, ka, va = attention(pr, layer, latent=False, mixer="a", sink=a_sink, kv_prev=kv_a)
        kv_a = (ka, va)
        (ya,) = attention(pr, layer, latent=True, mixer="a", sink=a_sink, ctx_k=cache_a_k, ctx_v=cache_a_v,
                          rope_tabs=rope_tabs, y_prev=ya)
        yc, kc, vc = attention(pr, layer, latent=False, mixer="c", q_gain=c_q_norm, k_gain=c_k_norm, kv_prev=kv_c)
        kv_c = (kc, vc)
        (yc,) = attention(pr, layer, latent=True, mixer="c", q_gain=c_q_norm, k_gain=c_k_norm, ctx_k=cache_c_k,
                          ctx_v=cache_c_v, rope_tabs=rope_tabs, y_prev=yc)
        ssd_args = dict(conv_w=conv_w, conv_b=conv_b, a_log_row=a_log_row, dt_bias_row=dt_bias_row, d_row=d_row,
                        norm_gain=ssm_norm_g)
        yb, state = ssd_mixer(pr, dt_raw, layer, latent=False, state_prev=state, **ssd_args)
        (yb,) = ssd_mixer(pr, dt_raw, layer, latent=True, h0=state_ssm, y_prev=yb, **ssd_args)

        merged = merge_branches(ya, yb, yc, pr, w_oa, w_ob, w_oc, layer)
        x1 = out_projection(merged, x, w_out, mod, layer)
        h2 = prenorm(x1, norm2_g, mod, layer, 3)
        x = mlp_down(mlp_up(h2, w_mlp1, layer), x1, w_mlp2, mod, layer)

    y_prompt, y_sample = final_norm(x, final_norm_g)
    y_prompt = y_prompt.reshape(BATCH, SEQ, D_MODEL)
    y_sample = y_sample.reshape(DEC_BATCH, DEC_SEQ, D_MODEL)
    kv5 = (BATCH, DEPTH, SEQ, KV_HEADS, HEAD_DIM)
    new_state = state.reshape(BATCH, DEPTH, 2, SSM_HEADS, SSM_HEAD_DIM, SSM_STATE)
    return (y_prompt, y_sample, kv_a[0].reshape(kv5), kv_a[1].reshape(kv5), kv_c[0].reshape(kv5),
            kv_c[1].reshape(kv5), new_state)
```

```python
import functools
import math

import jax
import jax.numpy as jnp
from jax.experimental import pallas as pl
from jax.experimental.pallas import tpu as pltpu

F32 = jnp.float32
BF16 = jnp.bfloat16

D_MODEL = 2048
BATCH = 16
SEQ = 256
DEPTH = 2
DEC_BATCH = 4
DEC_SEQ = 1024
PAST_LEN = 256
GRID_W = 64
HEAD_DIM = 128
ROPE_THETA = 10000.0
EPS = 1e-6
Q_HEADS = 8
KV_HEADS = 2
Q_PER_KV = Q_HEADS // KV_HEADS
A_WINDOW = 128
SSM_HEADS = 32
SSM_HEAD_DIM = 64
SSM_GROUPS = 2
SSM_STATE = 128
CONV_K = 3
D_FF = 4 * D_MODEL
QW = Q_HEADS * HEAD_DIM
KVW = KV_HEADS * HEAD_DIM
SSM_INNER = SSM_HEADS * SSM_HEAD_DIM
CONV_CH = SSM_INNER + 2 * SSM_GROUPS * SSM_STATE
N_MOD = 6

T_CTX = BATCH * SEQ
T_LAT = DEC_BATCH * DEC_SEQ
T_ALL = T_CTX + T_LAT
MOD_ROWS = 8
CTX_MOD_ROW = DEC_BATCH

R1_COLS = QW + 2 * KVW + SSM_INNER + CONV_CH
DT_COLS = 2 * SSM_HEADS
R2_COLS = QW + 2 * KVW + 3 * D_MODEL
PR_COLS = R1_COLS + R2_COLS
OFF_Q, OFF_K, OFF_V = 0, QW, QW + KVW
OFF_Z = QW + 2 * KVW
OFF_XBC = OFF_Z + SSM_INNER
OFF_GATES = R1_COLS + QW + 2 * KVW

LANE = 128
HALF_LANE = LANE // 2
SSD_CHUNK = 128
VMEM_LIMIT = 56 * 1024 * 1024
LOG2E = math.log2(math.e)


def _cparams(n_axes, vmem=VMEM_LIMIT):
    return pltpu.CompilerParams(dimension_semantics=("arbitrary",) * n_axes, vmem_limit_bytes=vmem)


def _mod_row(tile, tm):
    n_ctx_tiles = T_CTX // tm
    return jnp.where(tile < n_ctx_tiles, CTX_MOD_ROW, (tile - n_ctx_tiles) // (DEC_SEQ // tm))


def _ctx_tile(tile, tm):
    return jnp.minimum(tile, T_CTX // tm - 1)


def _lat_tile(tile, tm):
    return jnp.maximum(tile - T_CTX // tm, 0)


def _silu(x):
    return x * (1.0 / (1.0 + jnp.exp(-x)))


def _sigmoid(x):
    return 1.0 / (1.0 + jnp.exp(-x))


def _rms(x, gain):
    return x * jax.lax.rsqrt(jnp.mean(x * x, axis=-1, keepdims=True) + EPS) * gain


def _ada_body(cond_ref, w_ref, b_ref, o_ref):
    a = _silu(cond_ref[...]).astype(BF16)
    o_ref[...] = jnp.dot(a, w_ref[...].astype(BF16), preferred_element_type=F32) + b_ref[...]


def ada_modulation(cond, w_ada, b_ada):
    tn = 1024
    n_out = N_MOD * D_MODEL
    return pl.pallas_call(
        _ada_body,
        grid=(DEPTH, n_out // tn),
        in_specs=[
            pl.BlockSpec((MOD_ROWS, D_MODEL), lambda l, n: (0, 0)),
            pl.BlockSpec((None, D_MODEL, tn), lambda l, n: (l, 0, n)),
            pl.BlockSpec((None, 1, tn), lambda l, n: (l, 0, n)),
        ],
        out_specs=pl.BlockSpec((None, MOD_ROWS, tn), lambda l, n: (l, 0, n)),
        out_shape=jax.ShapeDtypeStruct((DEPTH, MOD_ROWS, n_out), F32),
        compiler_params=_cparams(2),
        name="ada_modulation",
    )(cond, w_ada, b_ada.reshape(DEPTH, 1, n_out))


PRENORM_TM = 512


def _prenorm_body(*refs, split_x):
    if split_x:
        xc_ref, xl_ref, g_ref, sh_ref, sc_ref, wdt_ref, h_ref, dt_ref = refs
        x = jnp.where(pl.program_id(0) < T_CTX // PRENORM_TM, xc_ref[...], xl_ref[...])
    else:
        x_ref, g_ref, sh_ref, sc_ref, wdt_ref, h_ref, dt_ref = refs
        x = x_ref[...]
    h = (_rms(x, g_ref[...]) * (1.0 + sc_ref[...]) + sh_ref[...]).astype(BF16)
    dt_ref[...] = jax.lax.dot_general(h, wdt_ref[...].astype(BF16), (((1,), (1,)), ((), ())),
                                      preferred_element_type=F32)
    h_ref[...] = h


def prenorm(x, gain, mod, layer, w_in_t):
    tm = PRENORM_TM
    split_x = isinstance(x, tuple)
    mod_spec = lambda k: pl.BlockSpec((None, None, None, 1, D_MODEL),
                                      lambda i: (layer, _mod_row(i, tm), k, 0, 0))
    if split_x:
        x_specs = [pl.BlockSpec((tm, D_MODEL), lambda i: (_ctx_tile(i, tm), 0)),
                   pl.BlockSpec((tm, D_MODEL), lambda i: (_lat_tile(i, tm), 0))]
        x_args = list(x)
    else:
        x_specs = [pl.BlockSpec((tm, D_MODEL), lambda i: (i, 0))]
        x_args = [x]
    return pl.pallas_call(
        functools.partial(_prenorm_body, split_x=split_x),
        grid=(T_ALL // tm,),
        in_specs=x_specs + [
            pl.BlockSpec((None, 1, D_MODEL), lambda i: (layer, 0, 0)),
            mod_spec(0),
            mod_spec(1),
            pl.BlockSpec((None, LANE, D_MODEL), lambda i: (layer, R1_COLS // LANE, 0)),
        ],
        out_specs=[pl.BlockSpec((tm, D_MODEL), lambda i: (i, 0)), pl.BlockSpec((tm, LANE), lambda i: (i, 0))],
        out_shape=[jax.ShapeDtypeStruct((T_ALL, D_MODEL), BF16), jax.ShapeDtypeStruct((T_ALL, LANE), F32)],
        compiler_params=_cparams(1),
        name="prenorm_dt",
    )(*x_args, gain.reshape(DEPTH, 1, D_MODEL), mod, mod, w_in_t)


PROJ_TN = 1536
CAST_ROWS = 2048


def _cast_weight(w_ref, wbf_ref):
    for r in range(0, w_ref.shape[0], CAST_ROWS):
        rows = slice(r, min(r + CAST_ROWS, w_ref.shape[0]))
        wbf_ref[rows, :] = w_ref[rows, :].astype(BF16)


def _proj_body(h_ref, w_ref, o_ref, wbf_ref):
    @pl.when(pl.program_id(1) == 0)
    def _():
        _cast_weight(w_ref, wbf_ref)

    o_ref[...] = jax.lax.dot_general(h_ref[...], wbf_ref[...], (((1,), (1,)), ((), ())),
                                     preferred_element_type=F32).astype(o_ref.dtype)


def in_projection(h, w_in_t, layer):
    tm, tn = 1024, PROJ_TN
    first_row = lambda n: pl.multiple_of(n * tn + jnp.where(n >= R1_COLS // tn, DT_COLS, 0), DT_COLS)
    return pl.pallas_call(
        _proj_body,
        grid=(PR_COLS // tn, T_ALL // tm),
        in_specs=[
            pl.BlockSpec((tm, D_MODEL), lambda n, m: (m, 0)),
            pl.BlockSpec((None, pl.Element(tn), pl.Element(D_MODEL)), lambda n, m: (layer, first_row(n), 0)),
        ],
        out_specs=pl.BlockSpec((tm, tn), lambda n, m: (m, n)),
        out_shape=jax.ShapeDtypeStruct((T_ALL, PR_COLS), BF16),
        scratch_shapes=[pltpu.VMEM((tn, D_MODEL), BF16)],
        compiler_params=_cparams(2),
        name="in_projection",
    )(h, w_in_t)


MERGE_TM = 512
MERGE_TN = 512
MERGE_K = QW + SSM_INNER + QW


def _merge_body(yac_ref, yal_ref, ybc_ref, ybl_ref, ycc_ref, ycl_ref, ga_ref, gb_ref, gc_ref,
                wa_ref, wb_ref, wc_ref, o_ref, wbf_ref):
    m = pl.program_id(1)

    @pl.when(m == 0)
    def _():
        wbf_ref[0:QW, :] = wa_ref[...].astype(BF16)
        wbf_ref[QW:QW + SSM_INNER, :] = wb_ref[...].astype(BF16)
        wbf_ref[QW + SSM_INNER:MERGE_K, :] = wc_ref[...].astype(BF16)

    def merge(ya_ref, yb_ref, yc_ref):
        br_a = jnp.dot(ya_ref[...], wbf_ref[0:QW, :], preferred_element_type=F32)
        br_b = jnp.dot(yb_ref[...], wbf_ref[QW:QW + SSM_INNER, :], preferred_element_type=F32)
        br_c = jnp.dot(yc_ref[...], wbf_ref[QW + SSM_INNER:MERGE_K, :], preferred_element_type=F32)
        merged = (_sigmoid(ga_ref[...].astype(F32)) * br_a + _sigmoid(gb_ref[...].astype(F32)) * br_b
                  + _sigmoid(gc_ref[...].astype(F32)) * br_c)
        o_ref[...] = merged.astype(BF16)

    @pl.when(m < T_CTX // MERGE_TM)
    def _():
        merge(yac_ref, ybc_ref, ycc_ref)

    @pl.when(m >= T_CTX // MERGE_TM)
    def _():
        merge(yal_ref, ybl_ref, ycl_ref)


def merge_branches(ya, yb, yc, pr, w_oa, w_ob, w_oc, layer):
    tm, tn = MERGE_TM, MERGE_TN
    gate_spec = lambda k: pl.BlockSpec((tm, tn), lambda n, m: (m, (OFF_GATES + k * D_MODEL) // tn + n))
    w_spec = lambda rows: pl.BlockSpec((None, rows, tn), lambda n, m: (layer, 0, n))
    pair_specs = lambda cols: [pl.BlockSpec((tm, cols), lambda n, m: (_ctx_tile(m, tm), 0)),
                               pl.BlockSpec((tm, cols), lambda n, m: (_lat_tile(m, tm), 0))]
    return pl.pallas_call(
        _merge_body,
        grid=(D_MODEL // tn, T_ALL // tm),
        in_specs=pair_specs(QW) + pair_specs(SSM_INNER) + pair_specs(QW) + [
            gate_spec(0), gate_spec(1), gate_spec(2),
            w_spec(QW), w_spec(SSM_INNER), w_spec(QW),
        ],
        out_specs=pl.BlockSpec((tm, tn), lambda n, m: (m, n)),
        out_shape=jax.ShapeDtypeStruct((T_ALL, D_MODEL), BF16),
        scratch_shapes=[pltpu.VMEM((MERGE_K, tn), BF16)],
        compiler_params=_cparams(2),
        name="merge_branches",
    )(*ya, *yb, *yc, pr, pr, pr, w_oa, w_ob, w_oc)


OUTPROJ_TM = 512


def _outproj_body(*refs, split_x):
    if split_x:
        a_ref, w_ref, xc_ref, xl_ref, g1_ref, ng_ref, sh_ref, sc_ref, x1_ref, h2_ref = refs
        x = jnp.where(pl.program_id(0) < T_CTX // OUTPROJ_TM, xc_ref[...], xl_ref[...])
    else:
        a_ref, w_ref, x_ref, g1_ref, ng_ref, sh_ref, sc_ref, x1_ref, h2_ref = refs
        x = x_ref[...]
    x1 = x + g1_ref[...] * jnp.dot(a_ref[...], w_ref[...], preferred_element_type=F32)
    x1_ref[...] = x1
    h2_ref[...] = (_rms(x1, ng_ref[...]) * (1.0 + sc_ref[...]) + sh_ref[...]).astype(BF16)


def out_projection_norm(merged, x, w_out_bf, gain2, mod, layer):
    tm = OUTPROJ_TM
    split_x = isinstance(x, tuple)
    mod_spec = lambda k: pl.BlockSpec((None, None, None, 1, D_MODEL), lambda i: (layer, _mod_row(i, tm), k, 0, 0))
    if split_x:
        x_specs = [pl.BlockSpec((tm, D_MODEL), lambda i: (_ctx_tile(i, tm), 0)),
                   pl.BlockSpec((tm, D_MODEL), lambda i: (_lat_tile(i, tm), 0))]
        x_args = list(x)
    else:
        x_specs = [pl.BlockSpec((tm, D_MODEL), lambda i: (i, 0))]
        x_args = [x]
    return pl.pallas_call(
        functools.partial(_outproj_body, split_x=split_x),
        grid=(T_ALL // tm,),
        in_specs=[
            pl.BlockSpec((tm, D_MODEL), lambda i: (i, 0)),
            pl.BlockSpec((None, D_MODEL, D_MODEL), lambda i: (layer, 0, 0), pipeline_mode=pl.Buffered(1)),
        ] + x_specs + [
            mod_spec(2),
            pl.BlockSpec((None, 1, D_MODEL), lambda i: (layer, 0, 0)),
            mod_spec(3),
            mod_spec(4),
        ],
        out_specs=[pl.BlockSpec((tm, D_MODEL), lambda i: (i, 0)), pl.BlockSpec((tm, D_MODEL), lambda i: (i, 0))],
        out_shape=[jax.ShapeDtypeStruct((T_ALL, D_MODEL), F32), jax.ShapeDtypeStruct((T_ALL, D_MODEL), BF16)],
        compiler_params=_cparams(1),
        name="out_projection_norm",
    )(merged, w_out_bf, *x_args, mod, gain2.reshape(DEPTH, 1, D_MODEL), mod, mod)


def _mlp_up_body(h_ref, w_ref, o_ref, wbf_ref):
    @pl.when(pl.program_id(1) == 0)
    def _():
        _cast_weight(w_ref, wbf_ref)

    hid = jnp.dot(h_ref[...], wbf_ref[...], preferred_element_type=F32)
    o_ref[...] = jnp.square(jnp.maximum(hid, 0.0)).astype(BF16)


def mlp_up(h2, w_mlp1, layer):
    tm, tn = 1024, 1024
    return pl.pallas_call(
        _mlp_up_body,
        grid=(D_FF // tn, T_ALL // tm),
        in_specs=[
            pl.BlockSpec((tm, D_MODEL), lambda n, m: (m, 0)),
            pl.BlockSpec((None, D_MODEL, tn), lambda n, m: (layer, 0, n)),
        ],
        out_specs=pl.BlockSpec((tm, tn), lambda n, m: (m, n)),
        out_shape=jax.ShapeDtypeStruct((T_ALL, D_FF), BF16),
        scratch_shapes=[pltpu.VMEM((D_MODEL, tn), BF16)],
        compiler_params=_cparams(2),
        name="mlp_up",
    )(h2, w_mlp1)


def _mlp_down_body(a_ref, w_ref, x_ref, g_ref, o_ref, wbf_ref):
    @pl.when(pl.program_id(1) == 0)
    def _():
        _cast_weight(w_ref, wbf_ref)

    o_ref[...] = x_ref[...] + g_ref[...] * jnp.dot(a_ref[...], wbf_ref[...], preferred_element_type=F32)


def mlp_down(hid, x1, w_mlp2, mod, layer):
    tm, tn = 512, 512
    return pl.pallas_call(
        _mlp_down_body,
        grid=(D_MODEL // tn, T_ALL // tm),
        in_specs=[
            pl.BlockSpec((tm, D_FF), lambda n, m: (m, 0)),
            pl.BlockSpec((None, D_FF, tn), lambda n, m: (layer, 0, n), pipeline_mode=pl.Buffered(1)),
            pl.BlockSpec((tm, tn), lambda n, m: (m, n)),
            pl.BlockSpec((None, None, None, 1, tn), lambda n, m: (layer, _mod_row(m, tm), 5, 0, n)),
        ],
        out_specs=pl.BlockSpec((tm, tn), lambda n, m: (m, n)),
        out_shape=jax.ShapeDtypeStruct((T_ALL, D_MODEL), F32),
        scratch_shapes=[pltpu.VMEM((D_FF, tn), BF16)],
        compiler_params=_cparams(2),
        name="mlp_down",
    )(hid, w_mlp2, x1, mod)


FINAL_TM = 512


def _final_norm_body(x_ref, g_ref, yp_ref, ys_ref):
    y = _rms(x_ref[...], g_ref[...])
    i = pl.program_id(0)

    @pl.when(i < T_CTX // FINAL_TM)
    def _():
        yp_ref[...] = y

    @pl.when(i >= T_CTX // FINAL_TM)
    def _():
        ys_ref[...] = y


def final_norm(x, gain):
    tm = FINAL_TM
    return pl.pallas_call(
        _final_norm_body,
        grid=(T_ALL // tm,),
        in_specs=[
            pl.BlockSpec((tm, D_MODEL), lambda i: (i, 0)),
            pl.BlockSpec((1, D_MODEL), lambda i: (0, 0)),
        ],
        out_specs=[
            pl.BlockSpec((tm, D_MODEL), lambda i: (_ctx_tile(i, tm), 0)),
            pl.BlockSpec((tm, D_MODEL), lambda i: (_lat_tile(i, tm), 0)),
        ],
        out_shape=[jax.ShapeDtypeStruct((T_CTX, D_MODEL), F32), jax.ShapeDtypeStruct((T_LAT, D_MODEL), F32)],
        compiler_params=_cparams(1),
        name="final_norm",
    )(x, gain.reshape(1, D_MODEL))


ATT_TQ = 256
ATT_WIN = ATT_TQ + 2 * A_WINDOW


def _rope(x, cos2, sin2):
    return x * cos2 + pltpu.roll(x, HALF_LANE, 1) * sin2


def _attn_body(*refs, layer, n_ctx, seq_len, use_sink, band, qk_norm, rope, emit_kv):
    it = iter(refs)
    q_ref, k_ref, v_ref = next(it), next(it), next(it)
    kctx_ref = vctx_ref = sink_ref = qg_ref = kg_ref = cosq_ref = sinq_ref = cosk_ref = sink_k_ref = None
    kprev_ref = vprev_ref = kout_ref = vout_ref = None
    if n_ctx:
        kctx_ref, vctx_ref = next(it), next(it)
    if use_sink:
        sink_ref = next(it)
    if qk_norm:
        qg_ref, kg_ref = next(it), next(it)
    if rope:
        cosq_ref, sinq_ref, cosk_ref, sink_k_ref = next(it), next(it), next(it), next(it)
    if emit_kv and layer:
        kprev_ref, vprev_ref = next(it), next(it)
    o_ref = next(it)
    if emit_kv:
        kout_ref, vout_ref = next(it), next(it)
    kall_ref, vall_ref = next(it), next(it)

    j = pl.program_id(1)
    n = pl.program_id(2)

    @pl.when(n == 0)
    def _():
        k = k_ref[...].astype(F32)
        if qk_norm:
            k = _rms(k, kg_ref[...])
        if emit_kv:
            if layer:
                kout_ref[0:layer] = kprev_ref[...]
                vout_ref[0:layer] = vprev_ref[...]
            kout_ref[layer] = k
            vout_ref[layer] = v_ref[...].astype(F32)
        if rope:
            k = _rope(k, cosk_ref[...], sink_k_ref[...])
        if n_ctx:
            kall_ref[0:n_ctx, :] = kctx_ref[...].astype(BF16)
            vall_ref[0:n_ctx, 0:HEAD_DIM] = vctx_ref[...].astype(BF16)
        kall_ref[n_ctx:n_ctx + seq_len, :] = k.astype(BF16)
        vall_ref[n_ctx:n_ctx + seq_len, 0:HEAD_DIM] = v_ref[...]
        vall_ref[:, HEAD_DIM:2 * HEAD_DIM] = jnp.ones((n_ctx + seq_len, HEAD_DIM), BF16)

    if band:
        w0 = pl.multiple_of(jnp.clip(n * ATT_TQ - A_WINDOW, 0, seq_len - ATT_WIN), A_WINDOW)
        slabs = [(0, n_ctx), (n_ctx + w0, ATT_WIN)]
        qpos = n * ATT_TQ + jax.lax.broadcasted_iota(jnp.int32, (ATT_TQ, ATT_WIN), 0)
        kpos = w0 + jax.lax.broadcasted_iota(jnp.int32, (ATT_TQ, ATT_WIN), 1)
        visible = jnp.abs(kpos - qpos) <= A_WINDOW
    else:
        slabs = [(0, n_ctx + seq_len)]
    scale2 = HEAD_DIM ** -0.5 * LOG2E
    for g in range(Q_PER_KV):
        q = q_ref[:, g * HEAD_DIM:(g + 1) * HEAD_DIM].astype(F32)
        if qk_norm:
            q = _rms(q, qg_ref[...])
        if rope:
            q = _rope(q, cosq_ref[...], sinq_ref[...])
        q = (q * scale2).astype(BF16)
        scores = []
        for idx, (k0, rows) in enumerate(slabs):
            s = jax.lax.dot_general(q, kall_ref[pl.ds(k0, rows), :], (((1,), (1,)), ((), ())),
                                    preferred_element_type=F32)
            if band and idx == 1:
                s = jnp.where(visible, s, -jnp.inf)
            scores.append(s)
        m = functools.reduce(jnp.maximum, [jnp.max(s, axis=-1, keepdims=True) for s in scores])
        if use_sink:
            sk2 = sink_ref[layer * Q_HEADS + j * Q_PER_KV + g] * LOG2E
            m = jnp.maximum(m, sk2)
        acc = None
        for s, (k0, rows) in zip(scores, slabs):
            p = jnp.exp2(s - m).astype(BF16)
            part = jnp.dot(p, vall_ref[pl.ds(k0, rows), :], preferred_element_type=F32)
            acc = part if acc is None else acc + part
        den = acc[:, HEAD_DIM:2 * HEAD_DIM]
        if use_sink:
            den = den + jnp.exp2(sk2 - m)
        o_ref[:, g * HEAD_DIM:(g + 1) * HEAD_DIM] = (acc[:, 0:HEAD_DIM] * (1.0 / den)).astype(BF16)


def attention(pr, layer, *, latent, mixer, ctx_k=None, ctx_v=None, sink=None, q_gain=None, k_gain=None,
              rope_tabs=None, kv_prev=None):
    n_seq, seq_len, row0 = (DEC_BATCH, DEC_SEQ, T_CTX) if latent else (BATCH, SEQ, 0)
    n_ctx = PAST_LEN if latent else 0
    base = 0 if mixer == "a" else R1_COLS
    use_sink = mixer == "a"
    qk_norm = mixer == "c"
    band = latent and mixer == "a"
    rope = latent
    emit_kv = not latent
    tq = ATT_TQ
    qblocks = seq_len // tq
    grid = (n_seq, KV_HEADS, qblocks)
    qw_kv = Q_PER_KV * HEAD_DIM

    in_specs = [
        pl.BlockSpec((tq, qw_kv), lambda b, j, n: (row0 // tq + b * qblocks + n, (base + OFF_Q) // qw_kv + j)),
        pl.BlockSpec((seq_len, HEAD_DIM), lambda b, j, n: (row0 // seq_len + b, (base + OFF_K) // HEAD_DIM + j)),
        pl.BlockSpec((seq_len, HEAD_DIM), lambda b, j, n: (row0 // seq_len + b, (base + OFF_V) // HEAD_DIM + j)),
    ]
    args = [pr, pr, pr]
    if n_ctx:
        cache_spec = pl.BlockSpec((None, None, PAST_LEN, HEAD_DIM), lambda b, j, n: (b, layer, 0, j))
        in_specs += [cache_spec, cache_spec]
        args += [ctx_k.reshape(DEC_BATCH, DEPTH, PAST_LEN, KVW), ctx_v.reshape(DEC_BATCH, DEPTH, PAST_LEN, KVW)]
    if use_sink:
        in_specs.append(pl.BlockSpec(memory_space=pltpu.SMEM))
        args.append(sink.reshape(DEPTH * Q_HEADS))
    if qk_norm:
        gain_spec = pl.BlockSpec((None, 1, HEAD_DIM), lambda b, j, n: (layer, 0, 0))
        in_specs += [gain_spec, gain_spec]
        args += [q_gain.reshape(DEPTH, 1, HEAD_DIM), k_gain.reshape(DEPTH, 1, HEAD_DIM)]
    if rope:
        cos2, sin2 = rope_tabs
        in_specs += [pl.BlockSpec((tq, HEAD_DIM), lambda b, j, n: (n, 0))] * 2
        in_specs += [pl.BlockSpec((seq_len, HEAD_DIM), lambda b, j, n: (0, 0))] * 2
        args += [cos2, sin2, cos2, sin2]
    if emit_kv and layer:
        prev_spec = pl.BlockSpec((None, layer, SEQ, HEAD_DIM), lambda b, j, n: (b, 0, 0, j))
        in_specs += [prev_spec, prev_spec]
        args += list(kv_prev)

    out_specs = [pl.BlockSpec((tq, qw_kv), lambda b, j, n: (b * qblocks + n, j))]
    out_shape = [jax.ShapeDtypeStruct((n_seq * seq_len, QW), BF16)]
    if emit_kv:
        kv_spec = pl.BlockSpec((None, layer + 1, SEQ, HEAD_DIM), lambda b, j, n: (b, 0, 0, j))
        out_specs += [kv_spec, kv_spec]
        out_shape += [jax.ShapeDtypeStruct((BATCH, layer + 1, SEQ, KVW), F32)] * 2

    return pl.pallas_call(
        functools.partial(_attn_body, layer=layer, n_ctx=n_ctx, seq_len=seq_len, use_sink=use_sink, band=band,
                          qk_norm=qk_norm, rope=rope, emit_kv=emit_kv),
        grid=grid,
        in_specs=in_specs,
        out_specs=out_specs,
        out_shape=out_shape,
        scratch_shapes=[pltpu.VMEM((n_ctx + seq_len, HEAD_DIM), BF16),
                        pltpu.VMEM((n_ctx + seq_len, 2 * HEAD_DIM), BF16)],
        compiler_params=_cparams(3),
        name=f"attn_{mixer}_{'lat' if latent else 'ctx'}",
    )(*args)


SSD_COLBLK = 512
N_Z_BLK = SSM_INNER // SSD_COLBLK
N_XBC_BLK = CONV_CH // SSD_COLBLK
HALO = 16
GROUP_W = SSM_INNER // SSM_GROUPS
PAIRS = SSM_HEADS // 2
PAIRS_PER_GROUP = PAIRS // SSM_GROUPS


def _softplus(x):
    return jnp.maximum(x, 0.0) + jnp.log1p(jnp.exp(-jnp.abs(x)))


def _split3(x):
    hi = x.astype(BF16)
    r = x - hi.astype(F32)
    mid = r.astype(BF16)
    lo = (r - mid.astype(F32)).astype(BF16)
    return hi, mid, lo


def _ssd_body(*refs, layer, seq_len, has_h0, emit_state):
    it = iter(refs)
    z_refs = [next(it) for _ in range(N_Z_BLK)]
    xbc_refs = [next(it) for _ in range(N_XBC_BLK)]
    dt_ref, convw_ref, convb_ref, alog_ref, dtb_ref, dexp_ref, ng_ref = (next(it) for _ in range(7))
    h0_ref = next(it) if has_h0 else None
    stprev_ref = next(it) if emit_state and layer else None
    y_ref = next(it)
    st_ref = next(it) if emit_state else None
    conv_scr, y_scr, h_scr = next(it), next(it), next(it)

    n_chunks = seq_len // SSD_CHUNK
    q = SSD_CHUNK
    lane = jax.lax.broadcasted_iota(jnp.int32, (q, LANE), 1)
    row = jax.lax.broadcasted_iota(jnp.int32, (q, LANE), 0)
    low_half = lane < HALF_LANE

    for d in range(2):
        for blk in range(SSM_INNER // LANE):
            cols = slice(blk * LANE, (blk + 1) * LANE)
            if has_h0:
                h_scr[d, :, cols] = h0_ref[d, cols, :].T
            else:
                h_scr[d, :, cols] = jnp.zeros((SSM_STATE, LANE), F32)

    def conv_chunk(c, carry):
        r0 = pl.multiple_of(c * q, q)
        prev0 = pl.multiple_of(jnp.maximum(r0 - HALO, 0), HALO)
        next0 = pl.multiple_of(jnp.minimum(r0 + q, seq_len - HALO), HALO)
        has_prev = (r0 > 0).astype(F32)
        has_next = (r0 + q < seq_len).astype(F32)
        for j in range(CONV_CH // LANE):
            src = xbc_refs[j // (SSD_COLBLK // LANE)]
            sc = slice((j % (SSD_COLBLK // LANE)) * LANE, (j % (SSD_COLBLK // LANE) + 1) * LANE)
            cols = slice(j * LANE, (j + 1) * LANE)
            u = src[pl.ds(r0, q), sc].astype(F32)
            prev_row = src[pl.ds(prev0, HALO), sc].astype(F32)[HALO - 1:HALO, :] * has_prev
            next_row = src[pl.ds(next0, HALO), sc].astype(F32)[0:1, :] * has_next
            up = jnp.where(row == 0, prev_row, pltpu.roll(u, 1, 0))
            dn = jnp.where(row == q - 1, next_row, pltpu.roll(u, q - 1, 0))
            v = (convw_ref[0:1, cols] * up + convw_ref[1:2, cols] * u + convw_ref[2:3, cols] * dn
                 + convb_ref[:, cols])
            act = _silu(v)
            conv_scr[pl.ds(r0, q), cols] = act.astype(BF16)
            if j < SSM_INNER // LANE:
                y_scr[pl.ds(r0, q), cols] = dexp_ref[:, cols] * act
        return carry

    jax.lax.fori_loop(0, n_chunks, conv_chunk, 0)

    def scan_chunk(c, d):
        r0 = pl.multiple_of(c * q, q)
        edge = q - 1 if d == 0 else 0
        vis = (row >= lane) if d == 0 else (row <= lane)
        tri = jnp.where(vis, 1.0, 0.0).astype(BF16)
        dt = _softplus(dt_ref[pl.ds(r0, q), :] + dtb_ref[...])
        a = dt * (-jnp.exp(alog_ref[...]))
        a_hi, a_mid, a_lo = _split3(a)
        acum = (jnp.dot(tri, a_hi, preferred_element_type=F32) + jnp.dot(tri, a_mid, preferred_element_type=F32)
                + jnp.dot(tri, a_lo, preferred_element_type=F32))
        acum2 = acum * LOG2E
        acum_t = acum.T
        dt_t = dt.T
        row_t2 = (acum_t - jnp.log(dt_t)) * LOG2E
        w_t = dt_t * jnp.exp(acum_t[:, edge:edge + 1] - acum_t)
        for g in range(SSM_GROUPS):
            b_g = conv_scr[pl.ds(r0, q), SSM_INNER + g * SSM_STATE:SSM_INNER + (g + 1) * SSM_STATE]
            c_lo = SSM_INNER + SSM_GROUPS * SSM_STATE + g * SSM_STATE
            c_g = conv_scr[pl.ds(r0, q), c_lo:c_lo + SSM_STATE]
            cb = jax.lax.dot_general(c_g, b_g, (((1,), (1,)), ((), ())),
                                     preferred_element_type=F32).astype(BF16)
            b_t = b_g.astype(F32).T.astype(BF16)
            h_g = h_scr[d, :, g * GROUP_W:(g + 1) * GROUP_W].astype(BF16)
            y_off = jnp.dot(c_g, h_g, preferred_element_type=F32)
            for pp in range(PAIRS_PER_GROUP):
                p = g * PAIRS_PER_GROUP + pp
                cols = slice(p * LANE, (p + 1) * LANE)
                lhs_top, lhs_bot, decay = [], [], []
                for h in (2 * p, 2 * p + 1):
                    ell = d * SSM_HEADS + h
                    col = jnp.broadcast_to(acum2[:, ell:ell + 1], (q, LANE))
                    seg = jnp.exp2(jnp.where(vis, col - row_t2[ell:ell + 1, :], -jnp.inf))
                    lhs_top.append(cb * seg.astype(BF16))
                    lhs_bot.append(b_t * w_t[ell:ell + 1, :].astype(BF16))
                    decay.append(jnp.exp2(col))
                lhs = jnp.concatenate([jnp.concatenate(lhs_top, axis=1), jnp.concatenate(lhs_bot, axis=1)],
                                      axis=0)
                xp = conv_scr[pl.ds(r0, q), cols]
                zero = jnp.zeros_like(xp)
                rhs = jnp.concatenate([jnp.where(low_half, xp, zero), jnp.where(low_half, zero, xp)], axis=0)
                res = jnp.dot(lhs, rhs, preferred_element_type=F32)
                factor = jnp.where(low_half, decay[0], decay[1])
                y_scr[pl.ds(r0, q), cols] += res[0:q, :] + y_off[:, pp * LANE:(pp + 1) * LANE] * factor
                h_scr[d, :, cols] = h_scr[d, :, cols] * factor[edge:edge + 1, :] + res[q:2 * q, :]

    def scan_step(i, carry):
        scan_chunk(i, 0)
        scan_chunk(n_chunks - 1 - i, 1)
        return carry

    jax.lax.fori_loop(0, n_chunks, scan_step, 0)

    if emit_state:
        if layer:
            st_ref[0:layer] = stprev_ref[...]
        for d in range(2):
            for blk in range(SSM_INNER // LANE):
                cols = slice(blk * LANE, (blk + 1) * LANE)
                st_ref[layer, d, cols, :] = h_scr[d, :, cols].T

    def finish_chunk(c, carry):
        r0 = pl.multiple_of(c * q, q)
        gated = []
        for k in range(N_Z_BLK):
            cols = slice(k * SSD_COLBLK, (k + 1) * SSD_COLBLK)
            gated.append(y_scr[pl.ds(r0, q), cols] * _silu(z_refs[k][pl.ds(r0, q), :].astype(F32)))
        ssq = sum(jnp.sum(gk * gk, axis=-1, keepdims=True) for gk in gated)
        inv = jax.lax.rsqrt(ssq * (1.0 / SSM_INNER) + EPS)
        for k in range(N_Z_BLK):
            cols = slice(k * SSD_COLBLK, (k + 1) * SSD_COLBLK)
            y_ref[pl.ds(r0, q), cols] = (gated[k] * inv * ng_ref[:, cols]).astype(BF16)
        return carry

    jax.lax.fori_loop(0, n_chunks, finish_chunk, 0)


def ssd_mixer(pr, dt_raw, layer, *, latent, conv_w, conv_b, a_log_row, dt_bias_row, d_row, norm_gain,
              h0=None, state_prev=None):
    n_seq, seq_len, row0 = (DEC_BATCH, DEC_SEQ, T_CTX) if latent else (BATCH, SEQ, 0)
    has_h0 = latent
    emit_state = not latent
    seq_blk = row0 // seq_len

    def col_spec(first, k):
        return pl.BlockSpec((seq_len, SSD_COLBLK), lambda b: (seq_blk + b, first // SSD_COLBLK + k))

    in_specs = [col_spec(OFF_Z, k) for k in range(N_Z_BLK)] + [col_spec(OFF_XBC, k) for k in range(N_XBC_BLK)]
    args = [pr] * (N_Z_BLK + N_XBC_BLK)
    in_specs += [
        pl.BlockSpec((seq_len, LANE), lambda b: (seq_blk + b, 0)),
        pl.BlockSpec((None, CONV_K, CONV_CH), lambda b: (layer, 0, 0)),
        pl.BlockSpec((None, 1, CONV_CH), lambda b: (layer, 0, 0)),
        pl.BlockSpec((None, 1, LANE), lambda b: (layer, 0, 0)),
        pl.BlockSpec((None, 1, LANE), lambda b: (layer, 0, 0)),
        pl.BlockSpec((None, 1, SSM_INNER), lambda b: (layer, 0, 0)),
        pl.BlockSpec((None, 1, SSM_INNER), lambda b: (layer, 0, 0)),
    ]
    args += [dt_raw, conv_w, conv_b.reshape(DEPTH, 1, CONV_CH), a_log_row, dt_bias_row, d_row,
             norm_gain.reshape(DEPTH, 1, SSM_INNER)]
    if has_h0:
        in_specs.append(pl.BlockSpec((None, None, 2, SSM_INNER, SSM_STATE), lambda b: (b, layer, 0, 0, 0)))
        args.append(h0.reshape(DEC_BATCH, DEPTH, 2, SSM_INNER, SSM_STATE))
    if emit_state and layer:
        in_specs.append(pl.BlockSpec((None, layer, 2, SSM_INNER, SSM_STATE), lambda b: (b, 0, 0, 0, 0)))
        args.append(state_prev)

    out_specs = [pl.BlockSpec((seq_len, SSM_INNER), lambda b: (b, 0))]
    out_shape = [jax.ShapeDtypeStruct((n_seq * seq_len, SSM_INNER), BF16)]
    if emit_state:
        out_specs.append(pl.BlockSpec((None, layer + 1, 2, SSM_INNER, SSM_STATE), lambda b: (b, 0, 0, 0, 0)))
        out_shape.append(jax.ShapeDtypeStruct((BATCH, layer + 1, 2, SSM_INNER, SSM_STATE), F32))

    return pl.pallas_call(
        functools.partial(_ssd_body, layer=layer, seq_len=seq_len, has_h0=has_h0, emit_state=emit_state),
        grid=(n_seq,),
        in_specs=in_specs,
        out_specs=out_specs,
        out_shape=out_shape,
        scratch_shapes=[
            pltpu.VMEM((seq_len, CONV_CH), BF16),
            pltpu.VMEM((seq_len, SSM_INNER), F32),
            pltpu.VMEM((2, SSM_STATE, SSM_INNER), F32),
        ],
        compiler_params=_cparams(1),
        name=f"ssd_{'lat' if latent else 'ctx'}",
    )(*args)


def _rope_tables():
    rows = DEC_SEQ // GRID_W
    r = jnp.repeat(jnp.arange(rows, dtype=F32), GRID_W)
    c = jnp.tile(jnp.arange(GRID_W, dtype=F32), rows)
    n_freq = HEAD_DIM // 4
    inv_freq = ROPE_THETA ** (-jnp.arange(n_freq, dtype=F32) / n_freq)
    ang = jnp.concatenate([r[:, None] * inv_freq, c[:, None] * inv_freq], axis=-1)
    cos, sin = jnp.cos(ang), jnp.sin(ang)
    return jnp.concatenate([cos, cos], axis=-1), jnp.concatenate([-sin, sin], axis=-1)


def kernel(x_prompt, x_sample, cache_a_k, cache_a_v, cache_c_k, cache_c_v, state_ssm, c, c_ctx, norm1_g, w_ada, b_ada, w_in, a_sink, conv_w, conv_b, ssm_a_log, ssm_dt_bias, ssm_d, ssm_norm_g, c_q_norm, c_k_norm, w_oa, w_ob, w_oc, w_out, norm2_g, w_mlp1, w_mlp2, final_norm_g):
    x = (x_prompt.reshape(T_CTX, D_MODEL), x_sample.reshape(T_LAT, D_MODEL))
    cond = jnp.concatenate([c, c_ctx[None, :], jnp.zeros((MOD_ROWS - DEC_BATCH - 1, D_MODEL), F32)], axis=0)
    mod = ada_modulation(cond, w_ada, b_ada).reshape(DEPTH, MOD_ROWS, N_MOD, 1, D_MODEL)
    rope_tabs = _rope_tables()
    lane_pad = jnp.zeros((DEPTH, LANE - 2 * SSM_HEADS), F32)
    a_log_row = jnp.concatenate([ssm_a_log.reshape(DEPTH, 2 * SSM_HEADS), lane_pad], axis=1).reshape(DEPTH, 1, LANE)
    dt_bias_row = jnp.concatenate([ssm_dt_bias.reshape(DEPTH, 2 * SSM_HEADS), lane_pad], axis=1).reshape(DEPTH, 1, LANE)
    d_row = jnp.repeat(ssm_d, SSM_HEAD_DIM, axis=1).reshape(DEPTH, 1, SSM_INNER)
    w_in_t = jnp.swapaxes(w_in, 1, 2)
    w_out_bf = w_out.astype(BF16)

    kv_a = kv_c = state = None
    for layer in range(DEPTH):
        h, dt_raw = prenorm(x, norm1_g, mod, layer, w_in_t)
        pr = in_projection(h, w_in_t, layer)

        ya_c, *kv_a = attention(pr, layer, latent=False, mixer="a", sink=a_sink, kv_prev=kv_a)
        (ya_l,) = attention(pr, layer, latent=True, mixer="a", sink=a_sink, ctx_k=cache_a_k, ctx_v=cache_a_v,
                            rope_tabs=rope_tabs)
        yc_c, *kv_c = attention(pr, layer, latent=False, mixer="c", q_gain=c_q_norm, k_gain=c_k_norm, kv_prev=kv_c)
        (yc_l,) = attention(pr, layer, latent=True, mixer="c", q_gain=c_q_norm, k_gain=c_k_norm, ctx_k=cache_c_k,
                            ctx_v=cache_c_v, rope_tabs=rope_tabs)
        ssd_args = dict(conv_w=conv_w, conv_b=conv_b, a_log_row=a_log_row, dt_bias_row=dt_bias_row, d_row=d_row,
                        norm_gain=ssm_norm_g)
        yb_c, state = ssd_mixer(pr, dt_raw, layer, latent=False, state_prev=state, **ssd_args)
        (yb_l,) = ssd_mixer(pr, dt_raw, layer, latent=True, h0=state_ssm, **ssd_args)

        merged = merge_branches((ya_c, ya_l), (yb_c, yb_l), (yc_c, yc_l), pr, w_oa, w_ob, w_oc, layer)
        x1, h2 = out_projection_norm(merged, x, w_out_bf, norm2_g, mod, layer)
        x = mlp_down(mlp_up(h2, w_mlp1, layer), x1, w_mlp2, mod, layer)

    y_prompt, y_sample = final_norm(x, final_norm_g)
    kv5 = (BATCH, DEPTH, SEQ, KV_HEADS, HEAD_DIM)
    return (y_prompt.reshape(BATCH, SEQ, D_MODEL), y_sample.reshape(DEC_BATCH, DEC_SEQ, D_MODEL),
            kv_a[0].reshape(kv5), kv_a[1].reshape(kv5), kv_c[0].reshape(kv5), kv_c[1].reshape(kv5),
            state.reshape(BATCH, DEPTH, 2, SSM_HEADS, SSM_HEAD_DIM, SSM_STATE))
```

```python
import functools
import math

import jax
import jax.numpy as jnp
from jax.experimental import pallas as pl
from jax.experimental.pallas import tpu as pltpu

F32 = jnp.float32
BF16 = jnp.bfloat16

D_MODEL = 2048
BATCH = 16
SEQ = 256
DEPTH = 2
DEC_BATCH = 4
DEC_SEQ = 1024
PAST_LEN = 256
GRID_W = 64
HEAD_DIM = 128
ROPE_THETA = 10000.0
EPS = 1e-6
Q_HEADS = 8
KV_HEADS = 2
Q_PER_KV = Q_HEADS // KV_HEADS
A_WINDOW = 128
SSM_HEADS = 32
SSM_HEAD_DIM = 64
SSM_GROUPS = 2
SSM_STATE = 128
CONV_K = 3
D_FF = 4 * D_MODEL
QW = Q_HEADS * HEAD_DIM
KVW = KV_HEADS * HEAD_DIM
SSM_INNER = SSM_HEADS * SSM_HEAD_DIM
CONV_CH = SSM_INNER + 2 * SSM_GROUPS * SSM_STATE
N_MOD = 6

T_CTX = BATCH * SEQ
T_LAT = DEC_BATCH * DEC_SEQ
T_ALL = T_CTX + T_LAT
MOD_ROWS = 8
CTX_MOD_ROW = DEC_BATCH

R1_COLS = QW + 2 * KVW + SSM_INNER + CONV_CH
DT_COLS = 2 * SSM_HEADS
R2_COLS = QW + 2 * KVW + 3 * D_MODEL
PR_COLS = R1_COLS + R2_COLS
OFF_Q, OFF_K, OFF_V = 0, QW, QW + KVW
OFF_Z = QW + 2 * KVW
OFF_XBC = OFF_Z + SSM_INNER
OFF_GATES = R1_COLS + QW + 2 * KVW

LANE = 128
HALF_LANE = LANE // 2
SSD_CHUNK = 128
VMEM_LIMIT = 56 * 1024 * 1024
LOG2E = math.log2(math.e)


def _cparams(n_axes, vmem=VMEM_LIMIT):
    return pltpu.CompilerParams(dimension_semantics=("arbitrary",) * n_axes, vmem_limit_bytes=vmem)


def _mod_row(tile, tm):
    n_ctx_tiles = T_CTX // tm
    return jnp.where(tile < n_ctx_tiles, CTX_MOD_ROW, (tile - n_ctx_tiles) // (DEC_SEQ // tm))


def _ctx_tile(tile, tm):
    return jnp.minimum(tile, T_CTX // tm - 1)


def _lat_tile(tile, tm):
    return jnp.maximum(tile - T_CTX // tm, 0)


def _sigmoid(x):
    return 0.5 * jnp.tanh(0.5 * x) + 0.5


def _silu(x):
    return x * _sigmoid(x)


def _rms(x, gain):
    return x * jax.lax.rsqrt(jnp.mean(x * x, axis=-1, keepdims=True) + EPS) * gain


def _ada_body(cond_ref, w_ref, b_ref, o_ref):
    a = _silu(cond_ref[...]).astype(BF16)
    o_ref[...] = jnp.dot(a, w_ref[...].astype(BF16), preferred_element_type=F32) + b_ref[...]


def ada_modulation(cond, w_ada, b_ada):
    tn = 1024
    n_out = N_MOD * D_MODEL
    return pl.pallas_call(
        _ada_body,
        grid=(DEPTH, n_out // tn),
        in_specs=[
            pl.BlockSpec((MOD_ROWS, D_MODEL), lambda l, n: (0, 0)),
            pl.BlockSpec((None, D_MODEL, tn), lambda l, n: (l, 0, n)),
            pl.BlockSpec((None, 1, tn), lambda l, n: (l, 0, n)),
        ],
        out_specs=pl.BlockSpec((None, MOD_ROWS, tn), lambda l, n: (l, 0, n)),
        out_shape=jax.ShapeDtypeStruct((DEPTH, MOD_ROWS, n_out), F32),
        compiler_params=_cparams(2),
        name="ada_modulation",
    )(cond, w_ada, b_ada.reshape(DEPTH, 1, n_out))


PRENORM_TM = 512


def _prenorm_body(*refs, split_x):
    if split_x:
        xc_ref, xl_ref, g_ref, sh_ref, sc_ref, wdt_ref, h_ref, dt_ref = refs
        x = jnp.where(pl.program_id(0) < T_CTX // PRENORM_TM, xc_ref[...], xl_ref[...])
    else:
        x_ref, g_ref, sh_ref, sc_ref, wdt_ref, h_ref, dt_ref = refs
        x = x_ref[...]
    h = (_rms(x, g_ref[...]) * (1.0 + sc_ref[...]) + sh_ref[...]).astype(BF16)
    dt_ref[...] = jax.lax.dot_general(h, wdt_ref[...].astype(BF16), (((1,), (1,)), ((), ())),
                                      preferred_element_type=F32)
    h_ref[...] = h


def prenorm(x, gain, mod, layer, w_in_t):
    tm = PRENORM_TM
    split_x = isinstance(x, tuple)
    mod_spec = lambda k: pl.BlockSpec((None, None, None, 1, D_MODEL),
                                      lambda i: (layer, _mod_row(i, tm), k, 0, 0))
    if split_x:
        x_specs = [pl.BlockSpec((tm, D_MODEL), lambda i: (_ctx_tile(i, tm), 0)),
                   pl.BlockSpec((tm, D_MODEL), lambda i: (_lat_tile(i, tm), 0))]
        x_args = list(x)
    else:
        x_specs = [pl.BlockSpec((tm, D_MODEL), lambda i: (i, 0))]
        x_args = [x]
    return pl.pallas_call(
        functools.partial(_prenorm_body, split_x=split_x),
        grid=(T_ALL // tm,),
        in_specs=x_specs + [
            pl.BlockSpec((None, 1, D_MODEL), lambda i: (layer, 0, 0)),
            mod_spec(0),
            mod_spec(1),
            pl.BlockSpec((None, LANE, D_MODEL), lambda i: (layer, R1_COLS // LANE, 0)),
        ],
        out_specs=[pl.BlockSpec((tm, D_MODEL), lambda i: (i, 0)), pl.BlockSpec((tm, LANE), lambda i: (i, 0))],
        out_shape=[jax.ShapeDtypeStruct((T_ALL, D_MODEL), BF16), jax.ShapeDtypeStruct((T_ALL, LANE), F32)],
        compiler_params=_cparams(1),
        name="prenorm_dt",
    )(*x_args, gain.reshape(DEPTH, 1, D_MODEL), mod, mod, w_in_t)


PROJ_TN = 1536
assert OFF_GATES % PROJ_TN == 0 and R1_COLS % PROJ_TN == 0
CAST_ROWS = 2048


def _cast_weight(w_ref, wbf_ref):
    for r in range(0, w_ref.shape[0], CAST_ROWS):
        rows = slice(r, min(r + CAST_ROWS, w_ref.shape[0]))
        wbf_ref[rows, :] = w_ref[rows, :].astype(BF16)


def _proj_body(h_ref, w_ref, o_ref, wbf_ref):
    @pl.when(pl.program_id(1) == 0)
    def _():
        _cast_weight(w_ref, wbf_ref)

    def project():
        return jax.lax.dot_general(h_ref[...], wbf_ref[...], (((1,), (1,)), ((), ())), preferred_element_type=F32)

    @pl.when(pl.program_id(0) < OFF_GATES // PROJ_TN)
    def _():
        o_ref[...] = project().astype(o_ref.dtype)

    @pl.when(pl.program_id(0) >= OFF_GATES // PROJ_TN)
    def _():
        o_ref[...] = _sigmoid(project()).astype(o_ref.dtype)


def in_projection(h, w_in_t, layer):
    tm, tn = 1024, PROJ_TN
    first_row = lambda n: pl.multiple_of(n * tn + jnp.where(n >= R1_COLS // tn, DT_COLS, 0), DT_COLS)
    return pl.pallas_call(
        _proj_body,
        grid=(PR_COLS // tn, T_ALL // tm),
        in_specs=[
            pl.BlockSpec((tm, D_MODEL), lambda n, m: (m, 0)),
            pl.BlockSpec((None, pl.Element(tn), pl.Element(D_MODEL)), lambda n, m: (layer, first_row(n), 0)),
        ],
        out_specs=pl.BlockSpec((tm, tn), lambda n, m: (m, n)),
        out_shape=jax.ShapeDtypeStruct((T_ALL, PR_COLS), BF16),
        scratch_shapes=[pltpu.VMEM((tn, D_MODEL), BF16)],
        compiler_params=_cparams(2),
        name="in_projection",
    )(h, w_in_t)


MERGE_TM = 512
MERGE_TN = 512
MERGE_K = QW + SSM_INNER + QW


def _merge_body(yac_ref, yal_ref, ybc_ref, ybl_ref, ycc_ref, ycl_ref, ga_ref, gb_ref, gc_ref,
                wa_ref, wb_ref, wc_ref, o_ref, wbf_ref):
    m = pl.program_id(1)

    @pl.when(m == 0)
    def _():
        wbf_ref[0:QW, :] = wa_ref[...].astype(BF16)
        wbf_ref[QW:QW + SSM_INNER, :] = wb_ref[...].astype(BF16)
        wbf_ref[QW + SSM_INNER:MERGE_K, :] = wc_ref[...].astype(BF16)

    def merge(ya_ref, yb_ref, yc_ref):
        br_a = jnp.dot(ya_ref[...], wbf_ref[0:QW, :], preferred_element_type=F32)
        br_b = jnp.dot(yb_ref[...], wbf_ref[QW:QW + SSM_INNER, :], preferred_element_type=F32)
        br_c = jnp.dot(yc_ref[...], wbf_ref[QW + SSM_INNER:MERGE_K, :], preferred_element_type=F32)
        merged = (ga_ref[...].astype(F32) * br_a + gb_ref[...].astype(F32) * br_b
                  + gc_ref[...].astype(F32) * br_c)
        o_ref[...] = merged.astype(BF16)

    @pl.when(m < T_CTX // MERGE_TM)
    def _():
        merge(yac_ref, ybc_ref, ycc_ref)

    @pl.when(m >= T_CTX // MERGE_TM)
    def _():
        merge(yal_ref, ybl_ref, ycl_ref)


def merge_branches(ya, yb, yc, pr, w_oa, w_ob, w_oc, layer):
    tm, tn = MERGE_TM, MERGE_TN
    gate_spec = lambda k: pl.BlockSpec((tm, tn), lambda n, m: (m, (OFF_GATES + k * D_MODEL) // tn + n))
    w_spec = lambda rows: pl.BlockSpec((None, rows, tn), lambda n, m: (layer, 0, n))
    pair_specs = lambda cols: [pl.BlockSpec((tm, cols), lambda n, m: (_ctx_tile(m, tm), 0)),
                               pl.BlockSpec((tm, cols), lambda n, m: (_lat_tile(m, tm), 0))]
    return pl.pallas_call(
        _merge_body,
        grid=(D_MODEL // tn, T_ALL // tm),
        in_specs=pair_specs(QW) + pair_specs(SSM_INNER) + pair_specs(QW) + [
            gate_spec(0), gate_spec(1), gate_spec(2),
            w_spec(QW), w_spec(SSM_INNER), w_spec(QW),
        ],
        out_specs=pl.BlockSpec((tm, tn), lambda n, m: (m, n)),
        out_shape=jax.ShapeDtypeStruct((T_ALL, D_MODEL), BF16),
        scratch_shapes=[pltpu.VMEM((MERGE_K, tn), BF16)],
        compiler_params=_cparams(2),
        name="merge_branches",
    )(*ya, *yb, *yc, pr, pr, pr, w_oa, w_ob, w_oc)


OUTPROJ_TM = 512


def _outproj_body(*refs, split_x):
    if split_x:
        a_ref, w_ref, xc_ref, xl_ref, g1_ref, ng_ref, sh_ref, sc_ref, x1_ref, h2_ref = refs
        x = jnp.where(pl.program_id(0) < T_CTX // OUTPROJ_TM, xc_ref[...], xl_ref[...])
    else:
        a_ref, w_ref, x_ref, g1_ref, ng_ref, sh_ref, sc_ref, x1_ref, h2_ref = refs
        x = x_ref[...]
    x1 = x + g1_ref[...] * jnp.dot(a_ref[...], w_ref[...], preferred_element_type=F32)
    x1_ref[...] = x1
    h2_ref[...] = (_rms(x1, ng_ref[...]) * (1.0 + sc_ref[...]) + sh_ref[...]).astype(BF16)


def out_projection_norm(merged, x, w_out_bf, gain2, mod, layer):
    tm = OUTPROJ_TM
    split_x = isinstance(x, tuple)
    mod_spec = lambda k: pl.BlockSpec((None, None, None, 1, D_MODEL), lambda i: (layer, _mod_row(i, tm), k, 0, 0))
    if split_x:
        x_specs = [pl.BlockSpec((tm, D_MODEL), lambda i: (_ctx_tile(i, tm), 0)),
                   pl.BlockSpec((tm, D_MODEL), lambda i: (_lat_tile(i, tm), 0))]
        x_args = list(x)
    else:
        x_specs = [pl.BlockSpec((tm, D_MODEL), lambda i: (i, 0))]
        x_args = [x]
    return pl.pallas_call(
        functools.partial(_outproj_body, split_x=split_x),
        grid=(T_ALL // tm,),
        in_specs=[
            pl.BlockSpec((tm, D_MODEL), lambda i: (i, 0)),
            pl.BlockSpec((None, D_MODEL, D_MODEL), lambda i: (layer, 0, 0), pipeline_mode=pl.Buffered(1)),
        ] + x_specs + [
            mod_spec(2),
            pl.BlockSpec((None, 1, D_MODEL), lambda i: (layer, 0, 0)),
            mod_spec(3),
            mod_spec(4),
        ],
        out_specs=[pl.BlockSpec((tm, D_MODEL), lambda i: (i, 0)), pl.BlockSpec((tm, D_MODEL), lambda i: (i, 0))],
        out_shape=[jax.ShapeDtypeStruct((T_ALL, D_MODEL), F32), jax.ShapeDtypeStruct((T_ALL, D_MODEL), BF16)],
        compiler_params=_cparams(1),
        name="out_projection_norm",
    )(merged, w_out_bf, *x_args, mod, gain2.reshape(DEPTH, 1, D_MODEL), mod, mod)


def _mlp_up_body(h_ref, w_ref, o_ref, wbf_ref):
    @pl.when(pl.program_id(1) == 0)
    def _():
        _cast_weight(w_ref, wbf_ref)

    hid = jnp.dot(h_ref[...], wbf_ref[...], preferred_element_type=F32)
    o_ref[...] = jnp.square(jnp.maximum(hid, 0.0)).astype(BF16)


def mlp_up(h2, w_mlp1, layer):
    tm, tn = 1024, 1024
    return pl.pallas_call(
        _mlp_up_body,
        grid=(D_FF // tn, T_ALL // tm),
        in_specs=[
            pl.BlockSpec((tm, D_MODEL), lambda n, m: (m, 0)),
            pl.BlockSpec((None, D_MODEL, tn), lambda n, m: (layer, 0, n)),
        ],
        out_specs=pl.BlockSpec((tm, tn), lambda n, m: (m, n)),
        out_shape=jax.ShapeDtypeStruct((T_ALL, D_FF), BF16),
        scratch_shapes=[pltpu.VMEM((D_MODEL, tn), BF16)],
        compiler_params=_cparams(2),
        name="mlp_up",
    )(h2, w_mlp1)


def _mlp_down_body(a_ref, w_ref, x_ref, g_ref, o_ref, wbf_ref):
    @pl.when(pl.program_id(1) == 0)
    def _():
        _cast_weight(w_ref, wbf_ref)

    o_ref[...] = x_ref[...] + g_ref[...] * jnp.dot(a_ref[...], wbf_ref[...], preferred_element_type=F32)


def mlp_down(hid, x1, w_mlp2, mod, layer):
    tm, tn = 512, 512
    return pl.pallas_call(
        _mlp_down_body,
        grid=(D_MODEL // tn, T_ALL // tm),
        in_specs=[
            pl.BlockSpec((tm, D_FF), lambda n, m: (m, 0)),
            pl.BlockSpec((None, D_FF, tn), lambda n, m: (layer, 0, n), pipeline_mode=pl.Buffered(1)),
            pl.BlockSpec((tm, tn), lambda n, m: (m, n)),
            pl.BlockSpec((None, None, None, 1, tn), lambda n, m: (layer, _mod_row(m, tm), 5, 0, n)),
        ],
        out_specs=pl.BlockSpec((tm, tn), lambda n, m: (m, n)),
        out_shape=jax.ShapeDtypeStruct((T_ALL, D_MODEL), F32),
        scratch_shapes=[pltpu.VMEM((D_FF, tn), BF16)],
        compiler_params=_cparams(2),
        name="mlp_down",
    )(hid, w_mlp2, x1, mod)


FINAL_TM = 512


def _final_norm_body(x_ref, g_ref, yp_ref, ys_ref):
    y = _rms(x_ref[...], g_ref[...])
    i = pl.program_id(0)

    @pl.when(i < T_CTX // FINAL_TM)
    def _():
        yp_ref[...] = y

    @pl.when(i >= T_CTX // FINAL_TM)
    def _():
        ys_ref[...] = y


def final_norm(x, gain):
    tm = FINAL_TM
    return pl.pallas_call(
        _final_norm_body,
        grid=(T_ALL // tm,),
        in_specs=[
            pl.BlockSpec((tm, D_MODEL), lambda i: (i, 0)),
            pl.BlockSpec((1, D_MODEL), lambda i: (0, 0)),
        ],
        out_specs=[
            pl.BlockSpec((tm, D_MODEL), lambda i: (_ctx_tile(i, tm), 0)),
            pl.BlockSpec((tm, D_MODEL), lambda i: (_lat_tile(i, tm), 0)),
        ],
        out_shape=[jax.ShapeDtypeStruct((T_CTX, D_MODEL), F32), jax.ShapeDtypeStruct((T_LAT, D_MODEL), F32)],
        compiler_params=_cparams(1),
        name="final_norm",
    )(x, gain.reshape(1, D_MODEL))


ATT_TQ = 256
ATT_WIN = ATT_TQ + 2 * A_WINDOW


def _rope(x, cos2, sin2):
    return x * cos2 + pltpu.roll(x, HALF_LANE, 1) * sin2


def _attn_body(*refs, layer, n_ctx, seq_len, use_sink, band, qk_norm, rope, emit_kv):
    it = iter(refs)
    q_ref, k_ref, v_ref = next(it), next(it), next(it)
    kctx_ref = vctx_ref = sink_ref = qg_ref = kg_ref = cosq_ref = sinq_ref = cosk_ref = sink_k_ref = None
    kprev_ref = vprev_ref = kout_ref = vout_ref = None
    if n_ctx:
        kctx_ref, vctx_ref = next(it), next(it)
    if use_sink:
        sink_ref = next(it)
    if qk_norm:
        qg_ref, kg_ref = next(it), next(it)
    if rope:
        cosq_ref, sinq_ref, cosk_ref, sink_k_ref = next(it), next(it), next(it), next(it)
    if emit_kv and layer:
        kprev_ref, vprev_ref = next(it), next(it)
    o_ref = next(it)
    if emit_kv:
        kout_ref, vout_ref = next(it), next(it)
    kall_ref, vall_ref = next(it), next(it)

    j = pl.program_id(1)
    n = pl.program_id(2)

    @pl.when(n == 0)
    def _():
        k = k_ref[...].astype(F32)
        if qk_norm:
            k = _rms(k, kg_ref[...])
        if emit_kv:
            if layer:
                kout_ref[0:layer] = kprev_ref[...]
                vout_ref[0:layer] = vprev_ref[...]
            kout_ref[layer] = k
            vout_ref[layer] = v_ref[...].astype(F32)
        if rope:
            k = _rope(k, cosk_ref[...], sink_k_ref[...])
        if n_ctx:
            kall_ref[0:n_ctx, :] = kctx_ref[...].astype(BF16)
            vall_ref[0:n_ctx, 0:HEAD_DIM] = vctx_ref[...].astype(BF16)
        kall_ref[n_ctx:n_ctx + seq_len, :] = k.astype(BF16)
        vall_ref[n_ctx:n_ctx + seq_len, 0:HEAD_DIM] = v_ref[...]
        vall_ref[:, HEAD_DIM:2 * HEAD_DIM] = jnp.ones((n_ctx + seq_len, HEAD_DIM), BF16)

    if band:
        w0 = pl.multiple_of(jnp.clip(n * ATT_TQ - A_WINDOW, 0, seq_len - ATT_WIN), A_WINDOW)
        slabs = [(0, n_ctx), (n_ctx + w0, ATT_WIN)]
        qpos = n * ATT_TQ + jax.lax.broadcasted_iota(jnp.int32, (ATT_TQ, ATT_WIN), 0)
        kpos = w0 + jax.lax.broadcasted_iota(jnp.int32, (ATT_TQ, ATT_WIN), 1)
        visible = jnp.abs(kpos - qpos) <= A_WINDOW
    else:
        slabs = [(0, n_ctx + seq_len)]
    scale2 = HEAD_DIM ** -0.5 * LOG2E
    for g in range(Q_PER_KV):
        q = q_ref[:, g * HEAD_DIM:(g + 1) * HEAD_DIM].astype(F32)
        if qk_norm:
            q = _rms(q, qg_ref[...])
        if rope:
            q = _rope(q, cosq_ref[...], sinq_ref[...])
        q = (q * scale2).astype(BF16)
        scores = []
        for idx, (k0, rows) in enumerate(slabs):
            s = jax.lax.dot_general(q, kall_ref[pl.ds(k0, rows), :], (((1,), (1,)), ((), ())),
                                    preferred_element_type=F32)
            if band and idx == 1:
                s = jnp.where(visible, s, -jnp.inf)
            scores.append(s)
        m = functools.reduce(jnp.maximum, [jnp.max(s, axis=-1, keepdims=True) for s in scores])
        if use_sink:
            sk2 = sink_ref[layer * Q_HEADS + j * Q_PER_KV + g] * LOG2E
            m = jnp.maximum(m, sk2)
        acc = None
        for s, (k0, rows) in zip(scores, slabs):
            p = jnp.exp2(s - m).astype(BF16)
            part = jnp.dot(p, vall_ref[pl.ds(k0, rows), :], preferred_element_type=F32)
            acc = part if acc is None else acc + part
        den = acc[:, HEAD_DIM:2 * HEAD_DIM]
        if use_sink:
            den = den + jnp.exp2(sk2 - m)
        o_ref[:, g * HEAD_DIM:(g + 1) * HEAD_DIM] = (acc[:, 0:HEAD_DIM] * (1.0 / den)).astype(BF16)


def attention(pr, layer, *, latent, mixer, ctx_k=None, ctx_v=None, sink=None, q_gain=None, k_gain=None,
              rope_tabs=None, kv_prev=None):
    n_seq, seq_len, row0 = (DEC_BATCH, DEC_SEQ, T_CTX) if latent else (BATCH, SEQ, 0)
    n_ctx = PAST_LEN if latent else 0
    base = 0 if mixer == "a" else R1_COLS
    use_sink = mixer == "a"
    qk_norm = mixer == "c"
    band = latent and mixer == "a"
    rope = latent
    emit_kv = not latent
    tq = ATT_TQ
    qblocks = seq_len // tq
    grid = (n_seq, KV_HEADS, qblocks)
    qw_kv = Q_PER_KV * HEAD_DIM

    in_specs = [
        pl.BlockSpec((tq, qw_kv), lambda b, j, n: (row0 // tq + b * qblocks + n, (base + OFF_Q) // qw_kv + j)),
        pl.BlockSpec((seq_len, HEAD_DIM), lambda b, j, n: (row0 // seq_len + b, (base + OFF_K) // HEAD_DIM + j)),
        pl.BlockSpec((seq_len, HEAD_DIM), lambda b, j, n: (row0 // seq_len + b, (base + OFF_V) // HEAD_DIM + j)),
    ]
    args = [pr, pr, pr]
    if n_ctx:
        cache_spec = pl.BlockSpec((None, None, PAST_LEN, HEAD_DIM), lambda b, j, n: (b, layer, 0, j))
        in_specs += [cache_spec, cache_spec]
        args += [ctx_k.reshape(DEC_BATCH, DEPTH, PAST_LEN, KVW), ctx_v.reshape(DEC_BATCH, DEPTH, PAST_LEN, KVW)]
    if use_sink:
        in_specs.append(pl.BlockSpec(memory_space=pltpu.SMEM))
        args.append(sink.reshape(DEPTH * Q_HEADS))
    if qk_norm:
        gain_spec = pl.BlockSpec((None, 1, HEAD_DIM), lambda b, j, n: (layer, 0, 0))
        in_specs += [gain_spec, gain_spec]
        args += [q_gain.reshape(DEPTH, 1, HEAD_DIM), k_gain.reshape(DEPTH, 1, HEAD_DIM)]
    if rope:
        cos2, sin2 = rope_tabs
        in_specs += [pl.BlockSpec((tq, HEAD_DIM), lambda b, j, n: (n, 0))] * 2
        in_specs += [pl.BlockSpec((seq_len, HEAD_DIM), lambda b, j, n: (0, 0))] * 2
        args += [cos2, sin2, cos2, sin2]
    if emit_kv and layer:
        prev_spec = pl.BlockSpec((None, layer, SEQ, HEAD_DIM), lambda b, j, n: (b, 0, 0, j))
        in_specs += [prev_spec, prev_spec]
        args += list(kv_prev)

    out_specs = [pl.BlockSpec((tq, qw_kv), lambda b, j, n: (b * qblocks + n, j))]
    out_shape = [jax.ShapeDtypeStruct((n_seq * seq_len, QW), BF16)]
    if emit_kv:
        kv_spec = pl.BlockSpec((None, layer + 1, SEQ, HEAD_DIM), lambda b, j, n: (b, 0, 0, j))
        out_specs += [kv_spec, kv_spec]
        out_shape += [jax.ShapeDtypeStruct((BATCH, layer + 1, SEQ, KVW), F32)] * 2

    return pl.pallas_call(
        functools.partial(_attn_body, layer=layer, n_ctx=n_ctx, seq_len=seq_len, use_sink=use_sink, band=band,
                          qk_norm=qk_norm, rope=rope, emit_kv=emit_kv),
        grid=grid,
        in_specs=in_specs,
        out_specs=out_specs,
        out_shape=out_shape,
        scratch_shapes=[pltpu.VMEM((n_ctx + seq_len, HEAD_DIM), BF16),
                        pltpu.VMEM((n_ctx + seq_len, 2 * HEAD_DIM), BF16)],
        compiler_params=_cparams(3),
        name=f"attn_{mixer}_{'lat' if latent else 'ctx'}",
    )(*args)


SSD_COLBLK = 512
N_Z_BLK = SSM_INNER // SSD_COLBLK
N_XBC_BLK = CONV_CH // SSD_COLBLK
HALO = 16
GROUP_W = SSM_INNER // SSM_GROUPS
PAIRS = SSM_HEADS // 2
PAIRS_PER_GROUP = PAIRS // SSM_GROUPS


def _softplus(x):
    return jnp.maximum(x, 0.0) + jnp.log1p(jnp.exp(-jnp.abs(x)))


def _split3(x):
    hi = x.astype(BF16)
    r = x - hi.astype(F32)
    mid = r.astype(BF16)
    lo = (r - mid.astype(F32)).astype(BF16)
    return hi, mid, lo


def _ssd_body(*refs, layer, seq_len, has_h0, emit_state):
    it = iter(refs)
    z_refs = [next(it) for _ in range(N_Z_BLK)]
    xbc_refs = [next(it) for _ in range(N_XBC_BLK)]
    dt_ref, convw_ref, convb_ref, alog_ref, dtb_ref, dexp_ref, ng_ref = (next(it) for _ in range(7))
    h0_ref = next(it) if has_h0 else None
    stprev_ref = next(it) if emit_state and layer else None
    y_ref = next(it)
    st_ref = next(it) if emit_state else None
    conv_scr, y_scr, h_scr = next(it), next(it), next(it)

    n_chunks = seq_len // SSD_CHUNK
    q = SSD_CHUNK
    lane = jax.lax.broadcasted_iota(jnp.int32, (q, LANE), 1)
    row = jax.lax.broadcasted_iota(jnp.int32, (q, LANE), 0)
    low_half = lane < HALF_LANE

    for d in range(2):
        for blk in range(SSM_INNER // LANE):
            cols = slice(blk * LANE, (blk + 1) * LANE)
            if has_h0:
                h_scr[d, :, cols] = h0_ref[d, cols, :].T
            else:
                h_scr[d, :, cols] = jnp.zeros((SSM_STATE, LANE), F32)

    def conv_chunk(c, carry):
        r0 = pl.multiple_of(c * q, q)
        prev0 = pl.multiple_of(jnp.maximum(r0 - HALO, 0), HALO)
        next0 = pl.multiple_of(jnp.minimum(r0 + q, seq_len - HALO), HALO)
        has_prev = (r0 > 0).astype(F32)
        has_next = (r0 + q < seq_len).astype(F32)
        for j in range(CONV_CH // LANE):
            src = xbc_refs[j // (SSD_COLBLK // LANE)]
            sc = slice((j % (SSD_COLBLK // LANE)) * LANE, (j % (SSD_COLBLK // LANE) + 1) * LANE)
            cols = slice(j * LANE, (j + 1) * LANE)
            u = src[pl.ds(r0, q), sc].astype(F32)
            prev_row = src[pl.ds(prev0, HALO), sc].astype(F32)[HALO - 1:HALO, :] * has_prev
            next_row = src[pl.ds(next0, HALO), sc].astype(F32)[0:1, :] * has_next
            up = jnp.where(row == 0, prev_row, pltpu.roll(u, 1, 0))
            dn = jnp.where(row == q - 1, next_row, pltpu.roll(u, q - 1, 0))
            v = (convw_ref[0:1, cols] * up + convw_ref[1:2, cols] * u + convw_ref[2:3, cols] * dn
                 + convb_ref[:, cols])
            act = _silu(v)
            conv_scr[pl.ds(r0, q), cols] = act.astype(BF16)
            if j < SSM_INNER // LANE:
                y_scr[pl.ds(r0, q), cols] = dexp_ref[:, cols] * act
        return carry

    jax.lax.fori_loop(0, n_chunks, conv_chunk, 0)

    def scan_chunk(c, d):
        r0 = pl.multiple_of(c * q, q)
        edge = q - 1 if d == 0 else 0
        vis = (row >= lane) if d == 0 else (row <= lane)
        tri = jnp.where(vis, 1.0, 0.0).astype(BF16)
        dt = _softplus(dt_ref[pl.ds(r0, q), :] + dtb_ref[...])
        a = dt * (-jnp.exp(alog_ref[...]))
        a_hi, a_mid, a_lo = _split3(a)
        acum = (jnp.dot(tri, a_hi, preferred_element_type=F32) + jnp.dot(tri, a_mid, preferred_element_type=F32)
                + jnp.dot(tri, a_lo, preferred_element_type=F32))
        acum2 = acum * LOG2E
        acum_t = acum.T
        dt_t = dt.T
        row_t2 = (acum_t - jnp.log(dt_t)) * LOG2E
        w_t = dt_t * jnp.exp(acum_t[:, edge:edge + 1] - acum_t)
        for g in range(SSM_GROUPS):
            b_g = conv_scr[pl.ds(r0, q), SSM_INNER + g * SSM_STATE:SSM_INNER + (g + 1) * SSM_STATE]
            c_lo = SSM_INNER + SSM_GROUPS * SSM_STATE + g * SSM_STATE
            c_g = conv_scr[pl.ds(r0, q), c_lo:c_lo + SSM_STATE]
            cb = jax.lax.dot_general(c_g, b_g, (((1,), (1,)), ((), ())),
                                     preferred_element_type=F32).astype(BF16)
            b_t = b_g.astype(F32).T.astype(BF16)
            h_g = h_scr[d, :, g * GROUP_W:(g + 1) * GROUP_W].astype(BF16)
            y_off = jnp.dot(c_g, h_g, preferred_element_type=F32)
            for pp in range(PAIRS_PER_GROUP):
                p = g * PAIRS_PER_GROUP + pp
                cols = slice(p * LANE, (p + 1) * LANE)
                lhs_top, lhs_bot, decay = [], [], []
                for h in (2 * p, 2 * p + 1):
                    ell = d * SSM_HEADS + h
                    col = jnp.broadcast_to(acum2[:, ell:ell + 1], (q, LANE))
                    seg = jnp.exp2(jnp.where(vis, col - row_t2[ell:ell + 1, :], -jnp.inf))
                    lhs_top.append(cb * seg.astype(BF16))
                    lhs_bot.append(b_t * w_t[ell:ell + 1, :].astype(BF16))
                    decay.append(jnp.exp2(col))
                lhs = jnp.concatenate([jnp.concatenate(lhs_top, axis=1), jnp.concatenate(lhs_bot, axis=1)],
                                      axis=0)
                xp = conv_scr[pl.ds(r0, q), cols]
                zero = jnp.zeros_like(xp)
                rhs = jnp.concatenate([jnp.where(low_half, xp, zero), jnp.where(low_half, zero, xp)], axis=0)
                res = jnp.dot(lhs, rhs, preferred_element_type=F32)
                factor = jnp.where(low_half, decay[0], decay[1])
                y_scr[pl.ds(r0, q), cols] += res[0:q, :] + y_off[:, pp * LANE:(pp + 1) * LANE] * factor
                h_scr[d, :, cols] = h_scr[d, :, cols] * factor[edge:edge + 1, :] + res[q:2 * q, :]

    def scan_step(i, carry):
        scan_chunk(i, 0)
        scan_chunk(n_chunks - 1 - i, 1)
        return carry

    jax.lax.fori_loop(0, n_chunks, scan_step, 0)

    if emit_state:
        if layer:
            st_ref[0:layer] = stprev_ref[...]
        for d in range(2):
            for blk in range(SSM_INNER // LANE):
                cols = slice(blk * LANE, (blk + 1) * LANE)
                st_ref[layer, d, cols, :] = h_scr[d, :, cols].T

    def finish_chunk(c, carry):
        r0 = pl.multiple_of(c * q, q)
        gated = []
        for k in range(N_Z_BLK):
            cols = slice(k * SSD_COLBLK, (k + 1) * SSD_COLBLK)
            gated.append(y_scr[pl.ds(r0, q), cols] * _silu(z_refs[k][pl.ds(r0, q), :].astype(F32)))
        ssq = sum(jnp.sum(gk * gk, axis=-1, keepdims=True) for gk in gated)
        inv = jax.lax.rsqrt(ssq * (1.0 / SSM_INNER) + EPS)
        for k in range(N_Z_BLK):
            cols = slice(k * SSD_COLBLK, (k + 1) * SSD_COLBLK)
            y_ref[pl.ds(r0, q), cols] = (gated[k] * inv * ng_ref[:, cols]).astype(BF16)
        return carry

    jax.lax.fori_loop(0, n_chunks, finish_chunk, 0)


def ssd_mixer(pr, dt_raw, layer, *, latent, conv_w, conv_b, a_log_row, dt_bias_row, d_row, norm_gain,
              h0=None, state_prev=None):
    n_seq, seq_len, row0 = (DEC_BATCH, DEC_SEQ, T_CTX) if latent else (BATCH, SEQ, 0)
    has_h0 = latent
    emit_state = not latent
    seq_blk = row0 // seq_len

    def col_spec(first, k):
        return pl.BlockSpec((seq_len, SSD_COLBLK), lambda b: (seq_blk + b, first // SSD_COLBLK + k))

    in_specs = [col_spec(OFF_Z, k) for k in range(N_Z_BLK)] + [col_spec(OFF_XBC, k) for k in range(N_XBC_BLK)]
    args = [pr] * (N_Z_BLK + N_XBC_BLK)
    in_specs += [
        pl.BlockSpec((seq_len, LANE), lambda b: (seq_blk + b, 0)),
        pl.BlockSpec((None, CONV_K, CONV_CH), lambda b: (layer, 0, 0)),
        pl.BlockSpec((None, 1, CONV_CH), lambda b: (layer, 0, 0)),
        pl.BlockSpec((None, 1, LANE), lambda b: (layer, 0, 0)),
        pl.BlockSpec((None, 1, LANE), lambda b: (layer, 0, 0)),
        pl.BlockSpec((None, 1, SSM_INNER), lambda b: (layer, 0, 0)),
        pl.BlockSpec((None, 1, SSM_INNER), lambda b: (layer, 0, 0)),
    ]
    args += [dt_raw, conv_w, conv_b.reshape(DEPTH, 1, CONV_CH), a_log_row, dt_bias_row, d_row,
             norm_gain.reshape(DEPTH, 1, SSM_INNER)]
    if has_h0:
        in_specs.append(pl.BlockSpec((None, None, 2, SSM_INNER, SSM_STATE), lambda b: (b, layer, 0, 0, 0)))
        args.append(h0.reshape(DEC_BATCH, DEPTH, 2, SSM_INNER, SSM_STATE))
    if emit_state and layer:
        in_specs.append(pl.BlockSpec((None, layer, 2, SSM_INNER, SSM_STATE), lambda b: (b, 0, 0, 0, 0)))
        args.append(state_prev)

    out_specs = [pl.BlockSpec((seq_len, SSM_INNER), lambda b: (b, 0))]
    out_shape = [jax.ShapeDtypeStruct((n_seq * seq_len, SSM_INNER), BF16)]
    if emit_state:
        out_specs.append(pl.BlockSpec((None, layer + 1, 2, SSM_INNER, SSM_STATE), lambda b: (b, 0, 0, 0, 0)))
        out_shape.append(jax.ShapeDtypeStruct((BATCH, layer + 1, 2, SSM_INNER, SSM_STATE), F32))

    return pl.pallas_call(
        functools.partial(_ssd_body, layer=layer, seq_len=seq_len, has_h0=has_h0, emit_state=emit_state),
        grid=(n_seq,),
        in_specs=in_specs,
        out_specs=out_specs,
        out_shape=out_shape,
        scratch_shapes=[
            pltpu.VMEM((seq_len, CONV_CH), BF16),
            pltpu.VMEM((seq_len, SSM_INNER), F32),
            pltpu.VMEM((2, SSM_STATE, SSM_INNER), F32),
        ],
        compiler_params=_cparams(1),
        name=f"ssd_{'lat' if latent else 'ctx'}",
    )(*args)


def _rope_tables():
    rows = DEC_SEQ // GRID_W
    r = jnp.repeat(jnp.arange(rows, dtype=F32), GRID_W)
    c = jnp.tile(jnp.arange(GRID_W, dtype=F32), rows)
    n_freq = HEAD_DIM // 4
    inv_freq = ROPE_THETA ** (-jnp.arange(n_freq, dtype=F32) / n_freq)
    ang = jnp.concatenate([r[:, None] * inv_freq, c[:, None] * inv_freq], axis=-1)
    cos, sin = jnp.cos(ang), jnp.sin(ang)
    return jnp.concatenate([cos, cos], axis=-1), jnp.concatenate([-sin, sin], axis=-1)


def kernel(x_prompt, x_sample, cache_a_k, cache_a_v, cache_c_k, cache_c_v, state_ssm, c, c_ctx, norm1_g, w_ada, b_ada, w_in, a_sink, conv_w, conv_b, ssm_a_log, ssm_dt_bias, ssm_d, ssm_norm_g, c_q_norm, c_k_norm, w_oa, w_ob, w_oc, w_out, norm2_g, w_mlp1, w_mlp2, final_norm_g):
    x = (x_prompt.reshape(T_CTX, D_MODEL), x_sample.reshape(T_LAT, D_MODEL))
    cond = jnp.concatenate([c, c_ctx[None, :], jnp.zeros((MOD_ROWS - DEC_BATCH - 1, D_MODEL), F32)], axis=0)
    mod = ada_modulation(cond, w_ada, b_ada).reshape(DEPTH, MOD_ROWS, N_MOD, 1, D_MODEL)
    rope_tabs = _rope_tables()
    lane_pad = jnp.zeros((DEPTH, LANE - 2 * SSM_HEADS), F32)
    a_log_row = jnp.concatenate([ssm_a_log.reshape(DEPTH, 2 * SSM_HEADS), lane_pad], axis=1).reshape(DEPTH, 1, LANE)
    dt_bias_row = jnp.concatenate([ssm_dt_bias.reshape(DEPTH, 2 * SSM_HEADS), lane_pad], axis=1).reshape(DEPTH, 1, LANE)
    d_row = jnp.repeat(ssm_d, SSM_HEAD_DIM, axis=1).reshape(DEPTH, 1, SSM_INNER)
    w_in_t = jnp.swapaxes(w_in, 1, 2)
    w_out_bf = w_out.astype(BF16)

    kv_a = kv_c = state = None
    for layer in range(DEPTH):
        h, dt_raw = prenorm(x, norm1_g, mod, layer, w_in_t)
        pr = in_projection(h, w_in_t, layer)

        ya_c, *kv_a = attention(pr, layer, latent=False, mixer="a", sink=a_sink, kv_prev=kv_a)
        (ya_l,) = attention(pr, layer, latent=True, mixer="a", sink=a_sink, ctx_k=cache_a_k, ctx_v=cache_a_v,
                            rope_tabs=rope_tabs)
        yc_c, *kv_c = attention(pr, layer, latent=False, mixer="c", q_gain=c_q_norm, k_gain=c_k_norm, kv_prev=kv_c)
        (yc_l,) = attention(pr, layer, latent=True, mixer="c", q_gain=c_q_norm, k_gain=c_k_norm, ctx_k=cache_c_k,
                            ctx_v=cache_c_v, rope_tabs=rope_tabs)
        ssd_args = dict(conv_w=conv_w, conv_b=conv_b, a_log_row=a_log_row, dt_bias_row=dt_bias_row, d_row=d_row,
                        norm_gain=ssm_norm_g)
        yb_c, state = ssd_mixer(pr, dt_raw, layer, latent=False, state_prev=state, **ssd_args)
        (yb_l,) = ssd_mixer(pr, dt_raw, layer, latent=True, h0=state_ssm, **ssd_args)

        merged = merge_branches((ya_c, ya_l), (yb_c, yb_l), (yc_c, yc_l), pr, w_oa, w_ob, w_oc, layer)
        x1, h2 = out_projection_norm(merged, x, w_out_bf, norm2_g, mod, layer)
        x = mlp_down(mlp_up(h2, w_mlp1, layer), x1, w_mlp2, mod, layer)

    y_prompt, y_sample = final_norm(x, final_norm_g)
    kv5 = (BATCH, DEPTH, SEQ, KV_HEADS, HEAD_DIM)
    return (y_prompt.reshape(BATCH, SEQ, D_MODEL), y_sample.reshape(DEC_BATCH, DEC_SEQ, D_MODEL),
            kv_a[0].reshape(kv5), kv_a[1].reshape(kv5), kv_c[0].reshape(kv5), kv_c[1].reshape(kv5),
            state.reshape(BATCH, DEPTH, 2, SSM_HEADS, SSM_HEAD_DIM, SSM_STATE))
```

```python
import functools
import math

import jax
import jax.numpy as jnp
from jax.experimental import pallas as pl
from jax.experimental.pallas import tpu as pltpu

F32 = jnp.float32
BF16 = jnp.bfloat16

D_MODEL = 2048
BATCH = 16
SEQ = 256
DEPTH = 2
DEC_BATCH = 4
DEC_SEQ = 1024
PAST_LEN = 256
GRID_W = 64
HEAD_DIM = 128
ROPE_THETA = 10000.0
EPS = 1e-6
Q_HEADS = 8
KV_HEADS = 2
Q_PER_KV = Q_HEADS // KV_HEADS
A_WINDOW = 128
SSM_HEADS = 32
SSM_HEAD_DIM = 64
SSM_GROUPS = 2
SSM_STATE = 128
CONV_K = 3
D_FF = 4 * D_MODEL
QW = Q_HEADS * HEAD_DIM
KVW = KV_HEADS * HEAD_DIM
SSM_INNER = SSM_HEADS * SSM_HEAD_DIM
CONV_CH = SSM_INNER + 2 * SSM_GROUPS * SSM_STATE
N_MOD = 6

T_CTX = BATCH * SEQ
T_LAT = DEC_BATCH * DEC_SEQ
T_ALL = T_CTX + T_LAT
MOD_ROWS = 8
CTX_MOD_ROW = DEC_BATCH

R1_COLS = QW + 2 * KVW + SSM_INNER + CONV_CH
DT_COLS = 2 * SSM_HEADS
R2_COLS = QW + 2 * KVW + 3 * D_MODEL
PR_COLS = R1_COLS + R2_COLS
OFF_Q, OFF_K, OFF_V = 0, QW, QW + KVW
OFF_Z = QW + 2 * KVW
OFF_XBC = OFF_Z + SSM_INNER
OFF_GATES = R1_COLS + QW + 2 * KVW

LANE = 128
HALF_LANE = LANE // 2
SSD_CHUNK = 128
VMEM_LIMIT = 56 * 1024 * 1024
LOG2E = math.log2(math.e)


def _cparams(n_axes, vmem=VMEM_LIMIT):
    return pltpu.CompilerParams(dimension_semantics=("arbitrary",) * n_axes, vmem_limit_bytes=vmem)


def _mod_row(tile, tm):
    n_ctx_tiles = T_CTX // tm
    return jnp.where(tile < n_ctx_tiles, CTX_MOD_ROW, (tile - n_ctx_tiles) // (DEC_SEQ // tm))


def _ctx_tile(tile, tm):
    return jnp.minimum(tile, T_CTX // tm - 1)


def _lat_tile(tile, tm):
    return jnp.maximum(tile - T_CTX // tm, 0)


def _sigmoid(x):
    return 0.5 * jnp.tanh(0.5 * x) + 0.5


def _silu(x):
    return x * _sigmoid(x)


def _rms(x, gain):
    return x * jax.lax.rsqrt(jnp.mean(x * x, axis=-1, keepdims=True) + EPS) * gain


def _ada_body(cond_ref, w_ref, b_ref, o_ref):
    a = _silu(cond_ref[...]).astype(BF16)
    o_ref[...] = jnp.dot(a, w_ref[...].astype(BF16), preferred_element_type=F32) + b_ref[...]


def ada_modulation(cond, w_ada, b_ada):
    tn = 1024
    n_out = N_MOD * D_MODEL
    return pl.pallas_call(
        _ada_body,
        grid=(DEPTH, n_out // tn),
        in_specs=[
            pl.BlockSpec((MOD_ROWS, D_MODEL), lambda l, n: (0, 0)),
            pl.BlockSpec((None, D_MODEL, tn), lambda l, n: (l, 0, n)),
            pl.BlockSpec((None, 1, tn), lambda l, n: (l, 0, n)),
        ],
        out_specs=pl.BlockSpec((None, MOD_ROWS, tn), lambda l, n: (l, 0, n)),
        out_shape=jax.ShapeDtypeStruct((DEPTH, MOD_ROWS, n_out), F32),
        compiler_params=_cparams(2),
        name="ada_modulation",
    )(cond, w_ada, b_ada.reshape(DEPTH, 1, n_out))


PRENORM_TM = 512


def _prenorm_body(*refs, split_x):
    if split_x:
        xc_ref, xl_ref, g_ref, sh_ref, sc_ref, wdt_ref, h_ref, dt_ref = refs
        x = jnp.where(pl.program_id(0) < T_CTX // PRENORM_TM, xc_ref[...], xl_ref[...])
    else:
        x_ref, g_ref, sh_ref, sc_ref, wdt_ref, h_ref, dt_ref = refs
        x = x_ref[...]
    h = (_rms(x, g_ref[...]) * (1.0 + sc_ref[...]) + sh_ref[...]).astype(BF16)
    dt_ref[...] = jax.lax.dot_general(h, wdt_ref[...].astype(BF16), (((1,), (1,)), ((), ())),
                                      preferred_element_type=F32)
    h_ref[...] = h


def prenorm(x, gain, mod, layer, w_in_t):
    tm = PRENORM_TM
    split_x = isinstance(x, tuple)
    mod_spec = lambda k: pl.BlockSpec((None, None, None, 1, D_MODEL),
                                      lambda i: (layer, _mod_row(i, tm), k, 0, 0))
    if split_x:
        x_specs = [pl.BlockSpec((tm, D_MODEL), lambda i: (_ctx_tile(i, tm), 0)),
                   pl.BlockSpec((tm, D_MODEL), lambda i: (_lat_tile(i, tm), 0))]
        x_args = list(x)
    else:
        x_specs = [pl.BlockSpec((tm, D_MODEL), lambda i: (i, 0))]
        x_args = [x]
    return pl.pallas_call(
        functools.partial(_prenorm_body, split_x=split_x),
        grid=(T_ALL // tm,),
        in_specs=x_specs + [
            pl.BlockSpec((None, 1, D_MODEL), lambda i: (layer, 0, 0)),
            mod_spec(0),
            mod_spec(1),
            pl.BlockSpec((None, LANE, D_MODEL), lambda i: (layer, R1_COLS // LANE, 0)),
        ],
        out_specs=[pl.BlockSpec((tm, D_MODEL), lambda i: (i, 0)), pl.BlockSpec((tm, LANE), lambda i: (i, 0))],
        out_shape=[jax.ShapeDtypeStruct((T_ALL, D_MODEL), BF16), jax.ShapeDtypeStruct((T_ALL, LANE), F32)],
        compiler_params=_cparams(1),
        name="prenorm_dt",
    )(*x_args, gain.reshape(DEPTH, 1, D_MODEL), mod, mod, w_in_t)


PROJ_TM = 1024
PROJ_TN = 1536
assert OFF_GATES % PROJ_TN == 0 and R1_COLS % PROJ_TN == 0


def _stream_weight_piece(w_ref, wbf_ref, n_chunks, piece):
    n, m = pl.program_id(0), pl.program_id(1)

    @pl.when(n < n_chunks)
    def _():
        wbf_ref[n % 2, pl.ds(pl.multiple_of(m * piece, piece), piece), :] = w_ref[...].astype(BF16)


def _stream_maps(n_chunks, m_tiles):
    tile = lambda n, m: jnp.where(n == 0, 0, m)
    chunk = lambda n: jnp.maximum(n - 1, 0)
    piece = lambda n, m: jnp.where(n < n_chunks, m, m_tiles - 1)
    load_chunk = lambda n: jnp.minimum(n, n_chunks - 1)
    return tile, chunk, piece, load_chunk


PROJ_CHUNKS = PR_COLS // PROJ_TN
PROJ_PIECE = PROJ_TN // (T_ALL // PROJ_TM)


def _proj_body(h_ref, w_ref, o_ref, wbf_ref):
    _stream_weight_piece(w_ref, wbf_ref, PROJ_CHUNKS, PROJ_PIECE)
    n = pl.program_id(0)

    def project():
        return jax.lax.dot_general(h_ref[...], wbf_ref[(n + 1) % 2], (((1,), (1,)), ((), ())),
                                   preferred_element_type=F32)

    @pl.when((n > 0) & (n - 1 < OFF_GATES // PROJ_TN))
    def _():
        o_ref[...] = project().astype(o_ref.dtype)

    @pl.when(n - 1 >= OFF_GATES // PROJ_TN)
    def _():
        o_ref[...] = _sigmoid(project()).astype(o_ref.dtype)


def in_projection(h, w_in_t, layer):
    tm, tn = PROJ_TM, PROJ_TN
    tile, chunk, piece, load_chunk = _stream_maps(PROJ_CHUNKS, T_ALL // tm)
    first_row = lambda c: c * tn + jnp.where(c >= R1_COLS // tn, DT_COLS, 0)
    piece_row = lambda n, m: pl.multiple_of(first_row(load_chunk(n)) + piece(n, m) * PROJ_PIECE, DT_COLS)
    return pl.pallas_call(
        _proj_body,
        grid=(PROJ_CHUNKS + 1, T_ALL // tm),
        in_specs=[
            pl.BlockSpec((tm, D_MODEL), lambda n, m: (tile(n, m), 0)),
            pl.BlockSpec((None, pl.Element(PROJ_PIECE), pl.Element(D_MODEL)), lambda n, m: (layer, piece_row(n, m), 0)),
        ],
        out_specs=pl.BlockSpec((tm, tn), lambda n, m: (tile(n, m), chunk(n))),
        out_shape=jax.ShapeDtypeStruct((T_ALL, PR_COLS), BF16),
        scratch_shapes=[pltpu.VMEM((2, tn, D_MODEL), BF16)],
        compiler_params=_cparams(2),
        name="in_projection",
    )(h, w_in_t)


MERGE_TM = 512
MERGE_TN = 512
assert OFF_GATES % MERGE_TN == 0 and D_MODEL % MERGE_TN == 0
MERGE_K = QW + SSM_INNER + QW


def _merge_body(yac_ref, yal_ref, ybc_ref, ybl_ref, ycc_ref, ycl_ref, ga_ref, gb_ref, gc_ref,
                wa_ref, wb_ref, wc_ref, o_ref, wbf_ref):
    m = pl.program_id(1)

    @pl.when(m == 0)
    def _():
        wbf_ref[0:QW, :] = wa_ref[...].astype(BF16)
        wbf_ref[QW:QW + SSM_INNER, :] = wb_ref[...].astype(BF16)
        wbf_ref[QW + SSM_INNER:MERGE_K, :] = wc_ref[...].astype(BF16)

    def merge(ya_ref, yb_ref, yc_ref):
        br_a = jnp.dot(ya_ref[...], wbf_ref[0:QW, :], preferred_element_type=F32)
        br_b = jnp.dot(yb_ref[...], wbf_ref[QW:QW + SSM_INNER, :], preferred_element_type=F32)
        br_c = jnp.dot(yc_ref[...], wbf_ref[QW + SSM_INNER:MERGE_K, :], preferred_element_type=F32)
        merged = (ga_ref[...].astype(F32) * br_a + gb_ref[...].astype(F32) * br_b
                  + gc_ref[...].astype(F32) * br_c)
        o_ref[...] = merged.astype(BF16)

    @pl.when(m < T_CTX // MERGE_TM)
    def _():
        merge(yac_ref, ybc_ref, ycc_ref)

    @pl.when(m >= T_CTX // MERGE_TM)
    def _():
        merge(yal_ref, ybl_ref, ycl_ref)


def merge_branches(ya, yb, yc, pr, w_oa, w_ob, w_oc, layer):
    tm, tn = MERGE_TM, MERGE_TN
    gate_spec = lambda k: pl.BlockSpec((tm, tn), lambda n, m: (m, (OFF_GATES + k * D_MODEL) // tn + n))
    w_spec = lambda rows: pl.BlockSpec((None, rows, tn), lambda n, m: (layer, 0, n))
    pair_specs = lambda cols: [pl.BlockSpec((tm, cols), lambda n, m: (_ctx_tile(m, tm), 0)),
                               pl.BlockSpec((tm, cols), lambda n, m: (_lat_tile(m, tm), 0))]
    return pl.pallas_call(
        _merge_body,
        grid=(D_MODEL // tn, T_ALL // tm),
        in_specs=pair_specs(QW) + pair_specs(SSM_INNER) + pair_specs(QW) + [
            gate_spec(0), gate_spec(1), gate_spec(2),
            w_spec(QW), w_spec(SSM_INNER), w_spec(QW),
        ],
        out_specs=pl.BlockSpec((tm, tn), lambda n, m: (m, n)),
        out_shape=jax.ShapeDtypeStruct((T_ALL, D_MODEL), BF16),
        scratch_shapes=[pltpu.VMEM((MERGE_K, tn), BF16)],
        compiler_params=_cparams(2),
        name="merge_branches",
    )(*ya, *yb, *yc, pr, pr, pr, w_oa, w_ob, w_oc)


OUTPROJ_TM = 512


def _outproj_body(*refs, split_x):
    if split_x:
        a_ref, w_ref, xc_ref, xl_ref, g1_ref, ng_ref, sh_ref, sc_ref, x1_ref, h2_ref = refs
        x = jnp.where(pl.program_id(0) < T_CTX // OUTPROJ_TM, xc_ref[...], xl_ref[...])
    else:
        a_ref, w_ref, x_ref, g1_ref, ng_ref, sh_ref, sc_ref, x1_ref, h2_ref = refs
        x = x_ref[...]
    x1 = x + g1_ref[...] * jnp.dot(a_ref[...], w_ref[...], preferred_element_type=F32)
    x1_ref[...] = x1
    h2_ref[...] = (_rms(x1, ng_ref[...]) * (1.0 + sc_ref[...]) + sh_ref[...]).astype(BF16)


def out_projection_norm(merged, x, w_out_bf, gain2, mod, layer):
    tm = OUTPROJ_TM
    split_x = isinstance(x, tuple)
    mod_spec = lambda k: pl.BlockSpec((None, None, None, 1, D_MODEL), lambda i: (layer, _mod_row(i, tm), k, 0, 0))
    if split_x:
        x_specs = [pl.BlockSpec((tm, D_MODEL), lambda i: (_ctx_tile(i, tm), 0)),
                   pl.BlockSpec((tm, D_MODEL), lambda i: (_lat_tile(i, tm), 0))]
        x_args = list(x)
    else:
        x_specs = [pl.BlockSpec((tm, D_MODEL), lambda i: (i, 0))]
        x_args = [x]
    return pl.pallas_call(
        functools.partial(_outproj_body, split_x=split_x),
        grid=(T_ALL // tm,),
        in_specs=[
            pl.BlockSpec((tm, D_MODEL), lambda i: (i, 0)),
            pl.BlockSpec((None, D_MODEL, D_MODEL), lambda i: (layer, 0, 0), pipeline_mode=pl.Buffered(1)),
        ] + x_specs + [
            mod_spec(2),
            pl.BlockSpec((None, 1, D_MODEL), lambda i: (layer, 0, 0)),
            mod_spec(3),
            mod_spec(4),
        ],
        out_specs=[pl.BlockSpec((tm, D_MODEL), lambda i: (i, 0)), pl.BlockSpec((tm, D_MODEL), lambda i: (i, 0))],
        out_shape=[jax.ShapeDtypeStruct((T_ALL, D_MODEL), F32), jax.ShapeDtypeStruct((T_ALL, D_MODEL), BF16)],
        compiler_params=_cparams(1),
        name="out_projection_norm",
    )(merged, w_out_bf, *x_args, mod, gain2.reshape(DEPTH, 1, D_MODEL), mod, mod)


MLP_UP_TM, MLP_UP_TN = 2048, 1024
MLP_UP_CHUNKS = D_FF // MLP_UP_TN
MLP_UP_PIECE = D_MODEL // (T_ALL // MLP_UP_TM)


def _mlp_up_body(h_ref, w_ref, o_ref, wbf_ref):
    _stream_weight_piece(w_ref, wbf_ref, MLP_UP_CHUNKS, MLP_UP_PIECE)
    n = pl.program_id(0)

    @pl.when(n > 0)
    def _():
        hid = jnp.dot(h_ref[...], wbf_ref[(n + 1) % 2], preferred_element_type=F32)
        o_ref[...] = jnp.square(jnp.maximum(hid, 0.0)).astype(BF16)


def mlp_up(h2, w_mlp1, layer):
    tm, tn = MLP_UP_TM, MLP_UP_TN
    tile, chunk, piece, load_chunk = _stream_maps(MLP_UP_CHUNKS, T_ALL // tm)
    return pl.pallas_call(
        _mlp_up_body,
        grid=(MLP_UP_CHUNKS + 1, T_ALL // tm),
        in_specs=[
            pl.BlockSpec((tm, D_MODEL), lambda n, m: (tile(n, m), 0)),
            pl.BlockSpec((None, MLP_UP_PIECE, tn), lambda n, m: (layer, piece(n, m), load_chunk(n))),
        ],
        out_specs=pl.BlockSpec((tm, tn), lambda n, m: (tile(n, m), chunk(n))),
        out_shape=jax.ShapeDtypeStruct((T_ALL, D_FF), BF16),
        scratch_shapes=[pltpu.VMEM((2, D_MODEL, tn), BF16)],
        compiler_params=_cparams(2),
        name="mlp_up",
    )(h2, w_mlp1)


MLP_DOWN_TM, MLP_DOWN_TN = 512, 512
MLP_DOWN_CHUNKS = D_MODEL // MLP_DOWN_TN
MLP_DOWN_PIECE = D_FF // (T_ALL // MLP_DOWN_TM)


def _mlp_down_body(a_ref, w_ref, x_ref, g_ref, o_ref, wbf_ref):
    _stream_weight_piece(w_ref, wbf_ref, MLP_DOWN_CHUNKS, MLP_DOWN_PIECE)
    n = pl.program_id(0)

    @pl.when(n > 0)
    def _():
        o_ref[...] = x_ref[...] + g_ref[...] * jnp.dot(a_ref[...], wbf_ref[(n + 1) % 2],
                                                       preferred_element_type=F32)


def mlp_down(hid, x1, w_mlp2, mod, layer):
    tm, tn = MLP_DOWN_TM, MLP_DOWN_TN
    tile, chunk, piece, load_chunk = _stream_maps(MLP_DOWN_CHUNKS, T_ALL // tm)
    return pl.pallas_call(
        _mlp_down_body,
        grid=(MLP_DOWN_CHUNKS + 1, T_ALL // tm),
        in_specs=[
            pl.BlockSpec((tm, D_FF), lambda n, m: (tile(n, m), 0)),
            pl.BlockSpec((None, MLP_DOWN_PIECE, tn), lambda n, m: (layer, piece(n, m), load_chunk(n))),
            pl.BlockSpec((tm, tn), lambda n, m: (tile(n, m), chunk(n))),
            pl.BlockSpec((None, None, None, 1, tn),
                         lambda n, m: (layer, _mod_row(tile(n, m), tm), 5, 0, chunk(n))),
        ],
        out_specs=pl.BlockSpec((tm, tn), lambda n, m: (tile(n, m), chunk(n))),
        out_shape=jax.ShapeDtypeStruct((T_ALL, D_MODEL), F32),
        scratch_shapes=[pltpu.VMEM((2, D_FF, tn), BF16)],
        compiler_params=_cparams(2),
        name="mlp_down",
    )(hid, w_mlp2, x1, mod)


FINAL_TM = 512


def _final_norm_body(x_ref, g_ref, yp_ref, ys_ref):
    y = _rms(x_ref[...], g_ref[...])
    i = pl.program_id(0)

    @pl.when(i < T_CTX // FINAL_TM)
    def _():
        yp_ref[...] = y

    @pl.when(i >= T_CTX // FINAL_TM)
    def _():
        ys_ref[...] = y


def final_norm(x, gain):
    tm = FINAL_TM
    return pl.pallas_call(
        _final_norm_body,
        grid=(T_ALL // tm,),
        in_specs=[
            pl.BlockSpec((tm, D_MODEL), lambda i: (i, 0)),
            pl.BlockSpec((1, D_MODEL), lambda i: (0, 0)),
        ],
        out_specs=[
            pl.BlockSpec((tm, D_MODEL), lambda i: (_ctx_tile(i, tm), 0)),
            pl.BlockSpec((tm, D_MODEL), lambda i: (_lat_tile(i, tm), 0)),
        ],
        out_shape=[jax.ShapeDtypeStruct((T_CTX, D_MODEL), F32), jax.ShapeDtypeStruct((T_LAT, D_MODEL), F32)],
        compiler_params=_cparams(1),
        name="final_norm",
    )(x, gain.reshape(1, D_MODEL))


ATT_TQ = 256


def _rope(x, cos2, sin2):
    return x * cos2 + pltpu.roll(x, HALF_LANE, 1) * sin2


def _attn_body(*refs, layer, tq, n_ctx, seq_len, use_sink, band, qk_norm, rope, emit_kv):
    it = iter(refs)
    q_ref, k_ref, v_ref = next(it), next(it), next(it)
    kctx_ref = vctx_ref = sink_ref = qg_ref = kg_ref = cosq_ref = sinq_ref = cosk_ref = sink_k_ref = None
    kprev_ref = vprev_ref = kout_ref = vout_ref = None
    if n_ctx:
        kctx_ref, vctx_ref = next(it), next(it)
    if use_sink:
        sink_ref = next(it)
    if qk_norm:
        qg_ref, kg_ref = next(it), next(it)
    if rope:
        cosq_ref, sinq_ref, cosk_ref, sink_k_ref = next(it), next(it), next(it), next(it)
    if emit_kv and layer:
        kprev_ref, vprev_ref = next(it), next(it)
    o_ref = next(it)
    if emit_kv:
        kout_ref, vout_ref = next(it), next(it)
    kall_ref, vall_ref = next(it), next(it)

    j = pl.program_id(1)
    n = pl.program_id(2)

    @pl.when(n == 0)
    def _():
        k = k_ref[...].astype(F32)
        if qk_norm:
            k = _rms(k, kg_ref[...])
        if emit_kv:
            if layer:
                kout_ref[0:layer] = kprev_ref[...]
                vout_ref[0:layer] = vprev_ref[...]
            kout_ref[layer] = k
            vout_ref[layer] = v_ref[...].astype(F32)
        if rope:
            k = _rope(k, cosk_ref[...], sink_k_ref[...])
        if n_ctx:
            kall_ref[0:n_ctx, :] = kctx_ref[...].astype(BF16)
            vall_ref[0:n_ctx, 0:HEAD_DIM] = vctx_ref[...].astype(BF16)
        kall_ref[n_ctx:n_ctx + seq_len, :] = k.astype(BF16)
        vall_ref[n_ctx:n_ctx + seq_len, 0:HEAD_DIM] = v_ref[...]
        vall_ref[:, HEAD_DIM:2 * HEAD_DIM] = jnp.ones((n_ctx + seq_len, HEAD_DIM), BF16)

    if band:
        win = tq + 2 * A_WINDOW
        w0 = pl.multiple_of(jnp.clip(n * tq - A_WINDOW, 0, seq_len - win), A_WINDOW)
        slabs = [(0, n_ctx), (n_ctx + w0, win)]
        qpos = n * tq + jax.lax.broadcasted_iota(jnp.int32, (tq, win), 0)
        kpos = w0 + jax.lax.broadcasted_iota(jnp.int32, (tq, win), 1)
        visible = jnp.abs(kpos - qpos) <= A_WINDOW
    else:
        slabs = [(0, n_ctx + seq_len)]
    scale2 = HEAD_DIM ** -0.5 * LOG2E
    for g in range(Q_PER_KV):
        q = q_ref[:, g * HEAD_DIM:(g + 1) * HEAD_DIM].astype(F32)
        if qk_norm:
            q = _rms(q, qg_ref[...])
        if rope:
            q = _rope(q, cosq_ref[...], sinq_ref[...])
        q = (q * scale2).astype(BF16)
        scores = []
        for idx, (k0, rows) in enumerate(slabs):
            s = jax.lax.dot_general(q, kall_ref[pl.ds(k0, rows), :], (((1,), (1,)), ((), ())),
                                    preferred_element_type=F32)
            if band and idx == 1:
                s = jnp.where(visible, s, -jnp.inf)
            scores.append(s)
        m = functools.reduce(jnp.maximum, [jnp.max(s, axis=-1, keepdims=True) for s in scores])
        if use_sink:
            sk2 = sink_ref[layer * Q_HEADS + j * Q_PER_KV + g] * LOG2E
            m = jnp.maximum(m, sk2)
        acc = None
        for s, (k0, rows) in zip(scores, slabs):
            p = jnp.exp2(s - m).astype(BF16)
            part = jnp.dot(p, vall_ref[pl.ds(k0, rows), :], preferred_element_type=F32)
            acc = part if acc is None else acc + part
        den = acc[:, HEAD_DIM:2 * HEAD_DIM]
        if use_sink:
            den = den + jnp.exp2(sk2 - m)
        o_ref[:, g * HEAD_DIM:(g + 1) * HEAD_DIM] = (acc[:, 0:HEAD_DIM] * (1.0 / den)).astype(BF16)


def attention(pr, layer, *, latent, mixer, ctx_k=None, ctx_v=None, sink=None, q_gain=None, k_gain=None,
              rope_tabs=None, kv_prev=None):
    n_seq, seq_len, row0 = (DEC_BATCH, DEC_SEQ, T_CTX) if latent else (BATCH, SEQ, 0)
    n_ctx = PAST_LEN if latent else 0
    base = 0 if mixer == "a" else R1_COLS
    use_sink = mixer == "a"
    qk_norm = mixer == "c"
    band = latent and mixer == "a"
    rope = latent
    emit_kv = not latent
    tq = ATT_TQ
    qblocks = seq_len // tq
    grid = (n_seq, KV_HEADS, qblocks)
    qw_kv = Q_PER_KV * HEAD_DIM

    in_specs = [
        pl.BlockSpec((tq, qw_kv), lambda b, j, n: (row0 // tq + b * qblocks + n, (base + OFF_Q) // qw_kv + j)),
        pl.BlockSpec((seq_len, HEAD_DIM), lambda b, j, n: (row0 // seq_len + b, (base + OFF_K) // HEAD_DIM + j)),
        pl.BlockSpec((seq_len, HEAD_DIM), lambda b, j, n: (row0 // seq_len + b, (base + OFF_V) // HEAD_DIM + j)),
    ]
    args = [pr, pr, pr]
    if n_ctx:
        cache_spec = pl.BlockSpec((None, None, PAST_LEN, HEAD_DIM), lambda b, j, n: (b, layer, 0, j))
        in_specs += [cache_spec, cache_spec]
        args += [ctx_k.reshape(DEC_BATCH, DEPTH, PAST_LEN, KVW), ctx_v.reshape(DEC_BATCH, DEPTH, PAST_LEN, KVW)]
    if use_sink:
        in_specs.append(pl.BlockSpec(memory_space=pltpu.SMEM))
        args.append(sink.reshape(DEPTH * Q_HEADS))
    if qk_norm:
        gain_spec = pl.BlockSpec((None, 1, HEAD_DIM), lambda b, j, n: (layer, 0, 0))
        in_specs += [gain_spec, gain_spec]
        args += [q_gain.reshape(DEPTH, 1, HEAD_DIM), k_gain.reshape(DEPTH, 1, HEAD_DIM)]
    if rope:
        cos2, sin2 = rope_tabs
        in_specs += [pl.BlockSpec((tq, HEAD_DIM), lambda b, j, n: (n, 0))] * 2
        in_specs += [pl.BlockSpec((seq_len, HEAD_DIM), lambda b, j, n: (0, 0))] * 2
        args += [cos2, sin2, cos2, sin2]
    if emit_kv and layer:
        prev_spec = pl.BlockSpec((None, layer, SEQ, HEAD_DIM), lambda b, j, n: (b, 0, 0, j))
        in_specs += [prev_spec, prev_spec]
        args += list(kv_prev)

    out_specs = [pl.BlockSpec((tq, qw_kv), lambda b, j, n: (b * qblocks + n, j))]
    out_shape = [jax.ShapeDtypeStruct((n_seq * seq_len, QW), BF16)]
    if emit_kv:
        kv_spec = pl.BlockSpec((None, layer + 1, SEQ, HEAD_DIM), lambda b, j, n: (b, 0, 0, j))
        out_specs += [kv_spec, kv_spec]
        out_shape += [jax.ShapeDtypeStruct((BATCH, layer + 1, SEQ, KVW), F32)] * 2

    return pl.pallas_call(
        functools.partial(_attn_body, layer=layer, tq=tq, n_ctx=n_ctx, seq_len=seq_len, use_sink=use_sink, band=band,
                          qk_norm=qk_norm, rope=rope, emit_kv=emit_kv),
        grid=grid,
        in_specs=in_specs,
        out_specs=out_specs,
        out_shape=out_shape,
        scratch_shapes=[pltpu.VMEM((n_ctx + seq_len, HEAD_DIM), BF16),
                        pltpu.VMEM((n_ctx + seq_len, 2 * HEAD_DIM), BF16)],
        compiler_params=_cparams(3),
        name=f"attn_{mixer}_{'lat' if latent else 'ctx'}",
    )(*args)


SSD_COLBLK = 512
N_Z_BLK = SSM_INNER // SSD_COLBLK
N_XBC_BLK = CONV_CH // SSD_COLBLK
HALO = 16
GROUP_W = SSM_INNER // SSM_GROUPS
PAIRS = SSM_HEADS // 2
PAIRS_PER_GROUP = PAIRS // SSM_GROUPS


def _softplus(x):
    return jnp.maximum(x, 0.0) + jnp.log1p(jnp.exp(-jnp.abs(x)))


def _split3(x):
    hi = x.astype(BF16)
    r = x - hi.astype(F32)
    mid = r.astype(BF16)
    lo = (r - mid.astype(F32)).astype(BF16)
    return hi, mid, lo


def _ssd_body(*refs, layer, seq_len, has_h0, emit_state):
    it = iter(refs)
    z_refs = [next(it) for _ in range(N_Z_BLK)]
    xbc_refs = [next(it) for _ in range(N_XBC_BLK)]
    dt_ref, convw_ref, convb_ref, alog_ref, dtb_ref, dexp_ref, ng_ref = (next(it) for _ in range(7))
    h0_ref = next(it) if has_h0 else None
    stprev_ref = next(it) if emit_state and layer else None
    y_ref = next(it)
    st_ref = next(it) if emit_state else None
    conv_scr, y_scr, h_scr = next(it), next(it), next(it)

    n_chunks = seq_len // SSD_CHUNK
    q = SSD_CHUNK
    lane = jax.lax.broadcasted_iota(jnp.int32, (q, LANE), 1)
    row = jax.lax.broadcasted_iota(jnp.int32, (q, LANE), 0)
    low_half = lane < HALF_LANE

    for d in range(2):
        for blk in range(SSM_INNER // LANE):
            cols = slice(blk * LANE, (blk + 1) * LANE)
            if has_h0:
                h_scr[d, :, cols] = h0_ref[d, cols, :].T
            else:
                h_scr[d, :, cols] = jnp.zeros((SSM_STATE, LANE), F32)

    def conv_chunk(c, carry):
        r0 = pl.multiple_of(c * q, q)
        prev0 = pl.multiple_of(jnp.maximum(r0 - HALO, 0), HALO)
        next0 = pl.multiple_of(jnp.minimum(r0 + q, seq_len - HALO), HALO)
        has_prev = (r0 > 0).astype(F32)
        has_next = (r0 + q < seq_len).astype(F32)
        for j in range(CONV_CH // LANE):
            src = xbc_refs[j // (SSD_COLBLK // LANE)]
            sc = slice((j % (SSD_COLBLK // LANE)) * LANE, (j % (SSD_COLBLK // LANE) + 1) * LANE)
            cols = slice(j * LANE, (j + 1) * LANE)
            u = src[pl.ds(r0, q), sc].astype(F32)
            prev_row = src[pl.ds(prev0, HALO), sc].astype(F32)[HALO - 1:HALO, :] * has_prev
            next_row = src[pl.ds(next0, HALO), sc].astype(F32)[0:1, :] * has_next
            up = jnp.where(row == 0, prev_row, pltpu.roll(u, 1, 0))
            dn = jnp.where(row == q - 1, next_row, pltpu.roll(u, q - 1, 0))
            v = (convw_ref[0:1, cols] * up + convw_ref[1:2, cols] * u + convw_ref[2:3, cols] * dn
                 + convb_ref[:, cols])
            act = _silu(v)
            conv_scr[pl.ds(r0, q), cols] = act.astype(BF16)
            if j < SSM_INNER // LANE:
                y_scr[pl.ds(r0, q), cols] = dexp_ref[:, cols] * act
        return carry

    jax.lax.fori_loop(0, n_chunks, conv_chunk, 0)

    def scan_chunk(c, d):
        r0 = pl.multiple_of(c * q, q)
        edge = q - 1 if d == 0 else 0
        vis = (row >= lane) if d == 0 else (row <= lane)
        tri = jnp.where(vis, 1.0, 0.0).astype(BF16)
        dt = _softplus(dt_ref[pl.ds(r0, q), :] + dtb_ref[...])
        a = dt * (-jnp.exp(alog_ref[...]))
        a_hi, a_mid, a_lo = _split3(a)
        acum = (jnp.dot(tri, a_hi, preferred_element_type=F32) + jnp.dot(tri, a_mid, preferred_element_type=F32)
                + jnp.dot(tri, a_lo, preferred_element_type=F32))
        acum2 = acum * LOG2E
        acum_t = acum.T
        dt_t = dt.T
        row_t2 = (acum_t - jnp.log(dt_t)) * LOG2E
        w_t = dt_t * jnp.exp(acum_t[:, edge:edge + 1] - acum_t)
        for g in range(SSM_GROUPS):
            b_g = conv_scr[pl.ds(r0, q), SSM_INNER + g * SSM_STATE:SSM_INNER + (g + 1) * SSM_STATE]
            c_lo = SSM_INNER + SSM_GROUPS * SSM_STATE + g * SSM_STATE
            c_g = conv_scr[pl.ds(r0, q), c_lo:c_lo + SSM_STATE]
            cb = jax.lax.dot_general(c_g, b_g, (((1,), (1,)), ((), ())),
                                     preferred_element_type=F32).astype(BF16)
            b_t = b_g.astype(F32).T.astype(BF16)
            h_g = h_scr[d, :, g * GROUP_W:(g + 1) * GROUP_W].astype(BF16)
            y_off = jnp.dot(c_g, h_g, preferred_element_type=F32)
            for pp in range(PAIRS_PER_GROUP):
                p = g * PAIRS_PER_GROUP + pp
                cols = slice(p * LANE, (p + 1) * LANE)
                lhs_top, lhs_bot, decay = [], [], []
                for h in (2 * p, 2 * p + 1):
                    ell = d * SSM_HEADS + h
                    col = jnp.broadcast_to(acum2[:, ell:ell + 1], (q, LANE))
                    seg = jnp.exp2(jnp.where(vis, col - row_t2[ell:ell + 1, :], -jnp.inf))
                    lhs_top.append(cb * seg.astype(BF16))
                    lhs_bot.append(b_t * w_t[ell:ell + 1, :].astype(BF16))
                    decay.append(jnp.exp2(col))
                lhs = jnp.concatenate([jnp.concatenate(lhs_top, axis=1), jnp.concatenate(lhs_bot, axis=1)],
                                      axis=0)
                xp = conv_scr[pl.ds(r0, q), cols]
                zero = jnp.zeros_like(xp)
                rhs = jnp.concatenate([jnp.where(low_half, xp, zero), jnp.where(low_half, zero, xp)], axis=0)
                res = jnp.dot(lhs, rhs, preferred_element_type=F32)
                factor = jnp.where(low_half, decay[0], decay[1])
                y_scr[pl.ds(r0, q), cols] += res[0:q, :] + y_off[:, pp * LANE:(pp + 1) * LANE] * factor
                h_scr[d, :, cols] = h_scr[d, :, cols] * factor[edge:edge + 1, :] + res[q:2 * q, :]

    def scan_step(i, carry):
        scan_chunk(i, 0)
        scan_chunk(n_chunks - 1 - i, 1)
        return carry

    jax.lax.fori_loop(0, n_chunks, scan_step, 0)

    if emit_state:
        if layer:
            st_ref[0:layer] = stprev_ref[...]
        for d in range(2):
            for blk in range(SSM_INNER // LANE):
                cols = slice(blk * LANE, (blk + 1) * LANE)
                st_ref[layer, d, cols, :] = h_scr[d, :, cols].T

    def finish_chunk(c, carry):
        r0 = pl.multiple_of(c * q, q)
        gated = []
        for k in range(N_Z_BLK):
            cols = slice(k * SSD_COLBLK, (k + 1) * SSD_COLBLK)
            gated.append(y_scr[pl.ds(r0, q), cols] * _silu(z_refs[k][pl.ds(r0, q), :].astype(F32)))
        ssq = sum(jnp.sum(gk * gk, axis=-1, keepdims=True) for gk in gated)
        inv = jax.lax.rsqrt(ssq * (1.0 / SSM_INNER) + EPS)
        for k in range(N_Z_BLK):
            cols = slice(k * SSD_COLBLK, (k + 1) * SSD_COLBLK)
            y_ref[pl.ds(r0, q), cols] = (gated[k] * inv * ng_ref[:, cols]).astype(BF16)
        return carry

    jax.lax.fori_loop(0, n_chunks, finish_chunk, 0)


def ssd_mixer(pr, dt_raw, layer, *, latent, conv_w, conv_b, a_log_row, dt_bias_row, d_row, norm_gain,
              h0=None, state_prev=None):
    n_seq, seq_len, row0 = (DEC_BATCH, DEC_SEQ, T_CTX) if latent else (BATCH, SEQ, 0)
    has_h0 = latent
    emit_state = not latent
    seq_blk = row0 // seq_len

    def col_spec(first, k):
        return pl.BlockSpec((seq_len, SSD_COLBLK), lambda b: (seq_blk + b, first // SSD_COLBLK + k))

    in_specs = [col_spec(OFF_Z, k) for k in range(N_Z_BLK)] + [col_spec(OFF_XBC, k) for k in range(N_XBC_BLK)]
    args = [pr] * (N_Z_BLK + N_XBC_BLK)
    in_specs += [
        pl.BlockSpec((seq_len, LANE), lambda b: (seq_blk + b, 0)),
        pl.BlockSpec((None, CONV_K, CONV_CH), lambda b: (layer, 0, 0)),
        pl.BlockSpec((None, 1, CONV_CH), lambda b: (layer, 0, 0)),
        pl.BlockSpec((None, 1, LANE), lambda b: (layer, 0, 0)),
        pl.BlockSpec((None, 1, LANE), lambda b: (layer, 0, 0)),
        pl.BlockSpec((None, 1, SSM_INNER), lambda b: (layer, 0, 0)),
        pl.BlockSpec((None, 1, SSM_INNER), lambda b: (layer, 0, 0)),
    ]
    args += [dt_raw, conv_w, conv_b.reshape(DEPTH, 1, CONV_CH), a_log_row, dt_bias_row, d_row,
             norm_gain.reshape(DEPTH, 1, SSM_INNER)]
    if has_h0:
        in_specs.append(pl.BlockSpec((None, None, 2, SSM_INNER, SSM_STATE), lambda b: (b, layer, 0, 0, 0)))
        args.append(h0.reshape(DEC_BATCH, DEPTH, 2, SSM_INNER, SSM_STATE))
    if emit_state and layer:
        in_specs.append(pl.BlockSpec((None, layer, 2, SSM_INNER, SSM_STATE), lambda b: (b, 0, 0, 0, 0)))
        args.append(state_prev)

    out_specs = [pl.BlockSpec((seq_len, SSM_INNER), lambda b: (b, 0))]
    out_shape = [jax.ShapeDtypeStruct((n_seq * seq_len, SSM_INNER), BF16)]
    if emit_state:
        out_specs.append(pl.BlockSpec((None, layer + 1, 2, SSM_INNER, SSM_STATE), lambda b: (b, 0, 0, 0, 0)))
        out_shape.append(jax.ShapeDtypeStruct((BATCH, layer + 1, 2, SSM_INNER, SSM_STATE), F32))

    return pl.pallas_call(
        functools.partial(_ssd_body, layer=layer, seq_len=seq_len, has_h0=has_h0, emit_state=emit_state),
        grid=(n_seq,),
        in_specs=in_specs,
        out_specs=out_specs,
        out_shape=out_shape,
        scratch_shapes=[
            pltpu.VMEM((seq_len, CONV_CH), BF16),
            pltpu.VMEM((seq_len, SSM_INNER), F32),
            pltpu.VMEM((2, SSM_STATE, SSM_INNER), F32),
        ],
        compiler_params=_cparams(1),
        name=f"ssd_{'lat' if latent else 'ctx'}",
    )(*args)


def _rope_tables():
    rows = DEC_SEQ // GRID_W
    r = jnp.repeat(jnp.arange(rows, dtype=F32), GRID_W)
    c = jnp.tile(jnp.arange(GRID_W, dtype=F32), rows)
    n_freq = HEAD_DIM // 4
    inv_freq = ROPE_THETA ** (-jnp.arange(n_freq, dtype=F32) / n_freq)
    ang = jnp.concatenate([r[:, None] * inv_freq, c[:, None] * inv_freq], axis=-1)
    cos, sin = jnp.cos(ang), jnp.sin(ang)
    return jnp.concatenate([cos, cos], axis=-1), jnp.concatenate([-sin, sin], axis=-1)


def kernel(x_prompt, x_sample, cache_a_k, cache_a_v, cache_c_k, cache_c_v, state_ssm, c, c_ctx, norm1_g, w_ada, b_ada, w_in, a_sink, conv_w, conv_b, ssm_a_log, ssm_dt_bias, ssm_d, ssm_norm_g, c_q_norm, c_k_norm, w_oa, w_ob, w_oc, w_out, norm2_g, w_mlp1, w_mlp2, final_norm_g):
    x = (x_prompt.reshape(T_CTX, D_MODEL), x_sample.reshape(T_LAT, D_MODEL))
    cond = jnp.concatenate([c, c_ctx[None, :], jnp.zeros((MOD_ROWS - DEC_BATCH - 1, D_MODEL), F32)], axis=0)
    mod = ada_modulation(cond, w_ada, b_ada).reshape(DEPTH, MOD_ROWS, N_MOD, 1, D_MODEL)
    rope_tabs = _rope_tables()
    lane_pad = jnp.zeros((DEPTH, LANE - 2 * SSM_HEADS), F32)
    a_log_row = jnp.concatenate([ssm_a_log.reshape(DEPTH, 2 * SSM_HEADS), lane_pad], axis=1).reshape(DEPTH, 1, LANE)
    dt_bias_row = jnp.concatenate([ssm_dt_bias.reshape(DEPTH, 2 * SSM_HEADS), lane_pad], axis=1).reshape(DEPTH, 1, LANE)
    d_row = jnp.repeat(ssm_d, SSM_HEAD_DIM, axis=1).reshape(DEPTH, 1, SSM_INNER)
    w_in_t = jnp.swapaxes(w_in, 1, 2)
    w_out_bf = w_out.astype(BF16)

    kv_a = kv_c = state = None
    for layer in range(DEPTH):
        h, dt_raw = prenorm(x, norm1_g, mod, layer, w_in_t)
        pr = in_projection(h, w_in_t, layer)

        ya_c, *kv_a = attention(pr, layer, latent=False, mixer="a", sink=a_sink, kv_prev=kv_a)
        (ya_l,) = attention(pr, layer, latent=True, mixer="a", sink=a_sink, ctx_k=cache_a_k, ctx_v=cache_a_v,
                            rope_tabs=rope_tabs)
        yc_c, *kv_c = attention(pr, layer, latent=False, mixer="c", q_gain=c_q_norm, k_gain=c_k_norm, kv_prev=kv_c)
        (yc_l,) = attention(pr, layer, latent=True, mixer="c", q_gain=c_q_norm, k_gain=c_k_norm, ctx_k=cache_c_k,
                            ctx_v=cache_c_v, rope_tabs=rope_tabs)
        ssd_args = dict(conv_w=conv_w, conv_b=conv_b, a_log_row=a_log_row, dt_bias_row=dt_bias_row, d_row=d_row,
                        norm_gain=ssm_norm_g)
        yb_c, state = ssd_mixer(pr, dt_raw, layer, latent=False, state_prev=state, **ssd_args)
        (yb_l,) = ssd_mixer(pr, dt_raw, layer, latent=True, h0=state_ssm, **ssd_args)

        merged = merge_branches((ya_c, ya_l), (yb_c, yb_l), (yc_c, yc_l), pr, w_oa, w_ob, w_oc, layer)
        x1, h2 = out_projection_norm(merged, x, w_out_bf, norm2_g, mod, layer)
        x = mlp_down(mlp_up(h2, w_mlp1, layer), x1, w_mlp2, mod, layer)

    y_prompt, y_sample = final_norm(x, final_norm_g)
    kv5 = (BATCH, DEPTH, SEQ, KV_HEADS, HEAD_DIM)
    return (y_prompt.reshape(BATCH, SEQ, D_MODEL), y_sample.reshape(DEC_BATCH, DEC_SEQ, D_MODEL),
            kv_a[0].reshape(kv5), kv_a[1].reshape(kv5), kv_c[0].reshape(kv5), kv_c[1].reshape(kv5),
            state.reshape(BATCH, DEPTH, 2, SSM_HEADS, SSM_HEAD_DIM, SSM_STATE))
```

```python
import functools
import math

import jax
import jax.numpy as jnp
from jax.experimental import pallas as pl
from jax.experimental.pallas import tpu as pltpu

F32 = jnp.float32
BF16 = jnp.bfloat16

D_MODEL = 2048
BATCH = 16
SEQ = 256
DEPTH = 2
DEC_BATCH = 4
DEC_SEQ = 1024
PAST_LEN = 256
GRID_W = 64
HEAD_DIM = 128
ROPE_THETA = 10000.0
EPS = 1e-6
Q_HEADS = 8
KV_HEADS = 2
Q_PER_KV = Q_HEADS // KV_HEADS
A_WINDOW = 128
SSM_HEADS = 32
SSM_HEAD_DIM = 64
SSM_GROUPS = 2
SSM_STATE = 128
CONV_K = 3
D_FF = 4 * D_MODEL
QW = Q_HEADS * HEAD_DIM
KVW = KV_HEADS * HEAD_DIM
SSM_INNER = SSM_HEADS * SSM_HEAD_DIM
CONV_CH = SSM_INNER + 2 * SSM_GROUPS * SSM_STATE
N_MOD = 6

T_CTX = BATCH * SEQ
T_LAT = DEC_BATCH * DEC_SEQ
T_ALL = T_CTX + T_LAT
MOD_ROWS = 8
CTX_MOD_ROW = DEC_BATCH

R1_COLS = QW + 2 * KVW + SSM_INNER + CONV_CH
DT_COLS = 2 * SSM_HEADS
R2_COLS = QW + 2 * KVW + 3 * D_MODEL
PR_COLS = R1_COLS + R2_COLS
OFF_Q, OFF_K, OFF_V = 0, QW, QW + KVW
OFF_Z = QW + 2 * KVW
OFF_XBC = OFF_Z + SSM_INNER
OFF_GATES = R1_COLS + QW + 2 * KVW

LANE = 128
HALF_LANE = LANE // 2
SSD_CHUNK = 128
VMEM_LIMIT = 56 * 1024 * 1024
LOG2E = math.log2(math.e)


def _cparams(n_axes, vmem=VMEM_LIMIT):
    return pltpu.CompilerParams(dimension_semantics=("arbitrary",) * n_axes, vmem_limit_bytes=vmem)


def _mod_row(tile, tm):
    n_ctx_tiles = T_CTX // tm
    return jnp.where(tile < n_ctx_tiles, CTX_MOD_ROW, (tile - n_ctx_tiles) // (DEC_SEQ // tm))


def _ctx_tile(tile, tm):
    return jnp.minimum(tile, T_CTX // tm - 1)


def _lat_tile(tile, tm):
    return jnp.maximum(tile - T_CTX // tm, 0)


def _sigmoid(x):
    return 0.5 * jnp.tanh(0.5 * x) + 0.5


def _silu(x):
    return x * _sigmoid(x)


def _rms(x, gain):
    return x * jax.lax.rsqrt(jnp.mean(x * x, axis=-1, keepdims=True) + EPS) * gain


def _ada_body(cond_ref, w_ref, b_ref, o_ref):
    a = _silu(cond_ref[...]).astype(BF16)
    o_ref[...] = jnp.dot(a, w_ref[...].astype(BF16), preferred_element_type=F32) + b_ref[...]


def ada_modulation(cond, w_ada, b_ada):
    tn = 1024
    n_out = N_MOD * D_MODEL
    return pl.pallas_call(
        _ada_body,
        grid=(DEPTH, n_out // tn),
        in_specs=[
            pl.BlockSpec((MOD_ROWS, D_MODEL), lambda l, n: (0, 0)),
            pl.BlockSpec((None, D_MODEL, tn), lambda l, n: (l, 0, n)),
            pl.BlockSpec((None, 1, tn), lambda l, n: (l, 0, n)),
        ],
        out_specs=pl.BlockSpec((None, MOD_ROWS, tn), lambda l, n: (l, 0, n)),
        out_shape=jax.ShapeDtypeStruct((DEPTH, MOD_ROWS, n_out), F32),
        compiler_params=_cparams(2),
        name="ada_modulation",
    )(cond, w_ada, b_ada.reshape(DEPTH, 1, n_out))


PRENORM_TM = 512


def _prenorm_body(*refs, split_x):
    if split_x:
        xc_ref, xl_ref, g_ref, sh_ref, sc_ref, wdt_ref, h_ref, dt_ref = refs
        x = jnp.where(pl.program_id(0) < T_CTX // PRENORM_TM, xc_ref[...], xl_ref[...])
    else:
        x_ref, g_ref, sh_ref, sc_ref, wdt_ref, h_ref, dt_ref = refs
        x = x_ref[...]
    h = (_rms(x, g_ref[...]) * (1.0 + sc_ref[...]) + sh_ref[...]).astype(BF16)
    dt_ref[...] = jax.lax.dot_general(h, wdt_ref[...].astype(BF16), (((1,), (1,)), ((), ())),
                                      preferred_element_type=F32)
    h_ref[...] = h


def prenorm(x, gain, mod, layer, w_in_t):
    tm = PRENORM_TM
    split_x = isinstance(x, tuple)
    mod_spec = lambda k: pl.BlockSpec((None, None, None, 1, D_MODEL),
                                      lambda i: (layer, _mod_row(i, tm), k, 0, 0))
    if split_x:
        x_specs = [pl.BlockSpec((tm, D_MODEL), lambda i: (_ctx_tile(i, tm), 0)),
                   pl.BlockSpec((tm, D_MODEL), lambda i: (_lat_tile(i, tm), 0))]
        x_args = list(x)
    else:
        x_specs = [pl.BlockSpec((tm, D_MODEL), lambda i: (i, 0))]
        x_args = [x]
    return pl.pallas_call(
        functools.partial(_prenorm_body, split_x=split_x),
        grid=(T_ALL // tm,),
        in_specs=x_specs + [
            pl.BlockSpec((None, 1, D_MODEL), lambda i: (layer, 0, 0)),
            mod_spec(0),
            mod_spec(1),
            pl.BlockSpec((None, LANE, D_MODEL), lambda i: (layer, R1_COLS // LANE, 0)),
        ],
        out_specs=[pl.BlockSpec((tm, D_MODEL), lambda i: (i, 0)), pl.BlockSpec((tm, LANE), lambda i: (i, 0))],
        out_shape=[jax.ShapeDtypeStruct((T_ALL, D_MODEL), BF16), jax.ShapeDtypeStruct((T_ALL, LANE), F32)],
        compiler_params=_cparams(1),
        name="prenorm_dt",
    )(*x_args, gain.reshape(DEPTH, 1, D_MODEL), mod, mod, w_in_t)


PROJ_TM = 1024
PROJ_TN = 1536
assert OFF_GATES % PROJ_TN == 0 and R1_COLS % PROJ_TN == 0


def _stream_weight_piece(w_ref, wbf_ref, n_chunks, piece):
    n, m = pl.program_id(0), pl.program_id(1)

    @pl.when(n < n_chunks)
    def _():
        wbf_ref[n % 2, pl.ds(pl.multiple_of(m * piece, piece), piece), :] = w_ref[...].astype(BF16)


def _stream_maps(n_chunks, m_tiles):
    tile = lambda n, m: jnp.where(n == 0, 0, m)
    chunk = lambda n: jnp.maximum(n - 1, 0)
    piece = lambda n, m: jnp.where(n < n_chunks, m, m_tiles - 1)
    load_chunk = lambda n: jnp.minimum(n, n_chunks - 1)
    return tile, chunk, piece, load_chunk


PROJ_CHUNKS = PR_COLS // PROJ_TN
PROJ_PIECE = PROJ_TN // (T_ALL // PROJ_TM)


def _proj_body(h_ref, w_ref, o_ref, wbf_ref):
    _stream_weight_piece(w_ref, wbf_ref, PROJ_CHUNKS, PROJ_PIECE)
    n = pl.program_id(0)

    def project():
        return jax.lax.dot_general(h_ref[...], wbf_ref[(n + 1) % 2], (((1,), (1,)), ((), ())),
                                   preferred_element_type=F32)

    @pl.when((n > 0) & (n - 1 < OFF_GATES // PROJ_TN))
    def _():
        o_ref[...] = project().astype(o_ref.dtype)

    @pl.when(n - 1 >= OFF_GATES // PROJ_TN)
    def _():
        o_ref[...] = _sigmoid(project()).astype(o_ref.dtype)


def in_projection(h, w_in_t, layer):
    tm, tn = PROJ_TM, PROJ_TN
    tile, chunk, piece, load_chunk = _stream_maps(PROJ_CHUNKS, T_ALL // tm)
    first_row = lambda c: c * tn + jnp.where(c >= R1_COLS // tn, DT_COLS, 0)
    piece_row = lambda n, m: pl.multiple_of(first_row(load_chunk(n)) + piece(n, m) * PROJ_PIECE, DT_COLS)
    return pl.pallas_call(
        _proj_body,
        grid=(PROJ_CHUNKS + 1, T_ALL // tm),
        in_specs=[
            pl.BlockSpec((tm, D_MODEL), lambda n, m: (tile(n, m), 0)),
            pl.BlockSpec((None, pl.Element(PROJ_PIECE), pl.Element(D_MODEL)), lambda n, m: (layer, piece_row(n, m), 0)),
        ],
        out_specs=pl.BlockSpec((tm, tn), lambda n, m: (tile(n, m), chunk(n))),
        out_shape=jax.ShapeDtypeStruct((T_ALL, PR_COLS), BF16),
        scratch_shapes=[pltpu.VMEM((2, tn, D_MODEL), BF16)],
        compiler_params=_cparams(2),
        name="in_projection",
    )(h, w_in_t)


MERGE_TM = 1024
MERGE_TN = 512
assert OFF_GATES % MERGE_TN == 0 and D_MODEL % MERGE_TN == 0
MERGE_K = QW + SSM_INNER + QW


def _merge_body(ya_ref, yb_ref, yc_ref, ga_ref, gb_ref, gc_ref, wa_ref, wb_ref, wc_ref, o_ref, wbf_ref):
    @pl.when(pl.program_id(1) == 0)
    def _():
        wbf_ref[0:QW, :] = wa_ref[...].astype(BF16)
        wbf_ref[QW:QW + SSM_INNER, :] = wb_ref[...].astype(BF16)
        wbf_ref[QW + SSM_INNER:MERGE_K, :] = wc_ref[...].astype(BF16)

    br_a = jnp.dot(ya_ref[...], wbf_ref[0:QW, :], preferred_element_type=F32)
    br_b = jnp.dot(yb_ref[...], wbf_ref[QW:QW + SSM_INNER, :], preferred_element_type=F32)
    br_c = jnp.dot(yc_ref[...], wbf_ref[QW + SSM_INNER:MERGE_K, :], preferred_element_type=F32)
    merged = ga_ref[...].astype(F32) * br_a + gb_ref[...].astype(F32) * br_b + gc_ref[...].astype(F32) * br_c
    o_ref[...] = merged.astype(BF16)


def merge_branches(ya, yb, yc, pr, w_oa, w_ob, w_oc, layer, *, latent):
    tm, tn = MERGE_TM, MERGE_TN
    rows = T_LAT if latent else T_CTX
    tile0 = (T_CTX if latent else 0) // tm
    gate_spec = lambda k: pl.BlockSpec((tm, tn), lambda n, m: (tile0 + m, (OFF_GATES + k * D_MODEL) // tn + n))
    w_spec = lambda k_rows: pl.BlockSpec((None, k_rows, tn), lambda n, m: (layer, 0, n))
    y_spec = lambda cols: pl.BlockSpec((tm, cols), lambda n, m: (m, 0))
    return pl.pallas_call(
        _merge_body,
        grid=(D_MODEL // tn, rows // tm),
        in_specs=[
            y_spec(QW), y_spec(SSM_INNER), y_spec(QW),
            gate_spec(0), gate_spec(1), gate_spec(2),
            w_spec(QW), w_spec(SSM_INNER), w_spec(QW),
        ],
        out_specs=pl.BlockSpec((tm, tn), lambda n, m: (m, n)),
        out_shape=jax.ShapeDtypeStruct((rows, D_MODEL), BF16),
        scratch_shapes=[pltpu.VMEM((MERGE_K, tn), BF16)],
        compiler_params=_cparams(2),
        name=f"merge_{'lat' if latent else 'ctx'}",
    )(ya, yb, yc, pr, pr, pr, w_oa, w_ob, w_oc)


OUTPROJ_TM = 512


def _pick_group(refs, tm):
    if len(refs) == 1:
        return refs[0][...]
    return jnp.where(pl.program_id(0) < T_CTX // tm, refs[0][...], refs[1][...])


def _outproj_body(*refs, n_a, n_x):
    a_refs, refs = refs[:n_a], refs[n_a:]
    w_ref, refs = refs[0], refs[1:]
    x_refs, refs = refs[:n_x], refs[n_x:]
    g1_ref, ng_ref, sh_ref, sc_ref, x1_ref, h2_ref = refs
    a = _pick_group(a_refs, OUTPROJ_TM)
    x = _pick_group(x_refs, OUTPROJ_TM)
    x1 = x + g1_ref[...] * jnp.dot(a, w_ref[...], preferred_element_type=F32)
    x1_ref[...] = x1
    h2_ref[...] = (_rms(x1, ng_ref[...]) * (1.0 + sc_ref[...]) + sh_ref[...]).astype(BF16)


def _group_specs(x, tm):
    if isinstance(x, tuple):
        return [pl.BlockSpec((tm, D_MODEL), lambda i: (_ctx_tile(i, tm), 0)),
                pl.BlockSpec((tm, D_MODEL), lambda i: (_lat_tile(i, tm), 0))], list(x)
    return [pl.BlockSpec((tm, D_MODEL), lambda i: (i, 0))], [x]


def out_projection_norm(merged, x, w_out_bf, gain2, mod, layer):
    tm = OUTPROJ_TM
    mod_spec = lambda k: pl.BlockSpec((None, None, None, 1, D_MODEL), lambda i: (layer, _mod_row(i, tm), k, 0, 0))
    a_specs, a_args = _group_specs(merged, tm)
    x_specs, x_args = _group_specs(x, tm)
    return pl.pallas_call(
        functools.partial(_outproj_body, n_a=len(a_args), n_x=len(x_args)),
        grid=(T_ALL // tm,),
        in_specs=a_specs + [
            pl.BlockSpec((None, D_MODEL, D_MODEL), lambda i: (layer, 0, 0), pipeline_mode=pl.Buffered(1)),
        ] + x_specs + [
            mod_spec(2),
            pl.BlockSpec((None, 1, D_MODEL), lambda i: (layer, 0, 0)),
            mod_spec(3),
            mod_spec(4),
        ],
        out_specs=[pl.BlockSpec((tm, D_MODEL), lambda i: (i, 0)), pl.BlockSpec((tm, D_MODEL), lambda i: (i, 0))],
        out_shape=[jax.ShapeDtypeStruct((T_ALL, D_MODEL), F32), jax.ShapeDtypeStruct((T_ALL, D_MODEL), BF16)],
        compiler_params=_cparams(1),
        name="out_projection_norm",
    )(*a_args, w_out_bf, *x_args, mod, gain2.reshape(DEPTH, 1, D_MODEL), mod, mod)


MLP_UP_TM, MLP_UP_TN = 2048, 1024
MLP_UP_CHUNKS = D_FF // MLP_UP_TN
MLP_UP_PIECE = D_MODEL // (T_ALL // MLP_UP_TM)


def _mlp_up_body(h_ref, w_ref, o_ref, wbf_ref):
    _stream_weight_piece(w_ref, wbf_ref, MLP_UP_CHUNKS, MLP_UP_PIECE)
    n = pl.program_id(0)

    @pl.when(n > 0)
    def _():
        hid = jnp.dot(h_ref[...], wbf_ref[(n + 1) % 2], preferred_element_type=F32)
        o_ref[...] = jnp.square(jnp.maximum(hid, 0.0)).astype(BF16)


def mlp_up(h2, w_mlp1, layer):
    tm, tn = MLP_UP_TM, MLP_UP_TN
    tile, chunk, piece, load_chunk = _stream_maps(MLP_UP_CHUNKS, T_ALL // tm)
    return pl.pallas_call(
        _mlp_up_body,
        grid=(MLP_UP_CHUNKS + 1, T_ALL // tm),
        in_specs=[
            pl.BlockSpec((tm, D_MODEL), lambda n, m: (tile(n, m), 0)),
            pl.BlockSpec((None, MLP_UP_PIECE, tn), lambda n, m: (layer, piece(n, m), load_chunk(n))),
        ],
        out_specs=pl.BlockSpec((tm, tn), lambda n, m: (tile(n, m), chunk(n))),
        out_shape=jax.ShapeDtypeStruct((T_ALL, D_FF), BF16),
        scratch_shapes=[pltpu.VMEM((2, D_MODEL, tn), BF16)],
        compiler_params=_cparams(2),
        name="mlp_up",
    )(h2, w_mlp1)


MLP_DOWN_TM, MLP_DOWN_TN = 512, 512
MLP_DOWN_CHUNKS = D_MODEL // MLP_DOWN_TN
MLP_DOWN_PIECE = D_FF // (T_ALL // MLP_DOWN_TM)


def _mlp_down_body(a_ref, w_ref, x_ref, g_ref, o_ref, wbf_ref):
    _stream_weight_piece(w_ref, wbf_ref, MLP_DOWN_CHUNKS, MLP_DOWN_PIECE)
    n = pl.program_id(0)

    @pl.when(n > 0)
    def _():
        o_ref[...] = x_ref[...] + g_ref[...] * jnp.dot(a_ref[...], wbf_ref[(n + 1) % 2],
                                                       preferred_element_type=F32)


def mlp_down(hid, x1, w_mlp2, mod, layer):
    tm, tn = MLP_DOWN_TM, MLP_DOWN_TN
    tile, chunk, piece, load_chunk = _stream_maps(MLP_DOWN_CHUNKS, T_ALL // tm)
    return pl.pallas_call(
        _mlp_down_body,
        grid=(MLP_DOWN_CHUNKS + 1, T_ALL // tm),
        in_specs=[
            pl.BlockSpec((tm, D_FF), lambda n, m: (tile(n, m), 0)),
            pl.BlockSpec((None, MLP_DOWN_PIECE, tn), lambda n, m: (layer, piece(n, m), load_chunk(n))),
            pl.BlockSpec((tm, tn), lambda n, m: (tile(n, m), chunk(n))),
            pl.BlockSpec((None, None, None, 1, tn),
                         lambda n, m: (layer, _mod_row(tile(n, m), tm), 5, 0, chunk(n))),
        ],
        out_specs=pl.BlockSpec((tm, tn), lambda n, m: (tile(n, m), chunk(n))),
        out_shape=jax.ShapeDtypeStruct((T_ALL, D_MODEL), F32),
        scratch_shapes=[pltpu.VMEM((2, D_FF, tn), BF16)],
        compiler_params=_cparams(2),
        name="mlp_down",
    )(hid, w_mlp2, x1, mod)


FINAL_TM = 512


def _final_norm_body(x_ref, g_ref, yp_ref, ys_ref):
    y = _rms(x_ref[...], g_ref[...])
    i = pl.program_id(0)

    @pl.when(i < T_CTX // FINAL_TM)
    def _():
        yp_ref[...] = y

    @pl.when(i >= T_CTX // FINAL_TM)
    def _():
        ys_ref[...] = y


def final_norm(x, gain):
    tm = FINAL_TM
    return pl.pallas_call(
        _final_norm_body,
        grid=(T_ALL // tm,),
        in_specs=[
            pl.BlockSpec((tm, D_MODEL), lambda i: (i, 0)),
            pl.BlockSpec((1, D_MODEL), lambda i: (0, 0)),
        ],
        out_specs=[
            pl.BlockSpec((tm, D_MODEL), lambda i: (_ctx_tile(i, tm), 0)),
            pl.BlockSpec((tm, D_MODEL), lambda i: (_lat_tile(i, tm), 0)),
        ],
        out_shape=[jax.ShapeDtypeStruct((T_CTX, D_MODEL), F32), jax.ShapeDtypeStruct((T_LAT, D_MODEL), F32)],
        compiler_params=_cparams(1),
        name="final_norm",
    )(x, gain.reshape(1, D_MODEL))


ATT_TQ = 256


def _rope(x, cos2, sin2):
    return x * cos2 + pltpu.roll(x, HALF_LANE, 1) * sin2


def _attn_body(*refs, layer, tq, n_ctx, seq_len, use_sink, band, qk_norm, rope, emit_kv):
    it = iter(refs)
    q_ref, k_ref, v_ref = next(it), next(it), next(it)
    kctx_ref = vctx_ref = sink_ref = qg_ref = kg_ref = cosq_ref = sinq_ref = cosk_ref = sink_k_ref = None
    kprev_ref = vprev_ref = kout_ref = vout_ref = None
    if n_ctx:
        kctx_ref, vctx_ref = next(it), next(it)
    if use_sink:
        sink_ref = next(it)
    if qk_norm:
        qg_ref, kg_ref = next(it), next(it)
    if rope:
        cosq_ref, sinq_ref, cosk_ref, sink_k_ref = next(it), next(it), next(it), next(it)
    if emit_kv and layer:
        kprev_ref, vprev_ref = next(it), next(it)
    o_ref = next(it)
    if emit_kv:
        kout_ref, vout_ref = next(it), next(it)
    kall_ref, vall_ref = next(it), next(it)

    j = pl.program_id(1)
    n = pl.program_id(2)

    @pl.when(n == 0)
    def _():
        k = k_ref[...].astype(F32)
        if qk_norm:
            k = _rms(k, kg_ref[...])
        if emit_kv:
            if layer:
                @pl.when(j == 0)
                def _():
                    kout_ref[0:layer] = kprev_ref[...]
                    vout_ref[0:layer] = vprev_ref[...]
            head_rows = pl.ds(j, seq_len, stride=KV_HEADS)
            kout_ref[layer, head_rows, :] = k
            vout_ref[layer, head_rows, :] = v_ref[...].astype(F32)
        if rope:
            k = _rope(k, cosk_ref[...], sink_k_ref[...])
        if n_ctx:
            head_rows = pl.ds(j, n_ctx, stride=KV_HEADS)
            kall_ref[0:n_ctx, :] = kctx_ref[head_rows, :].astype(BF16)
            vall_ref[0:n_ctx, 0:HEAD_DIM] = vctx_ref[head_rows, :].astype(BF16)
        kall_ref[n_ctx:n_ctx + seq_len, :] = k.astype(BF16)
        vall_ref[n_ctx:n_ctx + seq_len, 0:HEAD_DIM] = v_ref[...]
        vall_ref[:, HEAD_DIM:2 * HEAD_DIM] = jnp.ones((n_ctx + seq_len, HEAD_DIM), BF16)

    if band:
        win = tq + 2 * A_WINDOW
        w0 = pl.multiple_of(jnp.clip(n * tq - A_WINDOW, 0, seq_len - win), A_WINDOW)
        slabs = [(0, n_ctx), (n_ctx + w0, win)]
        qpos = n * tq + jax.lax.broadcasted_iota(jnp.int32, (tq, win), 0)
        kpos = w0 + jax.lax.broadcasted_iota(jnp.int32, (tq, win), 1)
        visible = jnp.abs(kpos - qpos) <= A_WINDOW
    else:
        slabs = [(0, n_ctx + seq_len)]
    scale2 = HEAD_DIM ** -0.5 * LOG2E
    for g in range(Q_PER_KV):
        q = q_ref[:, g * HEAD_DIM:(g + 1) * HEAD_DIM].astype(F32)
        if qk_norm:
            q = _rms(q, qg_ref[...])
        if rope:
            q = _rope(q, cosq_ref[...], sinq_ref[...])
        q = (q * scale2).astype(BF16)
        scores = []
        for idx, (k0, rows) in enumerate(slabs):
            s = jax.lax.dot_general(q, kall_ref[pl.ds(k0, rows), :], (((1,), (1,)), ((), ())),
                                    preferred_element_type=F32)
            if band and idx == 1:
                s = jnp.where(visible, s, -jnp.inf)
            scores.append(s)
        m = functools.reduce(jnp.maximum, [jnp.max(s, axis=-1, keepdims=True) for s in scores])
        if use_sink:
            sk2 = sink_ref[layer * Q_HEADS + j * Q_PER_KV + g] * LOG2E
            m = jnp.maximum(m, sk2)
        acc = None
        for s, (k0, rows) in zip(scores, slabs):
            p = jnp.exp2(s - m).astype(BF16)
            part = jnp.dot(p, vall_ref[pl.ds(k0, rows), :], preferred_element_type=F32)
            acc = part if acc is None else acc + part
        den = acc[:, HEAD_DIM:2 * HEAD_DIM]
        if use_sink:
            den = den + jnp.exp2(sk2 - m)
        o_ref[:, g * HEAD_DIM:(g + 1) * HEAD_DIM] = (acc[:, 0:HEAD_DIM] * (1.0 / den)).astype(BF16)


def attention(pr, layer, *, latent, mixer, ctx_k=None, ctx_v=None, sink=None, q_gain=None, k_gain=None,
              rope_tabs=None, kv_prev=None):
    n_seq, seq_len, row0 = (DEC_BATCH, DEC_SEQ, T_CTX) if latent else (BATCH, SEQ, 0)
    n_ctx = PAST_LEN if latent else 0
    base = 0 if mixer == "a" else R1_COLS
    use_sink = mixer == "a"
    qk_norm = mixer == "c"
    band = latent and mixer == "a"
    rope = latent
    emit_kv = not latent
    tq = ATT_TQ
    qblocks = seq_len // tq
    grid = (n_seq, KV_HEADS, qblocks)
    qw_kv = Q_PER_KV * HEAD_DIM

    in_specs = [
        pl.BlockSpec((tq, qw_kv), lambda b, j, n: (row0 // tq + b * qblocks + n, (base + OFF_Q) // qw_kv + j)),
        pl.BlockSpec((seq_len, HEAD_DIM), lambda b, j, n: (row0 // seq_len + b, (base + OFF_K) // HEAD_DIM + j)),
        pl.BlockSpec((seq_len, HEAD_DIM), lambda b, j, n: (row0 // seq_len + b, (base + OFF_V) // HEAD_DIM + j)),
    ]
    args = [pr, pr, pr]
    if n_ctx:
        cache_spec = pl.BlockSpec((None, None, PAST_LEN * KV_HEADS, HEAD_DIM), lambda b, j, n: (b, layer, 0, 0))
        in_specs += [cache_spec, cache_spec]
        args += [ctx_k.reshape(DEC_BATCH, DEPTH, PAST_LEN * KV_HEADS, HEAD_DIM),
                 ctx_v.reshape(DEC_BATCH, DEPTH, PAST_LEN * KV_HEADS, HEAD_DIM)]
    if use_sink:
        in_specs.append(pl.BlockSpec(memory_space=pltpu.SMEM))
        args.append(sink.reshape(DEPTH * Q_HEADS))
    if qk_norm:
        gain_spec = pl.BlockSpec((None, 1, HEAD_DIM), lambda b, j, n: (layer, 0, 0))
        in_specs += [gain_spec, gain_spec]
        args += [q_gain.reshape(DEPTH, 1, HEAD_DIM), k_gain.reshape(DEPTH, 1, HEAD_DIM)]
    if rope:
        cos2, sin2 = rope_tabs
        in_specs += [pl.BlockSpec((tq, HEAD_DIM), lambda b, j, n: (n, 0))] * 2
        in_specs += [pl.BlockSpec((seq_len, HEAD_DIM), lambda b, j, n: (0, 0))] * 2
        args += [cos2, sin2, cos2, sin2]
    if emit_kv and layer:
        prev_spec = pl.BlockSpec((None, layer, SEQ * KV_HEADS, HEAD_DIM), lambda b, j, n: (b, 0, 0, 0))
        in_specs += [prev_spec, prev_spec]
        args += list(kv_prev)

    out_specs = [pl.BlockSpec((tq, qw_kv), lambda b, j, n: (b * qblocks + n, j))]
    out_shape = [jax.ShapeDtypeStruct((n_seq * seq_len, QW), BF16)]
    if emit_kv:
        kv_spec = pl.BlockSpec((None, layer + 1, SEQ * KV_HEADS, HEAD_DIM), lambda b, j, n: (b, 0, 0, 0))
        out_specs += [kv_spec, kv_spec]
        out_shape += [jax.ShapeDtypeStruct((BATCH, layer + 1, SEQ * KV_HEADS, HEAD_DIM), F32)] * 2

    return pl.pallas_call(
        functools.partial(_attn_body, layer=layer, tq=tq, n_ctx=n_ctx, seq_len=seq_len, use_sink=use_sink, band=band,
                          qk_norm=qk_norm, rope=rope, emit_kv=emit_kv),
        grid=grid,
        in_specs=in_specs,
        out_specs=out_specs,
        out_shape=out_shape,
        scratch_shapes=[pltpu.VMEM((n_ctx + seq_len, HEAD_DIM), BF16),
                        pltpu.VMEM((n_ctx + seq_len, 2 * HEAD_DIM), BF16)],
        compiler_params=_cparams(3),
        name=f"attn_{mixer}_{'lat' if latent else 'ctx'}",
    )(*args)


SSD_COLBLK = 512
N_Z_BLK = SSM_INNER // SSD_COLBLK
N_XBC_BLK = CONV_CH // SSD_COLBLK
HALO = 16
GROUP_W = SSM_INNER // SSM_GROUPS
PAIRS = SSM_HEADS // 2
PAIRS_PER_GROUP = PAIRS // SSM_GROUPS


def _softplus(x):
    return jnp.maximum(x, 0.0) + jnp.log1p(jnp.exp(-jnp.abs(x)))


def _split3(x):
    hi = x.astype(BF16)
    r = x - hi.astype(F32)
    mid = r.astype(BF16)
    lo = (r - mid.astype(F32)).astype(BF16)
    return hi, mid, lo


def _ssd_body(*refs, layer, seq_len, has_h0, emit_state):
    it = iter(refs)
    z_refs = [next(it) for _ in range(N_Z_BLK)]
    xbc_refs = [next(it) for _ in range(N_XBC_BLK)]
    dt_ref, convw_ref, convb_ref, alog_ref, dtb_ref, dexp_ref, ng_ref = (next(it) for _ in range(7))
    h0_ref = next(it) if has_h0 else None
    stprev_ref = next(it) if emit_state and layer else None
    y_ref = next(it)
    st_ref = next(it) if emit_state else None
    conv_scr, y_scr, h_scr = next(it), next(it), next(it)

    n_chunks = seq_len // SSD_CHUNK
    q = SSD_CHUNK
    lane = jax.lax.broadcasted_iota(jnp.int32, (q, LANE), 1)
    row = jax.lax.broadcasted_iota(jnp.int32, (q, LANE), 0)
    low_half = lane < HALF_LANE

    for d in range(2):
        for blk in range(SSM_INNER // LANE):
            cols = slice(blk * LANE, (blk + 1) * LANE)
            if has_h0:
                h_scr[d, :, cols] = h0_ref[d, cols, :].T
            else:
                h_scr[d, :, cols] = jnp.zeros((SSM_STATE, LANE), F32)

    def conv_chunk(c, carry):
        r0 = pl.multiple_of(c * q, q)
        prev0 = pl.multiple_of(jnp.maximum(r0 - HALO, 0), HALO)
        next0 = pl.multiple_of(jnp.minimum(r0 + q, seq_len - HALO), HALO)
        has_prev = (r0 > 0).astype(F32)
        has_next = (r0 + q < seq_len).astype(F32)
        for j in range(CONV_CH // LANE):
            src = xbc_refs[j // (SSD_COLBLK // LANE)]
            sc = slice((j % (SSD_COLBLK // LANE)) * LANE, (j % (SSD_COLBLK // LANE) + 1) * LANE)
            cols = slice(j * LANE, (j + 1) * LANE)
            u = src[pl.ds(r0, q), sc].astype(F32)
            prev_row = src[pl.ds(prev0, HALO), sc].astype(F32)[HALO - 1:HALO, :] * has_prev
            next_row = src[pl.ds(next0, HALO), sc].astype(F32)[0:1, :] * has_next
            up = jnp.where(row == 0, prev_row, pltpu.roll(u, 1, 0))
            dn = jnp.where(row == q - 1, next_row, pltpu.roll(u, q - 1, 0))
            v = (convw_ref[0:1, cols] * up + convw_ref[1:2, cols] * u + convw_ref[2:3, cols] * dn
                 + convb_ref[:, cols])
            act = _silu(v)
            conv_scr[pl.ds(r0, q), cols] = act.astype(BF16)
            if j < SSM_INNER // LANE:
                y_scr[pl.ds(r0, q), cols] = dexp_ref[:, cols] * act
        return carry

    jax.lax.fori_loop(0, n_chunks, conv_chunk, 0)

    def scan_chunk(c, d):
        r0 = pl.multiple_of(c * q, q)
        edge = q - 1 if d == 0 else 0
        vis = (row >= lane) if d == 0 else (row <= lane)
        tri = jnp.where(vis, 1.0, 0.0).astype(BF16)
        dt = _softplus(dt_ref[pl.ds(r0, q), :] + dtb_ref[...])
        a = dt * (-jnp.exp(alog_ref[...]))
        a_hi, a_mid, a_lo = _split3(a)
        acum = (jnp.dot(tri, a_hi, preferred_element_type=F32) + jnp.dot(tri, a_mid, preferred_element_type=F32)
                + jnp.dot(tri, a_lo, preferred_element_type=F32))
        acum2 = acum * LOG2E
        acum_t = acum.T
        dt_t = dt.T
        row_t2 = (acum_t - jnp.log(dt_t)) * LOG2E
        w_t = dt_t * jnp.exp(acum_t[:, edge:edge + 1] - acum_t)
        for g in range(SSM_GROUPS):
            b_g = conv_scr[pl.ds(r0, q), SSM_INNER + g * SSM_STATE:SSM_INNER + (g + 1) * SSM_STATE]
            c_lo = SSM_INNER + SSM_GROUPS * SSM_STATE + g * SSM_STATE
            c_g = conv_scr[pl.ds(r0, q), c_lo:c_lo + SSM_STATE]
            cb = jax.lax.dot_general(c_g, b_g, (((1,), (1,)), ((), ())),
                                     preferred_element_type=F32).astype(BF16)
            b_t = b_g.astype(F32).T.astype(BF16)
            h_g = h_scr[d, :, g * GROUP_W:(g + 1) * GROUP_W].astype(BF16)
            y_off = jnp.dot(c_g, h_g, preferred_element_type=F32)
            for pp in range(PAIRS_PER_GROUP):
                p = g * PAIRS_PER_GROUP + pp
                cols = slice(p * LANE, (p + 1) * LANE)
                lhs_top, lhs_bot, decay = [], [], []
                for h in (2 * p, 2 * p + 1):
                    ell = d * SSM_HEADS + h
                    col = jnp.broadcast_to(acum2[:, ell:ell + 1], (q, LANE))
                    seg = jnp.exp2(jnp.where(vis, col - row_t2[ell:ell + 1, :], -jnp.inf))
                    lhs_top.append(cb * seg.astype(BF16))
                    lhs_bot.append(b_t * w_t[ell:ell + 1, :].astype(BF16))
                    decay.append(jnp.exp2(col))
                lhs = jnp.concatenate([jnp.concatenate(lhs_top, axis=1), jnp.concatenate(lhs_bot, axis=1)],
                                      axis=0)
                xp = conv_scr[pl.ds(r0, q), cols]
                zero = jnp.zeros_like(xp)
                rhs = jnp.concatenate([jnp.where(low_half, xp, zero), jnp.where(low_half, zero, xp)], axis=0)
                res = jnp.dot(lhs, rhs, preferred_element_type=F32)
                factor = jnp.where(low_half, decay[0], decay[1])
                y_scr[pl.ds(r0, q), cols] += res[0:q, :] + y_off[:, pp * LANE:(pp + 1) * LANE] * factor
                h_scr[d, :, cols] = h_scr[d, :, cols] * factor[edge:edge + 1, :] + res[q:2 * q, :]

    def scan_step(i, carry):
        scan_chunk(i, 0)
        scan_chunk(n_chunks - 1 - i, 1)
        return carry

    jax.lax.fori_loop(0, n_chunks, scan_step, 0)

    if emit_state:
        if layer:
            st_ref[0:layer] = stprev_ref[...]
        for d in range(2):
            for blk in range(SSM_INNER // LANE):
                cols = slice(blk * LANE, (blk + 1) * LANE)
                st_ref[layer, d, cols, :] = h_scr[d, :, cols].T

    def finish_chunk(c, carry):
        r0 = pl.multiple_of(c * q, q)
        gated = []
        for k in range(N_Z_BLK):
            cols = slice(k * SSD_COLBLK, (k + 1) * SSD_COLBLK)
            gated.append(y_scr[pl.ds(r0, q), cols] * _silu(z_refs[k][pl.ds(r0, q), :].astype(F32)))
        ssq = sum(jnp.sum(gk * gk, axis=-1, keepdims=True) for gk in gated)
        inv = jax.lax.rsqrt(ssq * (1.0 / SSM_INNER) + EPS)
        for k in range(N_Z_BLK):
            cols = slice(k * SSD_COLBLK, (k + 1) * SSD_COLBLK)
            y_ref[pl.ds(r0, q), cols] = (gated[k] * inv * ng_ref[:, cols]).astype(BF16)
        return carry

    jax.lax.fori_loop(0, n_chunks, finish_chunk, 0)


def ssd_mixer(pr, dt_raw, layer, *, latent, conv_w, conv_b, a_log_row, dt_bias_row, d_row, norm_gain,
              h0=None, state_prev=None):
    n_seq, seq_len, row0 = (DEC_BATCH, DEC_SEQ, T_CTX) if latent else (BATCH, SEQ, 0)
    has_h0 = latent
    emit_state = not latent
    seq_blk = row0 // seq_len

    def col_spec(first, k):
        return pl.BlockSpec((seq_len, SSD_COLBLK), lambda b: (seq_blk + b, first // SSD_COLBLK + k))

    in_specs = [col_spec(OFF_Z, k) for k in range(N_Z_BLK)] + [col_spec(OFF_XBC, k) for k in range(N_XBC_BLK)]
    args = [pr] * (N_Z_BLK + N_XBC_BLK)
    in_specs += [
        pl.BlockSpec((seq_len, LANE), lambda b: (seq_blk + b, 0)),
        pl.BlockSpec((None, CONV_K, CONV_CH), lambda b: (layer, 0, 0)),
        pl.BlockSpec((None, 1, CONV_CH), lambda b: (layer, 0, 0)),
        pl.BlockSpec((None, 1, LANE), lambda b: (layer, 0, 0)),
        pl.BlockSpec((None, 1, LANE), lambda b: (layer, 0, 0)),
        pl.BlockSpec((None, 1, SSM_INNER), lambda b: (layer, 0, 0)),
        pl.BlockSpec((None, 1, SSM_INNER), lambda b: (layer, 0, 0)),
    ]
    args += [dt_raw, conv_w, conv_b.reshape(DEPTH, 1, CONV_CH), a_log_row, dt_bias_row, d_row,
             norm_gain.reshape(DEPTH, 1, SSM_INNER)]
    if has_h0:
        in_specs.append(pl.BlockSpec((None, None, 2, SSM_INNER, SSM_STATE), lambda b: (b, layer, 0, 0, 0)))
        args.append(h0.reshape(DEC_BATCH, DEPTH, 2, SSM_INNER, SSM_STATE))
    if emit_state and layer:
        in_specs.append(pl.BlockSpec((None, layer, 2, SSM_INNER, SSM_STATE), lambda b: (b, 0, 0, 0, 0)))
        args.append(state_prev)

    out_specs = [pl.BlockSpec((seq_len, SSM_INNER), lambda b: (b, 0))]
    out_shape = [jax.ShapeDtypeStruct((n_seq * seq_len, SSM_INNER), BF16)]
    if emit_state:
        out_specs.append(pl.BlockSpec((None, layer + 1, 2, SSM_INNER, SSM_STATE), lambda b: (b, 0, 0, 0, 0)))
        out_shape.append(jax.ShapeDtypeStruct((BATCH, layer + 1, 2, SSM_INNER, SSM_STATE), F32))

    return pl.pallas_call(
        functools.partial(_ssd_body, layer=layer, seq_len=seq_len, has_h0=has_h0, emit_state=emit_state),
        grid=(n_seq,),
        in_specs=in_specs,
        out_specs=out_specs,
        out_shape=out_shape,
        scratch_shapes=[
            pltpu.VMEM((seq_len, CONV_CH), BF16),
            pltpu.VMEM((seq_len, SSM_INNER), F32),
            pltpu.VMEM((2, SSM_STATE, SSM_INNER), F32),
        ],
        compiler_params=_cparams(1),
        name=f"ssd_{'lat' if latent else 'ctx'}",
    )(*args)


def _rope_tables():
    rows = DEC_SEQ // GRID_W
    r = jnp.repeat(jnp.arange(rows, dtype=F32), GRID_W)
    c = jnp.tile(jnp.arange(GRID_W, dtype=F32), rows)
    n_freq = HEAD_DIM // 4
    inv_freq = ROPE_THETA ** (-jnp.arange(n_freq, dtype=F32) / n_freq)
    ang = jnp.concatenate([r[:, None] * inv_freq, c[:, None] * inv_freq], axis=-1)
    cos, sin = jnp.cos(ang), jnp.sin(ang)
    return jnp.concatenate([cos, cos], axis=-1), jnp.concatenate([-sin, sin], axis=-1)


def kernel(x_prompt, x_sample, cache_a_k, cache_a_v, cache_c_k, cache_c_v, state_ssm, c, c_ctx, norm1_g, w_ada, b_ada, w_in, a_sink, conv_w, conv_b, ssm_a_log, ssm_dt_bias, ssm_d, ssm_norm_g, c_q_norm, c_k_norm, w_oa, w_ob, w_oc, w_out, norm2_g, w_mlp1, w_mlp2, final_norm_g):
    x = (x_prompt.reshape(T_CTX, D_MODEL), x_sample.reshape(T_LAT, D_MODEL))
    cond = jnp.concatenate([c, c_ctx[None, :], jnp.zeros((MOD_ROWS - DEC_BATCH - 1, D_MODEL), F32)], axis=0)
    mod = ada_modulation(cond, w_ada, b_ada).reshape(DEPTH, MOD_ROWS, N_MOD, 1, D_MODEL)
    rope_tabs = _rope_tables()
    lane_pad = jnp.zeros((DEPTH, LANE - 2 * SSM_HEADS), F32)
    a_log_row = jnp.concatenate([ssm_a_log.reshape(DEPTH, 2 * SSM_HEADS), lane_pad], axis=1).reshape(DEPTH, 1, LANE)
    dt_bias_row = jnp.concatenate([ssm_dt_bias.reshape(DEPTH, 2 * SSM_HEADS), lane_pad], axis=1).reshape(DEPTH, 1, LANE)
    d_row = jnp.repeat(ssm_d, SSM_HEAD_DIM, axis=1).reshape(DEPTH, 1, SSM_INNER)
    w_in_t = jnp.swapaxes(w_in, 1, 2)
    w_out_bf = w_out.astype(BF16)

    kv_a = kv_c = state = None
    for layer in range(DEPTH):
        h, dt_raw = prenorm(x, norm1_g, mod, layer, w_in_t)
        pr = in_projection(h, w_in_t, layer)

        ya_c, *kv_a = attention(pr, layer, latent=False, mixer="a", sink=a_sink, kv_prev=kv_a)
        (ya_l,) = attention(pr, layer, latent=True, mixer="a", sink=a_sink, ctx_k=cache_a_k, ctx_v=cache_a_v,
                            rope_tabs=rope_tabs)
        yc_c, *kv_c = attention(pr, layer, latent=False, mixer="c", q_gain=c_q_norm, k_gain=c_k_norm, kv_prev=kv_c)
        (yc_l,) = attention(pr, layer, latent=True, mixer="c", q_gain=c_q_norm, k_gain=c_k_norm, ctx_k=cache_c_k,
                            ctx_v=cache_c_v, rope_tabs=rope_tabs)
        ssd_args = dict(conv_w=conv_w, conv_b=conv_b, a_log_row=a_log_row, dt_bias_row=dt_bias_row, d_row=d_row,
                        norm_gain=ssm_norm_g)
        yb_c, state = ssd_mixer(pr, dt_raw, layer, latent=False, state_prev=state, **ssd_args)
        (yb_l,) = ssd_mixer(pr, dt_raw, layer, latent=True, h0=state_ssm, **ssd_args)

        merged = (merge_branches(ya_c, yb_c, yc_c, pr, w_oa, w_ob, w_oc, layer, latent=False),
                  merge_branches(ya_l, yb_l, yc_l, pr, w_oa, w_ob, w_oc, layer, latent=True))
        x1, h2 = out_projection_norm(merged, x, w_out_bf, norm2_g, mod, layer)
        x = mlp_down(mlp_up(h2, w_mlp1, layer), x1, w_mlp2, mod, layer)

    y_prompt, y_sample = final_norm(x, final_norm_g)
    kv5 = (BATCH, DEPTH, SEQ, KV_HEADS, HEAD_DIM)
    return (y_prompt.reshape(BATCH, SEQ, D_MODEL), y_sample.reshape(DEC_BATCH, DEC_SEQ, D_MODEL),
            kv_a[0].reshape(kv5), kv_a[1].reshape(kv5), kv_c[0].reshape(kv5), kv_c[1].reshape(kv5),
            state.reshape(BATCH, DEPTH, 2, SSM_HEADS, SSM_HEAD_DIM, SSM_STATE))
```

```python
import functools
import math

import jax
import jax.numpy as jnp
from jax.experimental import pallas as pl
from jax.experimental.pallas import tpu as pltpu

F32 = jnp.float32
BF16 = jnp.bfloat16

D_MODEL = 2048
BATCH = 16
SEQ = 256
DEPTH = 2
DEC_BATCH = 4
DEC_SEQ = 1024
PAST_LEN = 256
GRID_W = 64
HEAD_DIM = 128
ROPE_THETA = 10000.0
EPS = 1e-6
Q_HEADS = 8
KV_HEADS = 2
Q_PER_KV = Q_HEADS // KV_HEADS
A_WINDOW = 128
SSM_HEADS = 32
SSM_HEAD_DIM = 64
SSM_GROUPS = 2
SSM_STATE = 128
CONV_K = 3
D_FF = 4 * D_MODEL
QW = Q_HEADS * HEAD_DIM
KVW = KV_HEADS * HEAD_DIM
SSM_INNER = SSM_HEADS * SSM_HEAD_DIM
CONV_CH = SSM_INNER + 2 * SSM_GROUPS * SSM_STATE
N_MOD = 6

T_CTX = BATCH * SEQ
T_LAT = DEC_BATCH * DEC_SEQ
T_ALL = T_CTX + T_LAT
MOD_ROWS = 8
CTX_MOD_ROW = DEC_BATCH

R1_COLS = QW + 2 * KVW + SSM_INNER + CONV_CH
DT_COLS = 2 * SSM_HEADS
R2_COLS = QW + 2 * KVW + 3 * D_MODEL
PR_COLS = R1_COLS + R2_COLS
OFF_Q, OFF_K, OFF_V = 0, QW, QW + KVW
OFF_Z = QW + 2 * KVW
OFF_XBC = OFF_Z + SSM_INNER
OFF_GATES = R1_COLS + QW + 2 * KVW

LANE = 128
HALF_LANE = LANE // 2
SSD_CHUNK = 128
VMEM_LIMIT = 56 * 1024 * 1024
LOG2E = math.log2(math.e)


def _cparams(n_axes, vmem=VMEM_LIMIT):
    return pltpu.CompilerParams(dimension_semantics=("arbitrary",) * n_axes, vmem_limit_bytes=vmem)


def _mod_row(tile, tm):
    n_ctx_tiles = T_CTX // tm
    return jnp.where(tile < n_ctx_tiles, CTX_MOD_ROW, (tile - n_ctx_tiles) // (DEC_SEQ // tm))


def _ctx_tile(tile, tm):
    return jnp.minimum(tile, T_CTX // tm - 1)


def _lat_tile(tile, tm):
    return jnp.maximum(tile - T_CTX // tm, 0)


def _sigmoid(x):
    return 0.5 * jnp.tanh(0.5 * x) + 0.5


def _silu(x):
    return x * _sigmoid(x)


def _silu_from_half(half_x):
    return half_x + half_x * jnp.tanh(half_x)


def _rms(x, gain):
    return x * jax.lax.rsqrt(jnp.mean(x * x, axis=-1, keepdims=True) + EPS) * gain


def _ada_body(cond_ref, w_ref, b_ref, o_ref):
    a = _silu(cond_ref[...]).astype(BF16)
    o_ref[...] = jnp.dot(a, w_ref[...].astype(BF16), preferred_element_type=F32) + b_ref[...]


def ada_modulation(cond, w_ada, b_ada):
    tn = 1024
    n_out = N_MOD * D_MODEL
    return pl.pallas_call(
        _ada_body,
        grid=(DEPTH, n_out // tn),
        in_specs=[
            pl.BlockSpec((MOD_ROWS, D_MODEL), lambda l, n: (0, 0)),
            pl.BlockSpec((None, D_MODEL, tn), lambda l, n: (l, 0, n)),
            pl.BlockSpec((None, 1, tn), lambda l, n: (l, 0, n)),
        ],
        out_specs=pl.BlockSpec((None, MOD_ROWS, tn), lambda l, n: (l, 0, n)),
        out_shape=jax.ShapeDtypeStruct((DEPTH, MOD_ROWS, n_out), F32),
        compiler_params=_cparams(2),
        name="ada_modulation",
    )(cond, w_ada, b_ada.reshape(DEPTH, 1, n_out))


PRENORM_TM = 512


def _pick_group(refs, tm):
    if len(refs) == 1:
        return refs[0][...]
    return jnp.where(pl.program_id(0) < T_CTX // tm, refs[0][...], refs[1][...])


def _group_specs(x, tm):
    if isinstance(x, tuple):
        return [pl.BlockSpec((tm, D_MODEL), lambda i: (_ctx_tile(i, tm), 0)),
                pl.BlockSpec((tm, D_MODEL), lambda i: (_lat_tile(i, tm), 0))], list(x)
    return [pl.BlockSpec((tm, D_MODEL), lambda i: (i, 0))], [x]


def _prenorm_body(*refs, n_x):
    x_refs, (g_ref, sh_ref, sc_ref, wdt_ref, h_ref, dt_ref) = refs[:n_x], refs[n_x:]
    x = _pick_group(x_refs, PRENORM_TM)
    h = (_rms(x, g_ref[...]) * (1.0 + sc_ref[...]) + sh_ref[...]).astype(BF16)
    dt_ref[...] = jax.lax.dot_general(h, wdt_ref[...].astype(BF16), (((1,), (1,)), ((), ())),
                                      preferred_element_type=F32)
    h_ref[...] = h


def prenorm(x, gain, mod, layer, w_in_t):
    tm = PRENORM_TM
    mod_spec = lambda k: pl.BlockSpec((None, None, None, 1, D_MODEL),
                                      lambda i: (layer, _mod_row(i, tm), k, 0, 0))
    x_specs, x_args = _group_specs(x, tm)
    return pl.pallas_call(
        functools.partial(_prenorm_body, n_x=len(x_args)),
        grid=(T_ALL // tm,),
        in_specs=x_specs + [
            pl.BlockSpec((None, 1, D_MODEL), lambda i: (layer, 0, 0)),
            mod_spec(0),
            mod_spec(1),
            pl.BlockSpec((None, LANE, D_MODEL), lambda i: (layer, R1_COLS // LANE, 0)),
        ],
        out_specs=[pl.BlockSpec((tm, D_MODEL), lambda i: (i, 0)), pl.BlockSpec((tm, LANE), lambda i: (i, 0))],
        out_shape=[jax.ShapeDtypeStruct((T_ALL, D_MODEL), BF16), jax.ShapeDtypeStruct((T_ALL, LANE), F32)],
        compiler_params=_cparams(1),
        name="prenorm_dt",
    )(*x_args, gain.reshape(DEPTH, 1, D_MODEL), mod, mod, w_in_t)


PROJ_TM = 1024
PROJ_TN = 1536
assert OFF_GATES % PROJ_TN == 0 and R1_COLS % PROJ_TN == 0


def _stream_weight_piece(w_ref, wbf_ref, n_chunks, piece):
    n, m = pl.program_id(0), pl.program_id(1)

    @pl.when(n < n_chunks)
    def _():
        wbf_ref[n % 2, pl.ds(pl.multiple_of(m * piece, piece), piece), :] = w_ref[...].astype(BF16)


def _stream_maps(n_chunks, m_tiles):
    tile = lambda n, m: jnp.where(n == 0, 0, m)
    chunk = lambda n: jnp.maximum(n - 1, 0)
    piece = lambda n, m: jnp.where(n < n_chunks, m, m_tiles - 1)
    load_chunk = lambda n: jnp.minimum(n, n_chunks - 1)
    return tile, chunk, piece, load_chunk


PROJ_CHUNKS = PR_COLS // PROJ_TN
PROJ_PIECE = PROJ_TN // (T_ALL // PROJ_TM)


def _proj_body(h_ref, w_ref, o_ref, wbf_ref):
    _stream_weight_piece(w_ref, wbf_ref, PROJ_CHUNKS, PROJ_PIECE)
    n = pl.program_id(0)

    def project():
        return jax.lax.dot_general(h_ref[...], wbf_ref[(n + 1) % 2], (((1,), (1,)), ((), ())),
                                   preferred_element_type=F32)

    @pl.when((n > 0) & (n - 1 < OFF_GATES // PROJ_TN))
    def _():
        o_ref[...] = project().astype(o_ref.dtype)

    @pl.when(n - 1 >= OFF_GATES // PROJ_TN)
    def _():
        o_ref[...] = _sigmoid(project()).astype(o_ref.dtype)


def in_projection(h, w_in_t, layer):
    tm, tn = PROJ_TM, PROJ_TN
    tile, chunk, piece, load_chunk = _stream_maps(PROJ_CHUNKS, T_ALL // tm)
    first_row = lambda c: c * tn + jnp.where(c >= R1_COLS // tn, DT_COLS, 0)
    piece_row = lambda n, m: pl.multiple_of(first_row(load_chunk(n)) + piece(n, m) * PROJ_PIECE, DT_COLS)
    return pl.pallas_call(
        _proj_body,
        grid=(PROJ_CHUNKS + 1, T_ALL // tm),
        in_specs=[
            pl.BlockSpec((tm, D_MODEL), lambda n, m: (tile(n, m), 0)),
            pl.BlockSpec((None, pl.Element(PROJ_PIECE), pl.Element(D_MODEL)), lambda n, m: (layer, piece_row(n, m), 0)),
        ],
        out_specs=pl.BlockSpec((tm, tn), lambda n, m: (tile(n, m), chunk(n))),
        out_shape=jax.ShapeDtypeStruct((T_ALL, PR_COLS), BF16),
        scratch_shapes=[pltpu.VMEM((2, tn, D_MODEL), BF16)],
        compiler_params=_cparams(2),
        name="in_projection",
    )(h, w_in_t)


MERGE_TM = 1024
MERGE_TN = 512
assert OFF_GATES % MERGE_TN == 0 and D_MODEL % MERGE_TN == 0
MERGE_K = QW + SSM_INNER + QW


def _merge_body(ya_ref, yb_ref, yc_ref, ga_ref, gb_ref, gc_ref, wa_ref, wb_ref, wc_ref, o_ref, wbf_ref):
    @pl.when(pl.program_id(1) == 0)
    def _():
        wbf_ref[0:QW, :] = wa_ref[...].astype(BF16)
        wbf_ref[QW:QW + SSM_INNER, :] = wb_ref[...].astype(BF16)
        wbf_ref[QW + SSM_INNER:MERGE_K, :] = wc_ref[...].astype(BF16)

    br_a = jnp.dot(ya_ref[...], wbf_ref[0:QW, :], preferred_element_type=F32)
    br_b = jnp.dot(yb_ref[...], wbf_ref[QW:QW + SSM_INNER, :], preferred_element_type=F32)
    br_c = jnp.dot(yc_ref[...], wbf_ref[QW + SSM_INNER:MERGE_K, :], preferred_element_type=F32)
    merged = ga_ref[...].astype(F32) * br_a + gb_ref[...].astype(F32) * br_b + gc_ref[...].astype(F32) * br_c
    o_ref[...] = merged.astype(BF16)


def merge_branches(ya, yb, yc, pr, w_oa, w_ob, w_oc, layer, *, latent):
    tm, tn = MERGE_TM, MERGE_TN
    rows = T_LAT if latent else T_CTX
    tile0 = (T_CTX if latent else 0) // tm
    gate_spec = lambda k: pl.BlockSpec((tm, tn), lambda n, m: (tile0 + m, (OFF_GATES + k * D_MODEL) // tn + n))
    w_spec = lambda k_rows: pl.BlockSpec((None, k_rows, tn), lambda n, m: (layer, 0, n))
    y_spec = lambda cols: pl.BlockSpec((tm, cols), lambda n, m: (m, 0))
    return pl.pallas_call(
        _merge_body,
        grid=(D_MODEL // tn, rows // tm),
        in_specs=[
            y_spec(QW), y_spec(SSM_INNER), y_spec(QW),
            gate_spec(0), gate_spec(1), gate_spec(2),
            w_spec(QW), w_spec(SSM_INNER), w_spec(QW),
        ],
        out_specs=pl.BlockSpec((tm, tn), lambda n, m: (m, n)),
        out_shape=jax.ShapeDtypeStruct((rows, D_MODEL), BF16),
        scratch_shapes=[pltpu.VMEM((MERGE_K, tn), BF16)],
        compiler_params=_cparams(2),
        name=f"merge_{'lat' if latent else 'ctx'}",
    )(ya, yb, yc, pr, pr, pr, w_oa, w_ob, w_oc)


OUTPROJ_TM = 512


def _outproj_body(*refs, n_a, n_x):
    a_refs, refs = refs[:n_a], refs[n_a:]
    w_ref, refs = refs[0], refs[1:]
    x_refs, refs = refs[:n_x], refs[n_x:]
    g1_ref, ng_ref, sh_ref, sc_ref, x1_ref, h2_ref = refs
    a = _pick_group(a_refs, OUTPROJ_TM)
    x = _pick_group(x_refs, OUTPROJ_TM)
    x1 = x + g1_ref[...] * jnp.dot(a, w_ref[...], preferred_element_type=F32)
    x1_ref[...] = x1
    h2_ref[...] = (_rms(x1, ng_ref[...]) * (1.0 + sc_ref[...]) + sh_ref[...]).astype(BF16)


def out_projection_norm(merged, x, w_out_bf, gain2, mod, layer):
    tm = OUTPROJ_TM
    mod_spec = lambda k: pl.BlockSpec((None, None, None, 1, D_MODEL), lambda i: (layer, _mod_row(i, tm), k, 0, 0))
    a_specs, a_args = _group_specs(merged, tm)
    x_specs, x_args = _group_specs(x, tm)
    return pl.pallas_call(
        functools.partial(_outproj_body, n_a=len(a_args), n_x=len(x_args)),
        grid=(T_ALL // tm,),
        in_specs=a_specs + [
            pl.BlockSpec((None, D_MODEL, D_MODEL), lambda i: (layer, 0, 0), pipeline_mode=pl.Buffered(1)),
        ] + x_specs + [
            mod_spec(2),
            pl.BlockSpec((None, 1, D_MODEL), lambda i: (layer, 0, 0)),
            mod_spec(3),
            mod_spec(4),
        ],
        out_specs=[pl.BlockSpec((tm, D_MODEL), lambda i: (i, 0)), pl.BlockSpec((tm, D_MODEL), lambda i: (i, 0))],
        out_shape=[jax.ShapeDtypeStruct((T_ALL, D_MODEL), F32), jax.ShapeDtypeStruct((T_ALL, D_MODEL), BF16)],
        compiler_params=_cparams(1),
        name="out_projection_norm",
    )(*a_args, w_out_bf, *x_args, mod, gain2.reshape(DEPTH, 1, D_MODEL), mod, mod)


MLP_UP_TM, MLP_UP_TN = 2048, 1024
MLP_UP_CHUNKS = D_FF // MLP_UP_TN
MLP_UP_PIECE = D_MODEL // (T_ALL // MLP_UP_TM)


def _mlp_up_body(h_ref, w_ref, o_ref, wbf_ref):
    _stream_weight_piece(w_ref, wbf_ref, MLP_UP_CHUNKS, MLP_UP_PIECE)
    n = pl.program_id(0)

    @pl.when(n > 0)
    def _():
        hid = jnp.dot(h_ref[...], wbf_ref[(n + 1) % 2], preferred_element_type=F32)
        o_ref[...] = jnp.square(jnp.maximum(hid, 0.0)).astype(BF16)


def mlp_up(h2, w_mlp1, layer):
    tm, tn = MLP_UP_TM, MLP_UP_TN
    tile, chunk, piece, load_chunk = _stream_maps(MLP_UP_CHUNKS, T_ALL // tm)
    return pl.pallas_call(
        _mlp_up_body,
        grid=(MLP_UP_CHUNKS + 1, T_ALL // tm),
        in_specs=[
            pl.BlockSpec((tm, D_MODEL), lambda n, m: (tile(n, m), 0)),
            pl.BlockSpec((None, MLP_UP_PIECE, tn), lambda n, m: (layer, piece(n, m), load_chunk(n))),
        ],
        out_specs=pl.BlockSpec((tm, tn), lambda n, m: (tile(n, m), chunk(n))),
        out_shape=jax.ShapeDtypeStruct((T_ALL, D_FF), BF16),
        scratch_shapes=[pltpu.VMEM((2, D_MODEL, tn), BF16)],
        compiler_params=_cparams(2),
        name="mlp_up",
    )(h2, w_mlp1)


MLP_DOWN_TM, MLP_DOWN_TN = 512, 512
MLP_DOWN_CHUNKS = D_MODEL // MLP_DOWN_TN
MLP_DOWN_PIECE = D_FF // (T_ALL // MLP_DOWN_TM)


def _mlp_down_body(a_ref, w_ref, x_ref, g_ref, o_ref, wbf_ref):
    _stream_weight_piece(w_ref, wbf_ref, MLP_DOWN_CHUNKS, MLP_DOWN_PIECE)
    n = pl.program_id(0)

    @pl.when(n > 0)
    def _():
        o_ref[...] = x_ref[...] + g_ref[...] * jnp.dot(a_ref[...], wbf_ref[(n + 1) % 2],
                                                       preferred_element_type=F32)


def mlp_down(hid, x1, w_mlp2, mod, layer):
    tm, tn = MLP_DOWN_TM, MLP_DOWN_TN
    tile, chunk, piece, load_chunk = _stream_maps(MLP_DOWN_CHUNKS, T_ALL // tm)
    return pl.pallas_call(
        _mlp_down_body,
        grid=(MLP_DOWN_CHUNKS + 1, T_ALL // tm),
        in_specs=[
            pl.BlockSpec((tm, D_FF), lambda n, m: (tile(n, m), 0)),
            pl.BlockSpec((None, MLP_DOWN_PIECE, tn), lambda n, m: (layer, piece(n, m), load_chunk(n))),
            pl.BlockSpec((tm, tn), lambda n, m: (tile(n, m), chunk(n))),
            pl.BlockSpec((None, None, None, 1, tn),
                         lambda n, m: (layer, _mod_row(tile(n, m), tm), 5, 0, chunk(n))),
        ],
        out_specs=pl.BlockSpec((tm, tn), lambda n, m: (tile(n, m), chunk(n))),
        out_shape=jax.ShapeDtypeStruct((T_ALL, D_MODEL), F32),
        scratch_shapes=[pltpu.VMEM((2, D_FF, tn), BF16)],
        compiler_params=_cparams(2),
        name="mlp_down",
    )(hid, w_mlp2, x1, mod)


FINAL_TM = 512


def _final_norm_body(x_ref, g_ref, yp_ref, ys_ref):
    y = _rms(x_ref[...], g_ref[...])
    i = pl.program_id(0)

    @pl.when(i < T_CTX // FINAL_TM)
    def _():
        yp_ref[...] = y

    @pl.when(i >= T_CTX // FINAL_TM)
    def _():
        ys_ref[...] = y


def final_norm(x, gain):
    tm = FINAL_TM
    return pl.pallas_call(
        _final_norm_body,
        grid=(T_ALL // tm,),
        in_specs=[
            pl.BlockSpec((tm, D_MODEL), lambda i: (i, 0)),
            pl.BlockSpec((1, D_MODEL), lambda i: (0, 0)),
        ],
        out_specs=[
            pl.BlockSpec((tm, D_MODEL), lambda i: (_ctx_tile(i, tm), 0)),
            pl.BlockSpec((tm, D_MODEL), lambda i: (_lat_tile(i, tm), 0)),
        ],
        out_shape=[jax.ShapeDtypeStruct((T_CTX, D_MODEL), F32), jax.ShapeDtypeStruct((T_LAT, D_MODEL), F32)],
        compiler_params=_cparams(1),
        name="final_norm",
    )(x, gain.reshape(1, D_MODEL))


ATT_TQ = 256


def _rope(x, cos2, sin2):
    return x * cos2 + pltpu.roll(x, HALF_LANE, 1) * sin2


def _attn_body(*refs, layer, tq, kvs, n_ctx, seq_len, use_sink, band, qk_norm, rope, emit_kv):
    it = iter(refs)
    q_ref, k_ref, v_ref = next(it), next(it), next(it)
    kctx_ref = vctx_ref = sink_ref = qg_ref = kg_ref = cosq_ref = sinq_ref = cosk_ref = sink_k_ref = None
    kprev_ref = vprev_ref = kout_ref = vout_ref = None
    if n_ctx:
        kctx_ref, vctx_ref = next(it), next(it)
    if use_sink:
        sink_ref = next(it)
    if qk_norm:
        qg_ref, kg_ref = next(it), next(it)
    if rope:
        cosq_ref, sinq_ref, cosk_ref, sink_k_ref = next(it), next(it), next(it), next(it)
    if emit_kv and layer:
        kprev_ref, vprev_ref = next(it), next(it)
    o_ref = next(it)
    if emit_kv:
        kout_ref, vout_ref = next(it), next(it)
    kall_ref, vall_ref = next(it), next(it)

    n = pl.program_id(2)
    if band:
        win = tq + 2 * A_WINDOW
        w0 = pl.multiple_of(jnp.clip(n * tq - A_WINDOW, 0, seq_len - win), A_WINDOW)
        slabs = [(0, n_ctx), (n_ctx + w0, win)]
        qpos = n * tq + jax.lax.broadcasted_iota(jnp.int32, (tq, win), 0)
        kpos = w0 + jax.lax.broadcasted_iota(jnp.int32, (tq, win), 1)
        visible = jnp.abs(kpos - qpos) <= A_WINDOW
    else:
        slabs = [(0, n_ctx + seq_len)]
    scale2 = HEAD_DIM ** -0.5 * LOG2E

    for jj in range(kvs):
        j = pl.program_id(1) * kvs + jj
        kv_cols = slice(jj * HEAD_DIM, (jj + 1) * HEAD_DIM)
        kall, vall = kall_ref.at[jj], vall_ref.at[jj]

        @pl.when(n == 0)
        def _():
            k = k_ref[:, kv_cols].astype(F32)
            if qk_norm:
                k = _rms(k, kg_ref[...])
            if emit_kv:
                if layer:
                    @pl.when(j == 0)
                    def _():
                        kout_ref[0:layer] = kprev_ref[...]
                        vout_ref[0:layer] = vprev_ref[...]
                head_rows = pl.ds(j, seq_len, stride=KV_HEADS)
                kout_ref[layer, head_rows, :] = k
                vout_ref[layer, head_rows, :] = v_ref[:, kv_cols].astype(F32)
            if rope:
                k = _rope(k, cosk_ref[...], sink_k_ref[...])
            if n_ctx:
                head_rows = pl.ds(j, n_ctx, stride=KV_HEADS)
                kall[0:n_ctx, :] = kctx_ref[head_rows, :].astype(BF16)
                vall[0:n_ctx, 0:HEAD_DIM] = vctx_ref[head_rows, :].astype(BF16)
            kall[n_ctx:n_ctx + seq_len, :] = k.astype(BF16)
            vall[n_ctx:n_ctx + seq_len, 0:HEAD_DIM] = v_ref[:, kv_cols]
            vall[:, HEAD_DIM:2 * HEAD_DIM] = jnp.ones((n_ctx + seq_len, HEAD_DIM), BF16)

        for g in range(Q_PER_KV):
            q_cols = slice((jj * Q_PER_KV + g) * HEAD_DIM, (jj * Q_PER_KV + g + 1) * HEAD_DIM)
            q = q_ref[:, q_cols].astype(F32)
            if qk_norm:
                q = _rms(q, qg_ref[...])
            if rope:
                q = _rope(q, cosq_ref[...], sinq_ref[...])
            q = (q * scale2).astype(BF16)
            scores = []
            for idx, (k0, rows) in enumerate(slabs):
                s = jax.lax.dot_general(q, kall[pl.ds(k0, rows), :], (((1,), (1,)), ((), ())),
                                        preferred_element_type=F32)
                if band and idx == 1:
                    s = jnp.where(visible, s, -jnp.inf)
                scores.append(s)
            m = functools.reduce(jnp.maximum, [jnp.max(s, axis=-1, keepdims=True) for s in scores])
            if use_sink:
                sk2 = sink_ref[layer * Q_HEADS + j * Q_PER_KV + g] * LOG2E
                m = jnp.maximum(m, sk2)
            acc = None
            for s, (k0, rows) in zip(scores, slabs):
                p = jnp.exp2(s - m).astype(BF16)
                part = jnp.dot(p, vall[pl.ds(k0, rows), :], preferred_element_type=F32)
                acc = part if acc is None else acc + part
            den = acc[:, HEAD_DIM:2 * HEAD_DIM]
            if use_sink:
                den = den + jnp.exp2(sk2 - m)
            o_ref[:, q_cols] = (acc[:, 0:HEAD_DIM] * (1.0 / den)).astype(BF16)


def attention(pr, layer, *, latent, mixer, ctx_k=None, ctx_v=None, sink=None, q_gain=None, k_gain=None,
              rope_tabs=None, kv_prev=None):
    n_seq, seq_len, row0 = (DEC_BATCH, DEC_SEQ, T_CTX) if latent else (BATCH, SEQ, 0)
    n_ctx = PAST_LEN if latent else 0
    base = 0 if mixer == "a" else R1_COLS
    use_sink = mixer == "a"
    qk_norm = mixer == "c"
    band = latent and mixer == "a"
    rope = latent
    emit_kv = not latent
    tq = ATT_TQ
    qblocks = seq_len // tq
    kvs = KV_HEADS
    grid = (n_seq, KV_HEADS // kvs, qblocks)
    qw_kv = kvs * Q_PER_KV * HEAD_DIM
    kw = kvs * HEAD_DIM

    in_specs = [
        pl.BlockSpec((tq, qw_kv), lambda b, j, n: (row0 // tq + b * qblocks + n, (base + OFF_Q) // qw_kv + j)),
        pl.BlockSpec((seq_len, kw), lambda b, j, n: (row0 // seq_len + b, (base + OFF_K) // kw + j)),
        pl.BlockSpec((seq_len, kw), lambda b, j, n: (row0 // seq_len + b, (base + OFF_V) // kw + j)),
    ]
    args = [pr, pr, pr]
    if n_ctx:
        cache_spec = pl.BlockSpec((None, None, PAST_LEN * KV_HEADS, HEAD_DIM), lambda b, j, n: (b, layer, 0, 0))
        in_specs += [cache_spec, cache_spec]
        args += [ctx_k.reshape(DEC_BATCH, DEPTH, PAST_LEN * KV_HEADS, HEAD_DIM),
                 ctx_v.reshape(DEC_BATCH, DEPTH, PAST_LEN * KV_HEADS, HEAD_DIM)]
    if use_sink:
        in_specs.append(pl.BlockSpec(memory_space=pltpu.SMEM))
        args.append(sink.reshape(DEPTH * Q_HEADS))
    if qk_norm:
        gain_spec = pl.BlockSpec((None, 1, HEAD_DIM), lambda b, j, n: (layer, 0, 0))
        in_specs += [gain_spec, gain_spec]
        args += [q_gain.reshape(DEPTH, 1, HEAD_DIM), k_gain.reshape(DEPTH, 1, HEAD_DIM)]
    if rope:
        cos2, sin2 = rope_tabs
        in_specs += [pl.BlockSpec((tq, HEAD_DIM), lambda b, j, n: (n, 0))] * 2
        in_specs += [pl.BlockSpec((seq_len, HEAD_DIM), lambda b, j, n: (0, 0))] * 2
        args += [cos2, sin2, cos2, sin2]
    if emit_kv and layer:
        prev_spec = pl.BlockSpec((None, layer, SEQ * KV_HEADS, HEAD_DIM), lambda b, j, n: (b, 0, 0, 0))
        in_specs += [prev_spec, prev_spec]
        args += list(kv_prev)

    out_specs = [pl.BlockSpec((tq, qw_kv), lambda b, j, n: (b * qblocks + n, j))]
    out_shape = [jax.ShapeDtypeStruct((n_seq * seq_len, QW), BF16)]
    if emit_kv:
        kv_spec = pl.BlockSpec((None, layer + 1, SEQ * KV_HEADS, HEAD_DIM), lambda b, j, n: (b, 0, 0, 0))
        out_specs += [kv_spec, kv_spec]
        out_shape += [jax.ShapeDtypeStruct((BATCH, layer + 1, SEQ * KV_HEADS, HEAD_DIM), F32)] * 2

    return pl.pallas_call(
        functools.partial(_attn_body, layer=layer, tq=tq, kvs=kvs, n_ctx=n_ctx, seq_len=seq_len, use_sink=use_sink, band=band,
                          qk_norm=qk_norm, rope=rope, emit_kv=emit_kv),
        grid=grid,
        in_specs=in_specs,
        out_specs=out_specs,
        out_shape=out_shape,
        scratch_shapes=[pltpu.VMEM((kvs, n_ctx + seq_len, HEAD_DIM), BF16),
                        pltpu.VMEM((kvs, n_ctx + seq_len, 2 * HEAD_DIM), BF16)],
        compiler_params=_cparams(3),
        name=f"attn_{mixer}_{'lat' if latent else 'ctx'}",
    )(*args)


SSD_COLBLK = 512
N_Z_BLK = SSM_INNER // SSD_COLBLK
N_XBC_BLK = CONV_CH // SSD_COLBLK
HALO = 16
GROUP_W = SSM_INNER // SSM_GROUPS
PAIRS = SSM_HEADS // 2
PAIRS_PER_GROUP = PAIRS // SSM_GROUPS


def _softplus(x):
    return jnp.maximum(x, 0.0) + jnp.log1p(jnp.exp(-jnp.abs(x)))


def _split3(x):
    hi = x.astype(BF16)
    r = x - hi.astype(F32)
    mid = r.astype(BF16)
    lo = (r - mid.astype(F32)).astype(BF16)
    return hi, mid, lo


def _ssd_body(*refs, layer, seq_len, has_h0, emit_state):
    it = iter(refs)
    z_refs = [next(it) for _ in range(N_Z_BLK)]
    xbc_refs = [next(it) for _ in range(N_XBC_BLK)]
    dt_ref, convw_ref, convb_ref, alog_ref, dtb_ref, dexp_ref, ng_ref = (next(it) for _ in range(7))
    h0_ref = next(it) if has_h0 else None
    stprev_ref = next(it) if emit_state and layer else None
    y_ref = next(it)
    st_ref = next(it) if emit_state else None
    conv_scr, y_scr, h_scr = next(it), next(it), next(it)

    n_chunks = seq_len // SSD_CHUNK
    q = SSD_CHUNK
    lane = jax.lax.broadcasted_iota(jnp.int32, (q, LANE), 1)
    row = jax.lax.broadcasted_iota(jnp.int32, (q, LANE), 0)
    low_half = lane < HALF_LANE

    for d in range(2):
        for blk in range(SSM_INNER // LANE):
            cols = slice(blk * LANE, (blk + 1) * LANE)
            if has_h0:
                h_scr[d, :, cols] = h0_ref[d, cols, :].T
            else:
                h_scr[d, :, cols] = jnp.zeros((SSM_STATE, LANE), F32)

    def conv_chunk(c, carry):
        r0 = pl.multiple_of(c * q, q)
        prev0 = pl.multiple_of(jnp.maximum(r0 - HALO, 0), HALO)
        next0 = pl.multiple_of(jnp.minimum(r0 + q, seq_len - HALO), HALO)
        has_prev = (r0 > 0).astype(F32)
        has_next = (r0 + q < seq_len).astype(F32)
        for j in range(CONV_CH // LANE):
            src = xbc_refs[j // (SSD_COLBLK // LANE)]
            sc = slice((j % (SSD_COLBLK // LANE)) * LANE, (j % (SSD_COLBLK // LANE) + 1) * LANE)
            cols = slice(j * LANE, (j + 1) * LANE)
            u = src[pl.ds(r0, q), sc].astype(F32)
            prev_row = src[pl.ds(prev0, HALO), sc].astype(F32)[HALO - 1:HALO, :] * has_prev
            next_row = src[pl.ds(next0, HALO), sc].astype(F32)[0:1, :] * has_next
            up = jnp.where(row == 0, prev_row, pltpu.roll(u, 1, 0))
            dn = jnp.where(row == q - 1, next_row, pltpu.roll(u, q - 1, 0))
            hw = 0.5 * convw_ref[:, cols]
            hv = hw[0:1, :] * up + hw[1:2, :] * u + hw[2:3, :] * dn + 0.5 * convb_ref[:, cols]
            act = _silu_from_half(hv)
            conv_scr[pl.ds(r0, q), cols] = act.astype(BF16)
            if j < SSM_INNER // LANE:
                y_scr[pl.ds(r0, q), cols] = dexp_ref[:, cols] * act
        return carry

    jax.lax.fori_loop(0, n_chunks, conv_chunk, 0)

    def scan_chunk(c, d):
        r0 = pl.multiple_of(c * q, q)
        edge = q - 1 if d == 0 else 0
        vis = (row >= lane) if d == 0 else (row <= lane)
        tri = jnp.where(vis, 1.0, 0.0).astype(BF16)
        dt = _softplus(dt_ref[pl.ds(r0, q), :] + dtb_ref[...])
        a = dt * (-jnp.exp(alog_ref[...]))
        a_hi, a_mid, a_lo = _split3(a)
        acum = (jnp.dot(tri, a_hi, preferred_element_type=F32) + jnp.dot(tri, a_mid, preferred_element_type=F32)
                + jnp.dot(tri, a_lo, preferred_element_type=F32))
        acum2 = acum * LOG2E
        acum_t = acum.T
        dt_t = dt.T
        row_t2 = (acum_t - jnp.log(dt_t)) * LOG2E
        w_t = dt_t * jnp.exp(acum_t[:, edge:edge + 1] - acum_t)
        for g in range(SSM_GROUPS):
            b_g = conv_scr[pl.ds(r0, q), SSM_INNER + g * SSM_STATE:SSM_INNER + (g + 1) * SSM_STATE]
            c_lo = SSM_INNER + SSM_GROUPS * SSM_STATE + g * SSM_STATE
            c_g = conv_scr[pl.ds(r0, q), c_lo:c_lo + SSM_STATE]
            cb = jax.lax.dot_general(c_g, b_g, (((1,), (1,)), ((), ())),
                                     preferred_element_type=F32).astype(BF16)
            b_t = b_g.astype(F32).T.astype(BF16)
            h_g = h_scr[d, :, g * GROUP_W:(g + 1) * GROUP_W].astype(BF16)
            y_off = jnp.dot(c_g, h_g, preferred_element_type=F32)
            for pp in range(PAIRS_PER_GROUP):
                p = g * PAIRS_PER_GROUP + pp
                cols = slice(p * LANE, (p + 1) * LANE)
                lhs_top, lhs_bot, col_of = [], [], []
                for h in (2 * p, 2 * p + 1):
                    ell = d * SSM_HEADS + h
                    col = jnp.broadcast_to(acum2[:, ell:ell + 1], (q, LANE))
                    seg = jnp.exp2(jnp.where(vis, col - row_t2[ell:ell + 1, :], -jnp.inf))
                    lhs_top.append(cb * seg.astype(BF16))
                    lhs_bot.append(b_t * w_t[ell:ell + 1, :].astype(BF16))
                    col_of.append(col)
                lhs = jnp.concatenate([jnp.concatenate(lhs_top, axis=1), jnp.concatenate(lhs_bot, axis=1)],
                                      axis=0)
                xp = conv_scr[pl.ds(r0, q), cols]
                zero = jnp.zeros_like(xp)
                rhs = jnp.concatenate([jnp.where(low_half, xp, zero), jnp.where(low_half, zero, xp)], axis=0)
                res = jnp.dot(lhs, rhs, preferred_element_type=F32)
                factor = jnp.exp2(jnp.where(low_half, col_of[0], col_of[1]))
                y_scr[pl.ds(r0, q), cols] += res[0:q, :] + y_off[:, pp * LANE:(pp + 1) * LANE] * factor
                h_scr[d, :, cols] = h_scr[d, :, cols] * factor[edge:edge + 1, :] + res[q:2 * q, :]

    def scan_step(i, carry):
        scan_chunk(i, 0)
        scan_chunk(n_chunks - 1 - i, 1)
        return carry

    jax.lax.fori_loop(0, n_chunks, scan_step, 0)

    if emit_state:
        if layer:
            st_ref[0:layer] = stprev_ref[...]
        for d in range(2):
            for blk in range(SSM_INNER // LANE):
                cols = slice(blk * LANE, (blk + 1) * LANE)
                st_ref[layer, d, cols, :] = h_scr[d, :, cols].T

    def finish_chunk(c, carry):
        r0 = pl.multiple_of(c * q, q)
        gated = []
        for k in range(N_Z_BLK):
            cols = slice(k * SSD_COLBLK, (k + 1) * SSD_COLBLK)
            half_z = 0.5 * z_refs[k][pl.ds(r0, q), :].astype(F32)
            gated.append(y_scr[pl.ds(r0, q), cols] * _silu_from_half(half_z))
        ssq = sum(jnp.sum(gk * gk, axis=-1, keepdims=True) for gk in gated)
        inv = jax.lax.rsqrt(ssq * (1.0 / SSM_INNER) + EPS)
        for k in range(N_Z_BLK):
            cols = slice(k * SSD_COLBLK, (k + 1) * SSD_COLBLK)
            y_ref[pl.ds(r0, q), cols] = (gated[k] * inv * ng_ref[:, cols]).astype(BF16)
        return carry

    jax.lax.fori_loop(0, n_chunks, finish_chunk, 0)


def ssd_mixer(pr, dt_raw, layer, *, latent, conv_w, conv_b, a_log_row, dt_bias_row, d_row, norm_gain,
              h0=None, state_prev=None):
    n_seq, seq_len, row0 = (DEC_BATCH, DEC_SEQ, T_CTX) if latent else (BATCH, SEQ, 0)
    has_h0 = latent
    emit_state = not latent
    seq_blk = row0 // seq_len

    def col_spec(first, k):
        return pl.BlockSpec((seq_len, SSD_COLBLK), lambda b: (seq_blk + b, first // SSD_COLBLK + k))

    in_specs = [col_spec(OFF_Z, k) for k in range(N_Z_BLK)] + [col_spec(OFF_XBC, k) for k in range(N_XBC_BLK)]
    args = [pr] * (N_Z_BLK + N_XBC_BLK)
    in_specs += [
        pl.BlockSpec((seq_len, LANE), lambda b: (seq_blk + b, 0)),
        pl.BlockSpec((None, CONV_K, CONV_CH), lambda b: (layer, 0, 0)),
        pl.BlockSpec((None, 1, CONV_CH), lambda b: (layer, 0, 0)),
        pl.BlockSpec((None, 1, LANE), lambda b: (layer, 0, 0)),
        pl.BlockSpec((None, 1, LANE), lambda b: (layer, 0, 0)),
        pl.BlockSpec((None, 1, SSM_INNER), lambda b: (layer, 0, 0)),
        pl.BlockSpec((None, 1, SSM_INNER), lambda b: (layer, 0, 0)),
    ]
    args += [dt_raw, conv_w, conv_b.reshape(DEPTH, 1, CONV_CH), a_log_row, dt_bias_row, d_row,
             norm_gain.reshape(DEPTH, 1, SSM_INNER)]
    if has_h0:
        in_specs.append(pl.BlockSpec((None, None, 2, SSM_INNER, SSM_STATE), lambda b: (b, layer, 0, 0, 0)))
        args.append(h0.reshape(DEC_BATCH, DEPTH, 2, SSM_INNER, SSM_STATE))
    if emit_state and layer:
        in_specs.append(pl.BlockSpec((None, layer, 2, SSM_INNER, SSM_STATE), lambda b: (b, 0, 0, 0, 0)))
        args.append(state_prev)

    out_specs = [pl.BlockSpec((seq_len, SSM_INNER), lambda b: (b, 0))]
    out_shape = [jax.ShapeDtypeStruct((n_seq * seq_len, SSM_INNER), BF16)]
    if emit_state:
        out_specs.append(pl.BlockSpec((None, layer + 1, 2, SSM_INNER, SSM_STATE), lambda b: (b, 0, 0, 0, 0)))
        out_shape.append(jax.ShapeDtypeStruct((BATCH, layer + 1, 2, SSM_INNER, SSM_STATE), F32))

    return pl.pallas_call(
        functools.partial(_ssd_body, layer=layer, seq_len=seq_len, has_h0=has_h0, emit_state=emit_state),
        grid=(n_seq,),
        in_specs=in_specs,
        out_specs=out_specs,
        out_shape=out_shape,
        scratch_shapes=[
            pltpu.VMEM((seq_len, CONV_CH), BF16),
            pltpu.VMEM((seq_len, SSM_INNER), F32),
            pltpu.VMEM((2, SSM_STATE, SSM_INNER), F32),
        ],
        compiler_params=_cparams(1),
        name=f"ssd_{'lat' if latent else 'ctx'}",
    )(*args)


def _rope_tables():
    rows = DEC_SEQ // GRID_W
    r = jnp.repeat(jnp.arange(rows, dtype=F32), GRID_W)
    c = jnp.tile(jnp.arange(GRID_W, dtype=F32), rows)
    n_freq = HEAD_DIM // 4
    inv_freq = ROPE_THETA ** (-jnp.arange(n_freq, dtype=F32) / n_freq)
    ang = jnp.concatenate([r[:, None] * inv_freq, c[:, None] * inv_freq], axis=-1)
    cos, sin = jnp.cos(ang), jnp.sin(ang)
    return jnp.concatenate([cos, cos], axis=-1), jnp.concatenate([-sin, sin], axis=-1)


def kernel(x_prompt, x_sample, cache_a_k, cache_a_v, cache_c_k, cache_c_v, state_ssm, c, c_ctx, norm1_g, w_ada, b_ada, w_in, a_sink, conv_w, conv_b, ssm_a_log, ssm_dt_bias, ssm_d, ssm_norm_g, c_q_norm, c_k_norm, w_oa, w_ob, w_oc, w_out, norm2_g, w_mlp1, w_mlp2, final_norm_g):
    x = (x_prompt.reshape(T_CTX, D_MODEL), x_sample.reshape(T_LAT, D_MODEL))
    cond = jnp.concatenate([c, c_ctx[None, :], jnp.zeros((MOD_ROWS - DEC_BATCH - 1, D_MODEL), F32)], axis=0)
    mod = ada_modulation(cond, w_ada, b_ada).reshape(DEPTH, MOD_ROWS, N_MOD, 1, D_MODEL)
    rope_tabs = _rope_tables()
    lane_pad = jnp.zeros((DEPTH, LANE - 2 * SSM_HEADS), F32)
    a_log_row = jnp.concatenate([ssm_a_log.reshape(DEPTH, 2 * SSM_HEADS), lane_pad], axis=1).reshape(DEPTH, 1, LANE)
    dt_bias_row = jnp.concatenate([ssm_dt_bias.reshape(DEPTH, 2 * SSM_HEADS), lane_pad], axis=1).reshape(DEPTH, 1, LANE)
    d_row = jnp.repeat(ssm_d, SSM_HEAD_DIM, axis=1).reshape(DEPTH, 1, SSM_INNER)
    w_in_t = jnp.swapaxes(w_in, 1, 2)
    w_out_bf = w_out.astype(BF16)

    kv_a = kv_c = state = None
    for layer in range(DEPTH):
        h, dt_raw = prenorm(x, norm1_g, mod, layer, w_in_t)
        pr = in_projection(h, w_in_t, layer)

        ya_c, *kv_a = attention(pr, layer, latent=False, mixer="a", sink=a_sink, kv_prev=kv_a)
        (ya_l,) = attention(pr, layer, latent=True, mixer="a", sink=a_sink, ctx_k=cache_a_k, ctx_v=cache_a_v,
                            rope_tabs=rope_tabs)
        yc_c, *kv_c = attention(pr, layer, latent=False, mixer="c", q_gain=c_q_norm, k_gain=c_k_norm, kv_prev=kv_c)
        (yc_l,) = attention(pr, layer, latent=True, mixer="c", q_gain=c_q_norm, k_gain=c_k_norm, ctx_k=cache_c_k,
                            ctx_v=cache_c_v, rope_tabs=rope_tabs)
        ssd_args = dict(conv_w=conv_w, conv_b=conv_b, a_log_row=a_log_row, dt_bias_row=dt_bias_row, d_row=d_row,
                        norm_gain=ssm_norm_g)
        yb_c, state = ssd_mixer(pr, dt_raw, layer, latent=False, state_prev=state, **ssd_args)
        (yb_l,) = ssd_mixer(pr, dt_raw, layer, latent=True, h0=state_ssm, **ssd_args)

        merged = (merge_branches(ya_c, yb_c, yc_c, pr, w_oa, w_ob, w_oc, layer, latent=False),
                  merge_branches(ya_l, yb_l, yc_l, pr, w_oa, w_ob, w_oc, layer, latent=True))
        x1, h2 = out_projection_norm(merged, x, w_out_bf, norm2_g, mod, layer)
        x = mlp_down(mlp_up(h2, w_mlp1, layer), x1, w_mlp2, mod, layer)

    y_prompt, y_sample = final_norm(x, final_norm_g)
    kv5 = (BATCH, DEPTH, SEQ, KV_HEADS, HEAD_DIM)
    return (y_prompt.reshape(BATCH, SEQ, D_MODEL), y_sample.reshape(DEC_BATCH, DEC_SEQ, D_MODEL),
            kv_a[0].reshape(kv5), kv_a[1].reshape(kv5), kv_c[0].reshape(kv5), kv_c[1].reshape(kv5),
            state.reshape(BATCH, DEPTH, 2, SSM_HEADS, SSM_HEAD_DIM, SSM_STATE))
```

```python
import functools
import math

import jax
import jax.numpy as jnp
from jax.experimental import pallas as pl
from jax.experimental.pallas import tpu as pltpu

F32 = jnp.float32
BF16 = jnp.bfloat16

D_MODEL = 2048
BATCH = 16
SEQ = 256
DEPTH = 2
DEC_BATCH = 4
DEC_SEQ = 1024
PAST_LEN = 256
GRID_W = 64
HEAD_DIM = 128
ROPE_THETA = 10000.0
EPS = 1e-6
Q_HEADS = 8
KV_HEADS = 2
Q_PER_KV = Q_HEADS // KV_HEADS
A_WINDOW = 128
SSM_HEADS = 32
SSM_HEAD_DIM = 64
SSM_GROUPS = 2
SSM_STATE = 128
CONV_K = 3
D_FF = 4 * D_MODEL
QW = Q_HEADS * HEAD_DIM
KVW = KV_HEADS * HEAD_DIM
SSM_INNER = SSM_HEADS * SSM_HEAD_DIM
CONV_CH = SSM_INNER + 2 * SSM_GROUPS * SSM_STATE
N_MOD = 6

T_CTX = BATCH * SEQ
T_LAT = DEC_BATCH * DEC_SEQ
T_ALL = T_CTX + T_LAT
MOD_ROWS = 8
CTX_MOD_ROW = DEC_BATCH

R1_COLS = QW + 2 * KVW + SSM_INNER + CONV_CH
DT_COLS = 2 * SSM_HEADS
R2_COLS = QW + 2 * KVW + 3 * D_MODEL
PR_COLS = R1_COLS + R2_COLS
OFF_Q, OFF_K, OFF_V = 0, QW, QW + KVW
OFF_Z = QW + 2 * KVW
OFF_XBC = OFF_Z + SSM_INNER
OFF_GATES = R1_COLS + QW + 2 * KVW

LANE = 128
HALF_LANE = LANE // 2
SSD_CHUNK = 128
VMEM_LIMIT = 56 * 1024 * 1024
LOG2E = math.log2(math.e)


def _cparams(n_axes, vmem=VMEM_LIMIT):
    return pltpu.CompilerParams(dimension_semantics=("arbitrary",) * n_axes, vmem_limit_bytes=vmem)


def _mod_row(tile, tm):
    n_ctx_tiles = T_CTX // tm
    return jnp.where(tile < n_ctx_tiles, CTX_MOD_ROW, (tile - n_ctx_tiles) // (DEC_SEQ // tm))


def _ctx_tile(tile, tm):
    return jnp.minimum(tile, T_CTX // tm - 1)


def _lat_tile(tile, tm):
    return jnp.maximum(tile - T_CTX // tm, 0)


def _sigmoid(x):
    return 0.5 * jnp.tanh(0.5 * x) + 0.5


def _silu(x):
    return x * _sigmoid(x)


def _silu_from_half(half_x):
    return half_x + half_x * jnp.tanh(half_x)


def _rms(x, gain):
    return x * jax.lax.rsqrt(jnp.mean(x * x, axis=-1, keepdims=True) + EPS) * gain


def _ada_body(cond_ref, w_ref, b_ref, o_ref):
    a = _silu(cond_ref[...]).astype(BF16)
    o_ref[...] = jnp.dot(a, w_ref[...].astype(BF16), preferred_element_type=F32) + b_ref[...]


def ada_modulation(cond, w_ada, b_ada):
    tn = 1024
    n_out = N_MOD * D_MODEL
    return pl.pallas_call(
        _ada_body,
        grid=(DEPTH, n_out // tn),
        in_specs=[
            pl.BlockSpec((MOD_ROWS, D_MODEL), lambda l, n: (0, 0)),
            pl.BlockSpec((None, D_MODEL, tn), lambda l, n: (l, 0, n)),
            pl.BlockSpec((None, 1, tn), lambda l, n: (l, 0, n)),
        ],
        out_specs=pl.BlockSpec((None, MOD_ROWS, tn), lambda l, n: (l, 0, n)),
        out_shape=jax.ShapeDtypeStruct((DEPTH, MOD_ROWS, n_out), F32),
        compiler_params=_cparams(2),
        name="ada_modulation",
    )(cond, w_ada, b_ada.reshape(DEPTH, 1, n_out))


PRENORM_TM = 1024


def _pick_group(refs, tm):
    if len(refs) == 1:
        return refs[0][...]
    return jnp.where(pl.program_id(0) < T_CTX // tm, refs[0][...], refs[1][...])


def _group_specs(x, tm):
    if isinstance(x, tuple):
        return [pl.BlockSpec((tm, D_MODEL), lambda i: (_ctx_tile(i, tm), 0)),
                pl.BlockSpec((tm, D_MODEL), lambda i: (_lat_tile(i, tm), 0))], list(x)
    return [pl.BlockSpec((tm, D_MODEL), lambda i: (i, 0))], [x]


def _prenorm_body(*refs, n_x):
    x_refs, (g_ref, sh_ref, sc_ref, wdt_ref, h_ref, dt_ref) = refs[:n_x], refs[n_x:]
    x = _pick_group(x_refs, PRENORM_TM)
    h = (_rms(x, g_ref[...]) * (1.0 + sc_ref[...]) + sh_ref[...]).astype(BF16)
    dt_ref[...] = jax.lax.dot_general(h, wdt_ref[...].astype(BF16), (((1,), (1,)), ((), ())),
                                      preferred_element_type=F32)
    h_ref[...] = h


def prenorm(x, gain, mod, layer, w_in_t):
    tm = PRENORM_TM
    mod_spec = lambda k: pl.BlockSpec((None, None, None, 1, D_MODEL),
                                      lambda i: (layer, _mod_row(i, tm), k, 0, 0))
    x_specs, x_args = _group_specs(x, tm)
    return pl.pallas_call(
        functools.partial(_prenorm_body, n_x=len(x_args)),
        grid=(T_ALL // tm,),
        in_specs=x_specs + [
            pl.BlockSpec((None, 1, D_MODEL), lambda i: (layer, 0, 0)),
            mod_spec(0),
            mod_spec(1),
            pl.BlockSpec((None, LANE, D_MODEL), lambda i: (layer, R1_COLS // LANE, 0)),
        ],
        out_specs=[pl.BlockSpec((tm, D_MODEL), lambda i: (i, 0)), pl.BlockSpec((tm, LANE), lambda i: (i, 0))],
        out_shape=[jax.ShapeDtypeStruct((T_ALL, D_MODEL), BF16), jax.ShapeDtypeStruct((T_ALL, LANE), F32)],
        compiler_params=_cparams(1),
        name="prenorm_dt",
    )(*x_args, gain.reshape(DEPTH, 1, D_MODEL), mod, mod, w_in_t)


PROJ_TM = 1024
PROJ_TN = 1536
assert OFF_GATES % PROJ_TN == 0 and R1_COLS % PROJ_TN == 0


def _stream_weight_piece(w_ref, wbf_ref, n_chunks, piece):
    n, m = pl.program_id(0), pl.program_id(1)

    @pl.when(n < n_chunks)
    def _():
        wbf_ref[n % 2, pl.ds(pl.multiple_of(m * piece, piece), piece), :] = w_ref[...].astype(BF16)


def _stream_maps(n_chunks, m_tiles):
    tile = lambda n, m: jnp.where(n == 0, 0, m)
    chunk = lambda n: jnp.maximum(n - 1, 0)
    piece = lambda n, m: jnp.where(n < n_chunks, m, m_tiles - 1)
    load_chunk = lambda n: jnp.minimum(n, n_chunks - 1)
    return tile, chunk, piece, load_chunk


PROJ_CHUNKS = PR_COLS // PROJ_TN
PROJ_PIECE = PROJ_TN // (T_ALL // PROJ_TM)


def _proj_body(h_ref, w_ref, o_ref, wbf_ref):
    _stream_weight_piece(w_ref, wbf_ref, PROJ_CHUNKS, PROJ_PIECE)
    n = pl.program_id(0)

    def project():
        return jax.lax.dot_general(h_ref[...], wbf_ref[(n + 1) % 2], (((1,), (1,)), ((), ())),
                                   preferred_element_type=F32)

    @pl.when((n > 0) & (n - 1 < OFF_GATES // PROJ_TN))
    def _():
        o_ref[...] = project().astype(o_ref.dtype)

    @pl.when(n - 1 >= OFF_GATES // PROJ_TN)
    def _():
        o_ref[...] = _sigmoid(project()).astype(o_ref.dtype)


def in_projection(h, w_in_t, layer):
    tm, tn = PROJ_TM, PROJ_TN
    tile, chunk, piece, load_chunk = _stream_maps(PROJ_CHUNKS, T_ALL // tm)
    first_row = lambda c: c * tn + jnp.where(c >= R1_COLS // tn, DT_COLS, 0)
    piece_row = lambda n, m: pl.multiple_of(first_row(load_chunk(n)) + piece(n, m) * PROJ_PIECE, DT_COLS)
    return pl.pallas_call(
        _proj_body,
        grid=(PROJ_CHUNKS + 1, T_ALL // tm),
        in_specs=[
            pl.BlockSpec((tm, D_MODEL), lambda n, m: (tile(n, m), 0)),
            pl.BlockSpec((None, pl.Element(PROJ_PIECE), pl.Element(D_MODEL)), lambda n, m: (layer, piece_row(n, m), 0)),
        ],
        out_specs=pl.BlockSpec((tm, tn), lambda n, m: (tile(n, m), chunk(n))),
        out_shape=jax.ShapeDtypeStruct((T_ALL, PR_COLS), BF16),
        scratch_shapes=[pltpu.VMEM((2, tn, D_MODEL), BF16)],
        compiler_params=_cparams(2),
        name="in_projection",
    )(h, w_in_t)


MERGE_TM = 1024
MERGE_TN = 512
assert OFF_GATES % MERGE_TN == 0 and D_MODEL % MERGE_TN == 0
MERGE_K = QW + SSM_INNER + QW


def _merge_body(ya_ref, yb_ref, yc_ref, ga_ref, gb_ref, gc_ref, wa_ref, wb_ref, wc_ref, o_ref, wbf_ref):
    @pl.when(pl.program_id(1) == 0)
    def _():
        wbf_ref[0:QW, :] = wa_ref[...].astype(BF16)
        wbf_ref[QW:QW + SSM_INNER, :] = wb_ref[...].astype(BF16)
        wbf_ref[QW + SSM_INNER:MERGE_K, :] = wc_ref[...].astype(BF16)

    br_a = jnp.dot(ya_ref[...], wbf_ref[0:QW, :], preferred_element_type=F32)
    br_b = jnp.dot(yb_ref[...], wbf_ref[QW:QW + SSM_INNER, :], preferred_element_type=F32)
    br_c = jnp.dot(yc_ref[...], wbf_ref[QW + SSM_INNER:MERGE_K, :], preferred_element_type=F32)
    merged = ga_ref[...].astype(F32) * br_a + gb_ref[...].astype(F32) * br_b + gc_ref[...].astype(F32) * br_c
    o_ref[...] = merged.astype(BF16)


def merge_branches(ya, yb, yc, pr, w_oa, w_ob, w_oc, layer, *, latent):
    tm, tn = MERGE_TM, MERGE_TN
    rows = T_LAT if latent else T_CTX
    tile0 = (T_CTX if latent else 0) // tm
    gate_spec = lambda k: pl.BlockSpec((tm, tn), lambda n, m: (tile0 + m, (OFF_GATES + k * D_MODEL) // tn + n))
    w_spec = lambda k_rows: pl.BlockSpec((None, k_rows, tn), lambda n, m: (layer, 0, n))
    y_spec = lambda cols: pl.BlockSpec((tm, cols), lambda n, m: (m, 0))
    return pl.pallas_call(
        _merge_body,
        grid=(D_MODEL // tn, rows // tm),
        in_specs=[
            y_spec(QW), y_spec(SSM_INNER), y_spec(QW),
            gate_spec(0), gate_spec(1), gate_spec(2),
            w_spec(QW), w_spec(SSM_INNER), w_spec(QW),
        ],
        out_specs=pl.BlockSpec((tm, tn), lambda n, m: (m, n)),
        out_shape=jax.ShapeDtypeStruct((rows, D_MODEL), BF16),
        scratch_shapes=[pltpu.VMEM((MERGE_K, tn), BF16)],
        compiler_params=_cparams(2),
        name=f"merge_{'lat' if latent else 'ctx'}",
    )(ya, yb, yc, pr, pr, pr, w_oa, w_ob, w_oc)


OUTPROJ_TM = 512


def _outproj_body(*refs, n_a, n_x):
    a_refs, refs = refs[:n_a], refs[n_a:]
    w_ref, refs = refs[0], refs[1:]
    x_refs, refs = refs[:n_x], refs[n_x:]
    g1_ref, ng_ref, sh_ref, sc_ref, x1_ref, h2_ref = refs
    a = _pick_group(a_refs, OUTPROJ_TM)
    x = _pick_group(x_refs, OUTPROJ_TM)
    x1 = x + g1_ref[...] * jnp.dot(a, w_ref[...], preferred_element_type=F32)
    x1_ref[...] = x1
    h2_ref[...] = (_rms(x1, ng_ref[...]) * (1.0 + sc_ref[...]) + sh_ref[...]).astype(BF16)


def out_projection_norm(merged, x, w_out_bf, gain2, mod, layer):
    tm = OUTPROJ_TM
    mod_spec = lambda k: pl.BlockSpec((None, None, None, 1, D_MODEL), lambda i: (layer, _mod_row(i, tm), k, 0, 0))
    a_specs, a_args = _group_specs(merged, tm)
    x_specs, x_args = _group_specs(x, tm)
    return pl.pallas_call(
        functools.partial(_outproj_body, n_a=len(a_args), n_x=len(x_args)),
        grid=(T_ALL // tm,),
        in_specs=a_specs + [
            pl.BlockSpec((None, D_MODEL, D_MODEL), lambda i: (layer, 0, 0), pipeline_mode=pl.Buffered(1)),
        ] + x_specs + [
            mod_spec(2),
            pl.BlockSpec((None, 1, D_MODEL), lambda i: (layer, 0, 0)),
            mod_spec(3),
            mod_spec(4),
        ],
        out_specs=[pl.BlockSpec((tm, D_MODEL), lambda i: (i, 0)), pl.BlockSpec((tm, D_MODEL), lambda i: (i, 0))],
        out_shape=[jax.ShapeDtypeStruct((T_ALL, D_MODEL), F32), jax.ShapeDtypeStruct((T_ALL, D_MODEL), BF16)],
        compiler_params=_cparams(1),
        name="out_projection_norm",
    )(*a_args, w_out_bf, *x_args, mod, gain2.reshape(DEPTH, 1, D_MODEL), mod, mod)


MLP_UP_TM, MLP_UP_TN = 2048, 1024
MLP_UP_CHUNKS = D_FF // MLP_UP_TN
MLP_UP_PIECE = D_MODEL // (T_ALL // MLP_UP_TM)


def _mlp_up_body(h_ref, w_ref, o_ref, wbf_ref):
    _stream_weight_piece(w_ref, wbf_ref, MLP_UP_CHUNKS, MLP_UP_PIECE)
    n = pl.program_id(0)

    @pl.when(n > 0)
    def _():
        hid = jnp.dot(h_ref[...], wbf_ref[(n + 1) % 2], preferred_element_type=F32)
        o_ref[...] = jnp.square(jnp.maximum(hid, 0.0)).astype(BF16)


def mlp_up(h2, w_mlp1, layer):
    tm, tn = MLP_UP_TM, MLP_UP_TN
    tile, chunk, piece, load_chunk = _stream_maps(MLP_UP_CHUNKS, T_ALL // tm)
    return pl.pallas_call(
        _mlp_up_body,
        grid=(MLP_UP_CHUNKS + 1, T_ALL // tm),
        in_specs=[
            pl.BlockSpec((tm, D_MODEL), lambda n, m: (tile(n, m), 0)),
            pl.BlockSpec((None, MLP_UP_PIECE, tn), lambda n, m: (layer, piece(n, m), load_chunk(n))),
        ],
        out_specs=pl.BlockSpec((tm, tn), lambda n, m: (tile(n, m), chunk(n))),
        out_shape=jax.ShapeDtypeStruct((T_ALL, D_FF), BF16),
        scratch_shapes=[pltpu.VMEM((2, D_MODEL, tn), BF16)],
        compiler_params=_cparams(2),
        name="mlp_up",
    )(h2, w_mlp1)


MLP_DOWN_TM, MLP_DOWN_TN = 512, 512
MLP_DOWN_CHUNKS = D_MODEL // MLP_DOWN_TN
MLP_DOWN_PIECE = D_FF // (T_ALL // MLP_DOWN_TM)


def _mlp_down_body(a_ref, w_ref, x_ref, g_ref, o_ref, wbf_ref):
    _stream_weight_piece(w_ref, wbf_ref, MLP_DOWN_CHUNKS, MLP_DOWN_PIECE)
    n = pl.program_id(0)

    @pl.when(n > 0)
    def _():
        o_ref[...] = x_ref[...] + g_ref[...] * jnp.dot(a_ref[...], wbf_ref[(n + 1) % 2],
                                                       preferred_element_type=F32)


def mlp_down(hid, x1, w_mlp2, mod, layer):
    tm, tn = MLP_DOWN_TM, MLP_DOWN_TN
    tile, chunk, piece, load_chunk = _stream_maps(MLP_DOWN_CHUNKS, T_ALL // tm)
    return pl.pallas_call(
        _mlp_down_body,
        grid=(MLP_DOWN_CHUNKS + 1, T_ALL // tm),
        in_specs=[
            pl.BlockSpec((tm, D_FF), lambda n, m: (tile(n, m), 0)),
            pl.BlockSpec((None, MLP_DOWN_PIECE, tn), lambda n, m: (layer, piece(n, m), load_chunk(n))),
            pl.BlockSpec((tm, tn), lambda n, m: (tile(n, m), chunk(n))),
            pl.BlockSpec((None, None, None, 1, tn),
                         lambda n, m: (layer, _mod_row(tile(n, m), tm), 5, 0, chunk(n))),
        ],
        out_specs=pl.BlockSpec((tm, tn), lambda n, m: (tile(n, m), chunk(n))),
        out_shape=jax.ShapeDtypeStruct((T_ALL, D_MODEL), F32),
        scratch_shapes=[pltpu.VMEM((2, D_FF, tn), BF16)],
        compiler_params=_cparams(2),
        name="mlp_down",
    )(hid, w_mlp2, x1, mod)


FINAL_TM = 1024


def _final_norm_body(x_ref, g_ref, yp_ref, ys_ref):
    y = _rms(x_ref[...], g_ref[...])
    i = pl.program_id(0)

    @pl.when(i < T_CTX // FINAL_TM)
    def _():
        yp_ref[...] = y

    @pl.when(i >= T_CTX // FINAL_TM)
    def _():
        ys_ref[...] = y


def final_norm(x, gain):
    tm = FINAL_TM
    return pl.pallas_call(
        _final_norm_body,
        grid=(T_ALL // tm,),
        in_specs=[
            pl.BlockSpec((tm, D_MODEL), lambda i: (i, 0)),
            pl.BlockSpec((1, D_MODEL), lambda i: (0, 0)),
        ],
        out_specs=[
            pl.BlockSpec((tm, D_MODEL), lambda i: (_ctx_tile(i, tm), 0)),
            pl.BlockSpec((tm, D_MODEL), lambda i: (_lat_tile(i, tm), 0)),
        ],
        out_shape=[jax.ShapeDtypeStruct((T_CTX, D_MODEL), F32), jax.ShapeDtypeStruct((T_LAT, D_MODEL), F32)],
        compiler_params=_cparams(1),
        name="final_norm",
    )(x, gain.reshape(1, D_MODEL))


ATT_TQ = 256


def _rope(x, cos2, sin2):
    return x * cos2 + pltpu.roll(x, HALF_LANE, 1) * sin2


def _attn_body(*refs, layer, tq, kvs, n_ctx, seq_len, use_sink, band, qk_norm, rope, emit_kv):
    it = iter(refs)
    q_ref, k_ref, v_ref = next(it), next(it), next(it)
    kctx_ref = vctx_ref = sink_ref = qg_ref = kg_ref = cosq_ref = sinq_ref = cosk_ref = sink_k_ref = None
    kprev_ref = vprev_ref = kout_ref = vout_ref = None
    if n_ctx:
        kctx_ref, vctx_ref = next(it), next(it)
    if use_sink:
        sink_ref = next(it)
    if qk_norm:
        qg_ref, kg_ref = next(it), next(it)
    if rope:
        cosq_ref, sinq_ref, cosk_ref, sink_k_ref = next(it), next(it), next(it), next(it)
    if emit_kv and layer:
        kprev_ref, vprev_ref = next(it), next(it)
    o_ref = next(it)
    if emit_kv:
        kout_ref, vout_ref = next(it), next(it)
    kall_ref, vall_ref = next(it), next(it)

    n = pl.program_id(2)
    if band:
        win = tq + 2 * A_WINDOW
        w0 = pl.multiple_of(jnp.clip(n * tq - A_WINDOW, 0, seq_len - win), A_WINDOW)
        slabs = [(0, n_ctx), (n_ctx + w0, win)]
        qpos = n * tq + jax.lax.broadcasted_iota(jnp.int32, (tq, win), 0)
        kpos = w0 + jax.lax.broadcasted_iota(jnp.int32, (tq, win), 1)
        visible = jnp.abs(kpos - qpos) <= A_WINDOW
    else:
        slabs = [(0, n_ctx + seq_len)]
    scale2 = HEAD_DIM ** -0.5 * LOG2E

    for jj in range(kvs):
        j = pl.program_id(1) * kvs + jj
        kv_cols = slice(jj * HEAD_DIM, (jj + 1) * HEAD_DIM)
        kall, vall = kall_ref.at[jj], vall_ref.at[jj]

        @pl.when(n == 0)
        def _():
            k = k_ref[:, kv_cols].astype(F32)
            if qk_norm:
                k = _rms(k, kg_ref[...])
            if emit_kv:
                if layer:
                    @pl.when(j == 0)
                    def _():
                        kout_ref[0:layer] = kprev_ref[...]
                        vout_ref[0:layer] = vprev_ref[...]
                head_rows = pl.ds(j, seq_len, stride=KV_HEADS)
                kout_ref[layer, head_rows, :] = k
                vout_ref[layer, head_rows, :] = v_ref[:, kv_cols].astype(F32)
            if rope:
                k = _rope(k, cosk_ref[...], sink_k_ref[...])
            if n_ctx:
                head_rows = pl.ds(j, n_ctx, stride=KV_HEADS)
                kall[0:n_ctx, :] = kctx_ref[head_rows, :].astype(BF16)
                vall[0:n_ctx, 0:HEAD_DIM] = vctx_ref[head_rows, :].astype(BF16)
            kall[n_ctx:n_ctx + seq_len, :] = k.astype(BF16)
            vall[n_ctx:n_ctx + seq_len, 0:HEAD_DIM] = v_ref[:, kv_cols]
            vall[:, HEAD_DIM:2 * HEAD_DIM] = jnp.ones((n_ctx + seq_len, HEAD_DIM), BF16)

        for g in range(Q_PER_KV):
            q_cols = slice((jj * Q_PER_KV + g) * HEAD_DIM, (jj * Q_PER_KV + g + 1) * HEAD_DIM)
            q = q_ref[:, q_cols].astype(F32)
            if qk_norm:
                q = _rms(q, qg_ref[...])
            if rope:
                q = _rope(q, cosq_ref[...], sinq_ref[...])
            q = (q * scale2).astype(BF16)
            scores = []
            for idx, (k0, rows) in enumerate(slabs):
                s = jax.lax.dot_general(q, kall[pl.ds(k0, rows), :], (((1,), (1,)), ((), ())),
                                        preferred_element_type=F32)
                if band and idx == 1:
                    s = jnp.where(visible, s, -jnp.inf)
                scores.append(s)
            m = functools.reduce(jnp.maximum, [jnp.max(s, axis=-1, keepdims=True) for s in scores])
            if use_sink:
                sk2 = sink_ref[layer * Q_HEADS + j * Q_PER_KV + g] * LOG2E
                m = jnp.maximum(m, sk2)
            acc = None
            for s, (k0, rows) in zip(scores, slabs):
                p = jnp.exp2(s - m).astype(BF16)
                part = jnp.dot(p, vall[pl.ds(k0, rows), :], preferred_element_type=F32)
                acc = part if acc is None else acc + part
            den = acc[:, HEAD_DIM:2 * HEAD_DIM]
            if use_sink:
                den = den + jnp.exp2(sk2 - m)
            o_ref[:, q_cols] = (acc[:, 0:HEAD_DIM] * (1.0 / den)).astype(BF16)


def attention(pr, layer, *, latent, mixer, ctx_k=None, ctx_v=None, sink=None, q_gain=None, k_gain=None,
              rope_tabs=None, kv_prev=None):
    n_seq, seq_len, row0 = (DEC_BATCH, DEC_SEQ, T_CTX) if latent else (BATCH, SEQ, 0)
    n_ctx = PAST_LEN if latent else 0
    base = 0 if mixer == "a" else R1_COLS
    use_sink = mixer == "a"
    qk_norm = mixer == "c"
    band = latent and mixer == "a"
    rope = latent
    emit_kv = not latent
    tq = ATT_TQ
    qblocks = seq_len // tq
    kvs = KV_HEADS
    grid = (n_seq, KV_HEADS // kvs, qblocks)
    qw_kv = kvs * Q_PER_KV * HEAD_DIM
    kw = kvs * HEAD_DIM

    in_specs = [
        pl.BlockSpec((tq, qw_kv), lambda b, j, n: (row0 // tq + b * qblocks + n, (base + OFF_Q) // qw_kv + j)),
        pl.BlockSpec((seq_len, kw), lambda b, j, n: (row0 // seq_len + b, (base + OFF_K) // kw + j)),
        pl.BlockSpec((seq_len, kw), lambda b, j, n: (row0 // seq_len + b, (base + OFF_V) // kw + j)),
    ]
    args = [pr, pr, pr]
    if n_ctx:
        cache_spec = pl.BlockSpec((None, None, PAST_LEN * KV_HEADS, HEAD_DIM), lambda b, j, n: (b, layer, 0, 0))
        in_specs += [cache_spec, cache_spec]
        args += [ctx_k.reshape(DEC_BATCH, DEPTH, PAST_LEN * KV_HEADS, HEAD_DIM),
                 ctx_v.reshape(DEC_BATCH, DEPTH, PAST_LEN * KV_HEADS, HEAD_DIM)]
    if use_sink:
        in_specs.append(pl.BlockSpec(memory_space=pltpu.SMEM))
        args.append(sink.reshape(DEPTH * Q_HEADS))
    if qk_norm:
        gain_spec = pl.BlockSpec((None, 1, HEAD_DIM), lambda b, j, n: (layer, 0, 0))
        in_specs += [gain_spec, gain_spec]
        args += [q_gain.reshape(DEPTH, 1, HEAD_DIM), k_gain.reshape(DEPTH, 1, HEAD_DIM)]
    if rope:
        cos2, sin2 = rope_tabs
        in_specs += [pl.BlockSpec((tq, HEAD_DIM), lambda b, j, n: (n, 0))] * 2
        in_specs += [pl.BlockSpec((seq_len, HEAD_DIM), lambda b, j, n: (0, 0))] * 2
        args += [cos2, sin2, cos2, sin2]
    if emit_kv and layer:
        prev_spec = pl.BlockSpec((None, layer, SEQ * KV_HEADS, HEAD_DIM), lambda b, j, n: (b, 0, 0, 0))
        in_specs += [prev_spec, prev_spec]
        args += list(kv_prev)

    out_specs = [pl.BlockSpec((tq, qw_kv), lambda b, j, n: (b * qblocks + n, j))]
    out_shape = [jax.ShapeDtypeStruct((n_seq * seq_len, QW), BF16)]
    if emit_kv:
        kv_spec = pl.BlockSpec((None, layer + 1, SEQ * KV_HEADS, HEAD_DIM), lambda b, j, n: (b, 0, 0, 0))
        out_specs += [kv_spec, kv_spec]
        out_shape += [jax.ShapeDtypeStruct((BATCH, layer + 1, SEQ * KV_HEADS, HEAD_DIM), F32)] * 2

    return pl.pallas_call(
        functools.partial(_attn_body, layer=layer, tq=tq, kvs=kvs, n_ctx=n_ctx, seq_len=seq_len, use_sink=use_sink, band=band,
                          qk_norm=qk_norm, rope=rope, emit_kv=emit_kv),
        grid=grid,
        in_specs=in_specs,
        out_specs=out_specs,
        out_shape=out_shape,
        scratch_shapes=[pltpu.VMEM((kvs, n_ctx + seq_len, HEAD_DIM), BF16),
                        pltpu.VMEM((kvs, n_ctx + seq_len, 2 * HEAD_DIM), BF16)],
        compiler_params=_cparams(3),
        name=f"attn_{mixer}_{'lat' if latent else 'ctx'}",
    )(*args)


SSD_COLBLK = 512
N_Z_BLK = SSM_INNER // SSD_COLBLK
N_XBC_BLK = CONV_CH // SSD_COLBLK
HALO = 16
GROUP_W = SSM_INNER // SSM_GROUPS
PAIRS = SSM_HEADS // 2
PAIRS_PER_GROUP = PAIRS // SSM_GROUPS


def _softplus(x):
    return jnp.maximum(x, 0.0) + jnp.log1p(jnp.exp(-jnp.abs(x)))


def _split3(x):
    hi = x.astype(BF16)
    r = x - hi.astype(F32)
    mid = r.astype(BF16)
    lo = (r - mid.astype(F32)).astype(BF16)
    return hi, mid, lo


def _ssd_body(*refs, layer, seq_len, has_h0, emit_state):
    it = iter(refs)
    z_refs = [next(it) for _ in range(N_Z_BLK)]
    xbc_refs = [next(it) for _ in range(N_XBC_BLK)]
    dt_ref, convw_ref, convb_ref, alog_ref, dtb_ref, dexp_ref, ng_ref = (next(it) for _ in range(7))
    h0_ref = next(it) if has_h0 else None
    stprev_ref = next(it) if emit_state and layer else None
    y_ref = next(it)
    st_ref = next(it) if emit_state else None
    conv_scr, y_scr, h_scr = next(it), next(it), next(it)

    n_chunks = seq_len // SSD_CHUNK
    q = SSD_CHUNK
    lane = jax.lax.broadcasted_iota(jnp.int32, (q, LANE), 1)
    row = jax.lax.broadcasted_iota(jnp.int32, (q, LANE), 0)
    low_half = lane < HALF_LANE

    for d in range(2):
        for blk in range(SSM_INNER // LANE):
            cols = slice(blk * LANE, (blk + 1) * LANE)
            if has_h0:
                h_scr[d, :, cols] = h0_ref[d, cols, :].T
            else:
                h_scr[d, :, cols] = jnp.zeros((SSM_STATE, LANE), F32)

    def conv_chunk(c, carry):
        r0 = pl.multiple_of(c * q, q)
        prev0 = pl.multiple_of(jnp.maximum(r0 - HALO, 0), HALO)
        next0 = pl.multiple_of(jnp.minimum(r0 + q, seq_len - HALO), HALO)
        has_prev = (r0 > 0).astype(F32)
        has_next = (r0 + q < seq_len).astype(F32)
        for j in range(CONV_CH // LANE):
            src = xbc_refs[j // (SSD_COLBLK // LANE)]
            sc = slice((j % (SSD_COLBLK // LANE)) * LANE, (j % (SSD_COLBLK // LANE) + 1) * LANE)
            cols = slice(j * LANE, (j + 1) * LANE)
            u = src[pl.ds(r0, q), sc].astype(F32)
            prev_row = src[pl.ds(prev0, HALO), sc].astype(F32)[HALO - 1:HALO, :] * has_prev
            next_row = src[pl.ds(next0, HALO), sc].astype(F32)[0:1, :] * has_next
            up = jnp.where(row == 0, prev_row, pltpu.roll(u, 1, 0))
            dn = jnp.where(row == q - 1, next_row, pltpu.roll(u, q - 1, 0))
            hw = 0.5 * convw_ref[:, cols]
            hv = hw[0:1, :] * up + hw[1:2, :] * u + hw[2:3, :] * dn + 0.5 * convb_ref[:, cols]
            act = _silu_from_half(hv)
            conv_scr[pl.ds(r0, q), cols] = act.astype(BF16)
            if j < SSM_INNER // LANE:
                y_scr[pl.ds(r0, q), cols] = dexp_ref[:, cols] * act
        return carry

    jax.lax.fori_loop(0, n_chunks, conv_chunk, 0)

    def scan_chunk(c, d):
        r0 = pl.multiple_of(c * q, q)
        edge = q - 1 if d == 0 else 0
        vis = (row >= lane) if d == 0 else (row <= lane)
        tri = jnp.where(vis, 1.0, 0.0).astype(BF16)
        dt = _softplus(dt_ref[pl.ds(r0, q), :] + dtb_ref[...])
        a = dt * (-jnp.exp(alog_ref[...]))
        a_hi, a_mid, a_lo = _split3(a)
        acum = (jnp.dot(tri, a_hi, preferred_element_type=F32) + jnp.dot(tri, a_mid, preferred_element_type=F32)
                + jnp.dot(tri, a_lo, preferred_element_type=F32))
        acum2 = acum * LOG2E
        acum_t = acum.T
        dt_t = dt.T
        row_t2 = (acum_t - jnp.log(dt_t)) * LOG2E
        w_t = dt_t * jnp.exp(acum_t[:, edge:edge + 1] - acum_t)
        for g in range(SSM_GROUPS):
            b_g = conv_scr[pl.ds(r0, q), SSM_INNER + g * SSM_STATE:SSM_INNER + (g + 1) * SSM_STATE]
            c_lo = SSM_INNER + SSM_GROUPS * SSM_STATE + g * SSM_STATE
            c_g = conv_scr[pl.ds(r0, q), c_lo:c_lo + SSM_STATE]
            cb = jax.lax.dot_general(c_g, b_g, (((1,), (1,)), ((), ())),
                                     preferred_element_type=F32).astype(BF16)
            b_t = b_g.astype(F32).T.astype(BF16)
            h_g = h_scr[d, :, g * GROUP_W:(g + 1) * GROUP_W].astype(BF16)
            y_off = jnp.dot(c_g, h_g, preferred_element_type=F32)
            for pp in range(PAIRS_PER_GROUP):
                p = g * PAIRS_PER_GROUP + pp
                cols = slice(p * LANE, (p + 1) * LANE)
                lhs_top, lhs_bot, col_of = [], [], []
                for h in (2 * p, 2 * p + 1):
                    ell = d * SSM_HEADS + h
                    col = jnp.broadcast_to(acum2[:, ell:ell + 1], (q, LANE))
                    seg = jnp.exp2(jnp.where(vis, col - row_t2[ell:ell + 1, :], -jnp.inf))
                    lhs_top.append(cb * seg.astype(BF16))
                    lhs_bot.append(b_t * w_t[ell:ell + 1, :].astype(BF16))
                    col_of.append(col)
                lhs = jnp.concatenate([jnp.concatenate(lhs_top, axis=1), jnp.concatenate(lhs_bot, axis=1)],
                                      axis=0)
                xp = conv_scr[pl.ds(r0, q), cols]
                zero = jnp.zeros_like(xp)
                rhs = jnp.concatenate([jnp.where(low_half, xp, zero), jnp.where(low_half, zero, xp)], axis=0)
                res = jnp.dot(lhs, rhs, preferred_element_type=F32)
                factor = jnp.exp2(jnp.where(low_half, col_of[0], col_of[1]))
                y_scr[pl.ds(r0, q), cols] += res[0:q, :] + y_off[:, pp * LANE:(pp + 1) * LANE] * factor
                h_scr[d, :, cols] = h_scr[d, :, cols] * factor[edge:edge + 1, :] + res[q:2 * q, :]

    def scan_step(i, carry):
        scan_chunk(i, 0)
        scan_chunk(n_chunks - 1 - i, 1)
        return carry

    jax.lax.fori_loop(0, n_chunks, scan_step, 0)

    if emit_state:
        if layer:
            st_ref[0:layer] = stprev_ref[...]
        for d in range(2):
            for blk in range(SSM_INNER // LANE):
                cols = slice(blk * LANE, (blk + 1) * LANE)
                st_ref[layer, d, cols, :] = h_scr[d, :, cols].T

    def finish_chunk(c, carry):
        r0 = pl.multiple_of(c * q, q)
        gated = []
        for k in range(N_Z_BLK):
            cols = slice(k * SSD_COLBLK, (k + 1) * SSD_COLBLK)
            half_z = 0.5 * z_refs[k][pl.ds(r0, q), :].astype(F32)
            gated.append(y_scr[pl.ds(r0, q), cols] * _silu_from_half(half_z))
        ssq = sum(jnp.sum(gk * gk, axis=-1, keepdims=True) for gk in gated)
        inv = jax.lax.rsqrt(ssq * (1.0 / SSM_INNER) + EPS)
        for k in range(N_Z_BLK):
            cols = slice(k * SSD_COLBLK, (k + 1) * SSD_COLBLK)
            y_ref[pl.ds(r0, q), cols] = (gated[k] * inv * ng_ref[:, cols]).astype(BF16)
        return carry

    jax.lax.fori_loop(0, n_chunks, finish_chunk, 0)


def ssd_mixer(pr, dt_raw, layer, *, latent, conv_w, conv_b, a_log_row, dt_bias_row, d_row, norm_gain,
              h0=None, state_prev=None):
    n_seq, seq_len, row0 = (DEC_BATCH, DEC_SEQ, T_CTX) if latent else (BATCH, SEQ, 0)
    has_h0 = latent
    emit_state = not latent
    seq_blk = row0 // seq_len

    def col_spec(first, k):
        return pl.BlockSpec((seq_len, SSD_COLBLK), lambda b: (seq_blk + b, first // SSD_COLBLK + k))

    in_specs = [col_spec(OFF_Z, k) for k in range(N_Z_BLK)] + [col_spec(OFF_XBC, k) for k in range(N_XBC_BLK)]
    args = [pr] * (N_Z_BLK + N_XBC_BLK)
    in_specs += [
        pl.BlockSpec((seq_len, LANE), lambda b: (seq_blk + b, 0)),
        pl.BlockSpec((None, CONV_K, CONV_CH), lambda b: (layer, 0, 0)),
        pl.BlockSpec((None, 1, CONV_CH), lambda b: (layer, 0, 0)),
        pl.BlockSpec((None, 1, LANE), lambda b: (layer, 0, 0)),
        pl.BlockSpec((None, 1, LANE), lambda b: (layer, 0, 0)),
        pl.BlockSpec((None, 1, SSM_INNER), lambda b: (layer, 0, 0)),
        pl.BlockSpec((None, 1, SSM_INNER), lambda b: (layer, 0, 0)),
    ]
    args += [dt_raw, conv_w, conv_b.reshape(DEPTH, 1, CONV_CH), a_log_row, dt_bias_row, d_row,
             norm_gain.reshape(DEPTH, 1, SSM_INNER)]
    if has_h0:
        in_specs.append(pl.BlockSpec((None, None, 2, SSM_INNER, SSM_STATE), lambda b: (b, layer, 0, 0, 0)))
        args.append(h0.reshape(DEC_BATCH, DEPTH, 2, SSM_INNER, SSM_STATE))
    if emit_state and layer:
        in_specs.append(pl.BlockSpec((None, layer, 2, SSM_INNER, SSM_STATE), lambda b: (b, 0, 0, 0, 0)))
        args.append(state_prev)

    out_specs = [pl.BlockSpec((seq_len, SSM_INNER), lambda b: (b, 0))]
    out_shape = [jax.ShapeDtypeStruct((n_seq * seq_len, SSM_INNER), BF16)]
    if emit_state:
        out_specs.append(pl.BlockSpec((None, layer + 1, 2, SSM_INNER, SSM_STATE), lambda b: (b, 0, 0, 0, 0)))
        out_shape.append(jax.ShapeDtypeStruct((BATCH, layer + 1, 2, SSM_INNER, SSM_STATE), F32))

    return pl.pallas_call(
        functools.partial(_ssd_body, layer=layer, seq_len=seq_len, has_h0=has_h0, emit_state=emit_state),
        grid=(n_seq,),
        in_specs=in_specs,
        out_specs=out_specs,
        out_shape=out_shape,
        scratch_shapes=[
            pltpu.VMEM((seq_len, CONV_CH), BF16),
            pltpu.VMEM((seq_len, SSM_INNER), F32),
            pltpu.VMEM((2, SSM_STATE, SSM_INNER), F32),
        ],
        compiler_params=_cparams(1),
        name=f"ssd_{'lat' if latent else 'ctx'}",
    )(*args)


def _rope_tables():
    rows = DEC_SEQ // GRID_W
    r = jnp.repeat(jnp.arange(rows, dtype=F32), GRID_W)
    c = jnp.tile(jnp.arange(GRID_W, dtype=F32), rows)
    n_freq = HEAD_DIM // 4
    inv_freq = ROPE_THETA ** (-jnp.arange(n_freq, dtype=F32) / n_freq)
    ang = jnp.concatenate([r[:, None] * inv_freq, c[:, None] * inv_freq], axis=-1)
    cos, sin = jnp.cos(ang), jnp.sin(ang)
    return jnp.concatenate([cos, cos], axis=-1), jnp.concatenate([-sin, sin], axis=-1)


def kernel(x_prompt, x_sample, cache_a_k, cache_a_v, cache_c_k, cache_c_v, state_ssm, c, c_ctx, norm1_g, w_ada, b_ada, w_in, a_sink, conv_w, conv_b, ssm_a_log, ssm_dt_bias, ssm_d, ssm_norm_g, c_q_norm, c_k_norm, w_oa, w_ob, w_oc, w_out, norm2_g, w_mlp1, w_mlp2, final_norm_g):
    x = (x_prompt.reshape(T_CTX, D_MODEL), x_sample.reshape(T_LAT, D_MODEL))
    cond = jnp.concatenate([c, c_ctx[None, :], jnp.zeros((MOD_ROWS - DEC_BATCH - 1, D_MODEL), F32)], axis=0)
    mod = ada_modulation(cond, w_ada, b_ada).reshape(DEPTH, MOD_ROWS, N_MOD, 1, D_MODEL)
    rope_tabs = _rope_tables()
    lane_pad = jnp.zeros((DEPTH, LANE - 2 * SSM_HEADS), F32)
    a_log_row = jnp.concatenate([ssm_a_log.reshape(DEPTH, 2 * SSM_HEADS), lane_pad], axis=1).reshape(DEPTH, 1, LANE)
    dt_bias_row = jnp.concatenate([ssm_dt_bias.reshape(DEPTH, 2 * SSM_HEADS), lane_pad], axis=1).reshape(DEPTH, 1, LANE)
    d_row = jnp.repeat(ssm_d, SSM_HEAD_DIM, axis=1).reshape(DEPTH, 1, SSM_INNER)
    w_in_t = jnp.swapaxes(w_in, 1, 2)
    w_out_bf = w_out.astype(BF16)

    kv_a = kv_c = state = None
    for layer in range(DEPTH):
        h, dt_raw = prenorm(x, norm1_g, mod, layer, w_in_t)
        pr = in_projection(h, w_in_t, layer)

        ya_c, *kv_a = attention(pr, layer, latent=False, mixer="a", sink=a_sink, kv_prev=kv_a)
        (ya_l,) = attention(pr, layer, latent=True, mixer="a", sink=a_sink, ctx_k=cache_a_k, ctx_v=cache_a_v,
                            rope_tabs=rope_tabs)
        yc_c, *kv_c = attention(pr, layer, latent=False, mixer="c", q_gain=c_q_norm, k_gain=c_k_norm, kv_prev=kv_c)
        (yc_l,) = attention(pr, layer, latent=True, mixer="c", q_gain=c_q_norm, k_gain=c_k_norm, ctx_k=cache_c_k,
                            ctx_v=cache_c_v, rope_tabs=rope_tabs)
        ssd_args = dict(conv_w=conv_w, conv_b=conv_b, a_log_row=a_log_row, dt_bias_row=dt_bias_row, d_row=d_row,
                        norm_gain=ssm_norm_g)
        yb_c, state = ssd_mixer(pr, dt_raw, layer, latent=False, state_prev=state, **ssd_args)
        (yb_l,) = ssd_mixer(pr, dt_raw, layer, latent=True, h0=state_ssm, **ssd_args)

        merged = (merge_branches(ya_c, yb_c, yc_c, pr, w_oa, w_ob, w_oc, layer, latent=False),
                  merge_branches(ya_l, yb_l, yc_l, pr, w_oa, w_ob, w_oc, layer, latent=True))
        x1, h2 = out_projection_norm(merged, x, w_out_bf, norm2_g, mod, layer)
        x = mlp_down(mlp_up(h2, w_mlp1, layer), x1, w_mlp2, mod, layer)

    y_prompt, y_sample = final_norm(x, final_norm_g)
    kv5 = (BATCH, DEPTH, SEQ, KV_HEADS, HEAD_DIM)
    return (y_prompt.reshape(BATCH, SEQ, D_MODEL), y_sample.reshape(DEC_BATCH, DEC_SEQ, D_MODEL),
            kv_a[0].reshape(kv5), kv_a[1].reshape(kv5), kv_c[0].reshape(kv5), kv_c[1].reshape(kv5),
            state.reshape(BATCH, DEPTH, 2, SSM_HEADS, SSM_HEAD_DIM, SSM_STATE))
```
